```python
import jax
import jax.numpy as jnp
from jax import lax
import numpy as np

D_MODEL = 2048
BATCH = 2
SEQ = 8192
DEPTH = 1

MIX_WIDTH = D_MODEL
HGRN_WIDTH = MIX_WIDTH // 2
HGRN_HEAD_DIM = 128
HGRN_HEADS = HGRN_WIDTH // HGRN_HEAD_DIM
HGRN_CHUNK = 64
ATTN_WIDTH = MIX_WIDTH - HGRN_WIDTH
ATTN_HEAD_DIM = 64
ATTN_Q_HEADS = ATTN_WIDTH // ATTN_HEAD_DIM
ATTN_KV_HEADS = 4
ATTN_GROUP = ATTN_Q_HEADS // ATTN_KV_HEADS
WINDOW = 128
ATTN_BLOCK = 128
REL_BUCKETS = 32
REL_MAX_DIST = 128
D_FF = -(-8 * D_MODEL // (3 * 256)) * 256
KV_WIDTH = ATTN_KV_HEADS * ATTN_HEAD_DIM
IN_SPLITS = (HGRN_WIDTH, HGRN_WIDTH, HGRN_WIDTH, HGRN_WIDTH, ATTN_WIDTH, KV_WIDTH, KV_WIDTH)
IN_WIDTH = sum(IN_SPLITS)
EPS = 1e-6
NEG_INF = -1e30

kernel_name = "hymba_hgrn2_swa_sink_adaln"


def rms_norm(x, w):
    xf = x.astype(jnp.float32)
    y = xf * lax.rsqrt(jnp.mean(xf * xf, axis=-1, keepdims=True) + EPS)
    return (y * w.astype(jnp.float32)).astype(x.dtype)


def t5_causal_buckets(dist):
    max_exact = REL_BUCKETS // 2
    d = np.maximum(dist, 0)
    log_b = max_exact + (np.log(np.maximum(d, 1) / max_exact)
                         / np.log(REL_MAX_DIST / max_exact)
                         * (REL_BUCKETS - max_exact)).astype(np.int32)
    log_b = np.minimum(log_b, REL_BUCKETS - 1)
    return np.where(d < max_exact, d, log_b).astype(np.int32)


def hgrn2_chunkwise(q, f_logit, i_in, lb):
    B, S, _ = q.shape
    C, H, dk = HGRN_CHUNK, HGRN_HEADS, HGRN_HEAD_DIM
    N = S // C
    f = lb + (1.0 - lb) * jax.nn.sigmoid(f_logit.astype(jnp.float32))
    k = 1.0 - f
    log_f = jnp.log(f)

    def heads(t):
        return t.astype(jnp.float32).reshape(B, N, C, H, dk)

    qc, kc, vc, gc = heads(q), heads(k), heads(i_in), heads(log_f)
    b = jnp.cumsum(gc, axis=2)
    b_mid = b[:, :, C // 2 - 1:C // 2]
    q_rel = qc * jnp.exp(b - b_mid)
    k_rel = kc * jnp.exp(b_mid - b)
    causal = jnp.asarray(np.tril(np.ones((C, C), dtype=bool)))
    a = jnp.einsum('bnthd,bnshd->bnhts', q_rel, k_rel)
    a = jnp.where(causal, a, 0.0)
    o_intra = jnp.einsum('bnhts,bnshe->bnthe', a, vc)

    b_last = b[:, :, -1]
    k_dec = kc * jnp.exp(b_last[:, :, None] - b)
    kv = jnp.einsum('bnshd,bnshe->bnhde', k_dec, vc)
    decay = jnp.exp(b_last)

    def step(state, inp):
        dec, kv_n = inp
        return dec[..., None] * state + kv_n, state

    s0 = jnp.zeros((B, H, dk, dk), jnp.float32)
    _, s_prev = lax.scan(step, s0, (jnp.moveaxis(decay, 1, 0), jnp.moveaxis(kv, 1, 0)))
    s_prev = jnp.moveaxis(s_prev, 0, 1)
    o_inter = jnp.einsum('bnthd,bnhde->bnthe', qc * jnp.exp(b), s_prev)
    return (o_intra + o_inter).reshape(B, S, H, dk)


def sliding_window_gqa(q, k, v, sinks, rel_table):
    B, S, _ = q.shape
    L, Hkv, G, dh = ATTN_BLOCK, ATTN_KV_HEADS, ATTN_GROUP, ATTN_HEAD_DIM
    NB = S // L
    qb = q.reshape(B, NB, L, Hkv, G, dh)
    pad = ((0, 0), (L, 0), (0, 0))
    kp = jnp.pad(k, pad).reshape(B, NB + 1, L, Hkv, dh)
    vp = jnp.pad(v, pad).reshape(B, NB + 1, L, Hkv, dh)
    kb = jnp.concatenate([kp[:, :-1], kp[:, 1:]], axis=2)
    vb = jnp.concatenate([vp[:, :-1], vp[:, 1:]], axis=2)

    scores = jnp.einsum('bnqhgd,bnkhd->bnhgqk', qb, kb).astype(jnp.float32) * (dh ** -0.5)
    qi = np.arange(L)[:, None]
    kj = np.arange(2 * L)[None, :]
    dist = qi + L - kj
    in_window = (dist >= 0) & (dist < WINDOW)
    key_pos = np.arange(NB)[:, None] * L - L + np.arange(2 * L)[None, :]
    valid = in_window[None] & (key_pos >= 0)[:, None, :]
    bias = rel_table[t5_causal_buckets(dist)].astype(jnp.float32)
    bias = jnp.transpose(bias, (2, 0, 1)).reshape(Hkv, G, L, 2 * L)
    scores = jnp.where(jnp.asarray(valid)[None, :, None, None], scores + bias, NEG_INF)
    sink = sinks.astype(jnp.float32).reshape(Hkv, G)[None, None, :, :, None, None]
    sink = jnp.broadcast_to(sink, scores.shape[:-1] + (1,))
    probs = jax.nn.softmax(jnp.concatenate([scores, sink], axis=-1), axis=-1)[..., :-1]
    out = jnp.einsum('bnhgqk,bnkhd->bnqhgd', probs.astype(v.dtype), vb)
    return out.reshape(B, S, Hkv * G * dh)


def setup_inputs(seed: int = 0) -> dict:
    key = jax.random.key(seed)
    ks = jax.random.split(key, 16)

    def nrm(k, shape, scale):
        return jax.random.normal(k, shape, jnp.float32) * scale

    return {
        "x": nrm(ks[0], (BATCH, SEQ, D_MODEL), 1.0),
        "c": nrm(ks[1], (BATCH, D_MODEL), 1.0),
        "w_ada": nrm(ks[2], (DEPTH, D_MODEL, 6 * D_MODEL), 0.5 * D_MODEL ** -0.5),
        "b_ada": nrm(ks[3], (DEPTH, 6 * D_MODEL), 0.01),
        "norm1_w": 1.0 + nrm(ks[4], (DEPTH, D_MODEL), 0.02),
        "w_in": nrm(ks[5], (DEPTH, D_MODEL, IN_WIDTH), D_MODEL ** -0.5),
        "lower_bounds": nrm(ks[6], (DEPTH + 1, HGRN_WIDTH), 0.5),
        "hgrn_norm_w": 1.0 + nrm(ks[7], (DEPTH, HGRN_WIDTH), 0.02),
        "attn_sinks": nrm(ks[8], (DEPTH, ATTN_Q_HEADS), 1.0),
        "rel_bias_table": nrm(ks[9], (REL_BUCKETS, ATTN_Q_HEADS), 0.5),
        "w_out": nrm(ks[10], (DEPTH, MIX_WIDTH, D_MODEL), MIX_WIDTH ** -0.5),
        "norm2_w": 1.0 + nrm(ks[11], (DEPTH, D_MODEL), 0.02),
        "w_gate": nrm(ks[12], (DEPTH, D_MODEL, D_FF), D_MODEL ** -0.5),
        "w_up": nrm(ks[13], (DEPTH, D_MODEL, D_FF), D_MODEL ** -0.5),
        "w_down": nrm(ks[14], (DEPTH, D_FF, D_MODEL), D_FF ** -0.5),
        "final_norm_w": 1.0 + nrm(ks[15], (D_MODEL,), 0.02),
    }


def reference(x, c, w_ada, b_ada, norm1_w, w_in, lower_bounds, hgrn_norm_w, attn_sinks,
              rel_bias_table, w_out, norm2_w, w_gate, w_up, w_down, final_norm_w):
    B, S, _ = x.shape
    offsets = list(np.cumsum(IN_SPLITS)[:-1])
    lb_all = jnp.cumsum(jax.nn.softmax(lower_bounds.astype(jnp.float32), axis=0), axis=0)
    c_act = jax.nn.silu(c)
    for layer in range(DEPTH):
        mod = c_act @ w_ada[layer] + b_ada[layer]
        shift1, scale1, gate1, shift2, scale2, gate2 = jnp.split(mod[:, None, :], 6, axis=-1)

        h = rms_norm(x, norm1_w[layer]) * (1 + scale1) + shift1
        proj = h @ w_in[layer]
        hq, hf, hi, hg, aq, ak, av = jnp.split(proj, offsets, axis=-1)

        o_rec = hgrn2_chunkwise(hq, hf, hi, lb_all[layer])
        o_rec = o_rec * lax.rsqrt(jnp.mean(o_rec * o_rec, axis=-1, keepdims=True) + EPS)
        o_rec = o_rec.reshape(B, S, HGRN_WIDTH) * hgrn_norm_w[layer].astype(jnp.float32)
        o_rec = (o_rec * jax.nn.silu(hg.astype(jnp.float32))).astype(x.dtype)

        o_att = sliding_window_gqa(aq, ak, av, attn_sinks[layer], rel_bias_table)

        mixed = jnp.concatenate([o_rec, o_att], axis=-1)
        x = x + gate1 * (mixed @ w_out[layer])

        h = rms_norm(x, norm2_w[layer]) * (1 + scale2) + shift2
        ffn = (jax.nn.silu(h @ w_gate[layer]) * (h @ w_up[layer])) @ w_down[layer]
        x = x + gate2 * ffn
    return rms_norm(x, final_norm_w)
```

```python
import functools

import numpy as np
import jax
import jax.numpy as jnp
from jax import lax
from jax.experimental import pallas as pl
from jax.experimental.pallas import tpu as pltpu

D_MODEL = 2048
DEPTH = 1
HGRN_WIDTH = 1024
HGRN_HEAD_DIM = 128
HGRN_HEADS = HGRN_WIDTH // HGRN_HEAD_DIM
HGRN_CHUNK = 64
ATTN_WIDTH = 1024
ATTN_HEAD_DIM = 64
ATTN_Q_HEADS = ATTN_WIDTH // ATTN_HEAD_DIM
ATTN_KV_HEADS = 4
ATTN_GROUP = ATTN_Q_HEADS // ATTN_KV_HEADS
WINDOW = 128
ATTN_BLOCK = 128
REL_BUCKETS = 32
REL_MAX_DIST = 128
D_FF = 5632
KV_WIDTH = ATTN_KV_HEADS * ATTN_HEAD_DIM
IN_WIDTH = 4 * HGRN_WIDTH + ATTN_WIDTH + 2 * KV_WIDTH
EPS = 1e-6
NEG_INF = -1e30

F32 = jnp.float32
BF16 = jnp.bfloat16

VMEM_LIMIT_BYTES = 56 * 1024 * 1024

ADA_TN = 1024
INPROJ_TM = 1024
INPROJ_TN = 512
HGRN_TC = 512
OUTPROJ_TM = 512
FFN_TM = 512
FFN_TF = 512
ROW_CHUNK = 256


def _params(*semantics):
    return pltpu.CompilerParams(dimension_semantics=semantics,
                                vmem_limit_bytes=VMEM_LIMIT_BYTES)


def _silu(v):
    return v * jax.nn.sigmoid(v)


def _dot(a, b):
    return jnp.dot(a, b, preferred_element_type=F32)


def _dot_nt(a, b):
    return lax.dot_general(a, b, (((1,), (1,)), ((), ())), preferred_element_type=F32)


def _dot_tn(a, b):
    return lax.dot_general(a, b, (((0,), (0,)), ((), ())), preferred_element_type=F32)


def _ada_kernel(c_ref, w_ref, b_ref, o_ref):
    c_act = _silu(c_ref[...])
    o_ref[...] = _dot(c_act.astype(BF16), w_ref[...].astype(BF16)) + b_ref[...]


def _ada(c8, w, b):
    n = w.shape[1]
    return pl.pallas_call(
        _ada_kernel,
        grid=(n // ADA_TN,),
        in_specs=[pl.BlockSpec((8, D_MODEL), lambda j: (0, 0)),
                  pl.BlockSpec((D_MODEL, ADA_TN), lambda j: (0, j)),
                  pl.BlockSpec((1, ADA_TN), lambda j: (0, j))],
        out_specs=pl.BlockSpec((8, ADA_TN), lambda j: (0, j)),
        out_shape=jax.ShapeDtypeStruct((8, n), F32),
        compiler_params=_params("arbitrary"),
        name="ada",
    )(c8, w, b)


def _modulated_norm(x, norm_w, scale, shift):
    y = x * lax.rsqrt(jnp.mean(x * x, axis=-1, keepdims=True) + EPS)
    return (y * norm_w) * (1.0 + scale) + shift


def _inproj_kernel(x_ref, mod_ref, nw_ref, w_ref, o_ref, h_ref):
    @pl.when(pl.program_id(2) == 0)
    def _():
        shift = mod_ref[0, 0:1, :]
        scale = mod_ref[0, 1:2, :]
        nw = nw_ref[...]

        def body(i, carry):
            rows = pl.ds(pl.multiple_of(i * ROW_CHUNK, ROW_CHUNK), ROW_CHUNK)
            h_ref[rows, :] = _modulated_norm(x_ref[0, rows, :], nw, scale, shift).astype(BF16)
            return carry

        lax.fori_loop(0, INPROJ_TM // ROW_CHUNK, body, 0)

    o_ref[0] = _dot(h_ref[...], w_ref[...])


def _inproj(x, mod, norm_w, w_bf):
    b, s, d = x.shape
    n = w_bf.shape[1]
    return pl.pallas_call(
        _inproj_kernel,
        grid=(b, s // INPROJ_TM, n // INPROJ_TN),
        in_specs=[pl.BlockSpec((1, INPROJ_TM, d), lambda bi, m, j: (bi, m, 0)),
                  pl.BlockSpec((1, 6, d), lambda bi, m, j: (bi, 0, 0)),
                  pl.BlockSpec((1, d), lambda bi, m, j: (0, 0)),
                  pl.BlockSpec((d, INPROJ_TN), lambda bi, m, j: (0, j))],
        out_specs=pl.BlockSpec((1, INPROJ_TM, INPROJ_TN), lambda bi, m, j: (bi, m, j)),
        out_shape=jax.ShapeDtypeStruct((b, s, n), F32),
        scratch_shapes=[pltpu.VMEM((INPROJ_TM, d), BF16)],
        compiler_params=_params("parallel", "parallel", "arbitrary"),
        name="inproj",
    )(x, mod, norm_w, w_bf)


def _split3(v):
    hi = v.astype(BF16)
    r1 = v - hi.astype(F32)
    mid = r1.astype(BF16)
    lo = (r1 - mid.astype(F32)).astype(BF16)
    return hi, mid, lo


def _hgrn_kernel(q_ref, f_ref, i_ref, g_ref, lb_ref, nw_ref, o_ref, st_ref, *, layer):
    c = HGRN_CHUNK
    dk = HGRN_HEAD_DIM

    @pl.when(pl.program_id(1) == 0)
    def _():
        st_ref[...] = jnp.zeros_like(st_ref)

    lb_rows = [lb_ref[r:r + 1, :] for r in range(lb_ref.shape[0])]
    lb_max = functools.reduce(jnp.maximum, lb_rows)
    lb_exp = [jnp.exp(r - lb_max) for r in lb_rows]
    lb_all = sum(lb_exp[:layer + 1]) / sum(lb_exp)

    row = lax.broadcasted_iota(jnp.int32, (c, c), 0)
    col = lax.broadcasted_iota(jnp.int32, (c, c), 1)
    causal = row >= col
    tril = jnp.where(causal, 1.0, 0.0).astype(BF16)

    def chunk_body(ci, carry):
        rows = pl.ds(pl.multiple_of(ci * c, c), c)
        for h in range(HGRN_HEADS):
            cols = slice(h * dk, (h + 1) * dk)
            lb = lb_all[:, cols]
            q = q_ref[0, rows, cols]
            v = i_ref[0, rows, cols].astype(BF16)
            g = g_ref[0, rows, cols]
            f = lb + (1.0 - lb) * jax.nn.sigmoid(f_ref[0, rows, cols])
            k = 1.0 - f
            hi, mid, lo = _split3(jnp.log(f))
            b = _dot(tril, hi) + _dot(tril, mid) + _dot(tril, lo)
            b_mid = b[c // 2 - 1:c // 2, :]
            b_last = b[c - 1:c, :]
            q_rel = (q * jnp.exp(b - b_mid)).astype(BF16)
            k_rel = (k * jnp.exp(b_mid - b)).astype(BF16)
            a = jnp.where(causal, _dot_nt(q_rel, k_rel), 0.0).astype(BF16)
            o = _dot(a, v)
            st = st_ref[h]
            o = o + _dot_nt((q * jnp.exp(b)).astype(BF16), st.astype(BF16))
            k_dec = (k * jnp.exp(b_last - b)).astype(BF16)
            st_ref[h] = jnp.exp(b_last) * st + _dot_tn(v, k_dec)
            o = o * lax.rsqrt(jnp.mean(o * o, axis=-1, keepdims=True) + EPS)
            o_ref[0, rows, cols] = (o * nw_ref[:, cols] * _silu(g)).astype(o_ref.dtype)
        return carry

    lax.fori_loop(0, HGRN_TC // c, chunk_body, 0)


def _hgrn(proj, lower_bounds, norm_w, layer):
    b, s, _ = proj.shape
    blk = lambda j: pl.BlockSpec((1, HGRN_TC, HGRN_WIDTH), lambda bi, t, j=j: (bi, t, j))
    return pl.pallas_call(
        functools.partial(_hgrn_kernel, layer=layer),
        grid=(b, s // HGRN_TC),
        in_specs=[blk(0), blk(1), blk(2), blk(3),
                  pl.BlockSpec(lower_bounds.shape, lambda bi, t: (0, 0)),
                  pl.BlockSpec((1, HGRN_WIDTH), lambda bi, t: (0, 0))],
        out_specs=pl.BlockSpec((1, HGRN_TC, HGRN_WIDTH), lambda bi, t: (bi, t, 0)),
        out_shape=jax.ShapeDtypeStruct((b, s, HGRN_WIDTH), BF16),
        scratch_shapes=[pltpu.VMEM((HGRN_HEADS, HGRN_HEAD_DIM, HGRN_HEAD_DIM), F32)],
        compiler_params=_params("parallel", "arbitrary"),
        name="hgrn",
    )(proj, proj, proj, proj, lower_bounds, norm_w)


def _t5_causal_buckets(dist):
    max_exact = REL_BUCKETS // 2
    d = np.maximum(dist, 0)
    log_b = max_exact + (np.log(np.maximum(d, 1) / max_exact)
                         / np.log(REL_MAX_DIST / max_exact)
                         * (REL_BUCKETS - max_exact)).astype(np.int32)
    log_b = np.minimum(log_b, REL_BUCKETS - 1)
    return np.where(d < max_exact, d, log_b).astype(np.int32)


def _bias_kernel(tab_ref, bucket_ref, valid_ref, o_ref):
    h = pl.program_id(0)
    bucket = bucket_ref[...]
    acc = jnp.zeros(bucket.shape, F32)
    for bk in range(REL_BUCKETS):
        acc = jnp.where(bucket == bk, tab_ref[bk, h], acc)
    for variant in range(2):
        o_ref[variant, 0] = jnp.where(valid_ref[variant] != 0, acc, NEG_INF)


def _bias_table(rel_table):
    l = ATTN_BLOCK
    qi = np.arange(l)[:, None]
    kj = np.arange(2 * l)[None, :]
    dist = qi + l - kj
    in_window = (dist >= 0) & (dist < WINDOW)
    valid = np.stack([in_window & (kj >= l), in_window]).astype(np.int32)
    bucket = _t5_causal_buckets(dist)
    return pl.pallas_call(
        _bias_kernel,
        grid=(ATTN_Q_HEADS,),
        in_specs=[pl.BlockSpec(memory_space=pltpu.SMEM),
                  pl.BlockSpec((l, 2 * l), lambda h: (0, 0)),
                  pl.BlockSpec((2, l, 2 * l), lambda h: (0, 0, 0))],
        out_specs=pl.BlockSpec((2, 1, l, 2 * l), lambda h: (0, h, 0, 0)),
        out_shape=jax.ShapeDtypeStruct((2, ATTN_Q_HEADS, l, 2 * l), F32),
        compiler_params=_params("arbitrary"),
        name="bias",
    )(rel_table, jnp.asarray(bucket), jnp.asarray(valid))


def _attn_kernel(sink_ref, q_ref, kp_ref, kc_ref, vp_ref, vc_ref, bias_ref, o_ref):
    l = ATTN_BLOCK
    dh = ATTN_HEAD_DIM
    lane = lax.broadcasted_iota(jnp.int32, (1, 2 * dh), 1)
    low = lane < dh

    for pair in range(ATTN_KV_HEADS // 2):
        pc = slice(pair * 2 * dh, (pair + 1) * 2 * dh)
        kk = jnp.concatenate([kp_ref[0, :, pc], kc_ref[0, :, pc]], axis=0)
        vv = jnp.concatenate([vp_ref[0, :, pc], vc_ref[0, :, pc]], axis=0)
        kk_sw = pltpu.roll(kk, dh, axis=1)
        vv_sw = pltpu.roll(vv, dh, axis=1)
        for sub in range(2):
            kvh = pair * 2 + sub
            if sub == 0:
                k2 = jnp.where(low, kk, kk_sw)
                v2 = jnp.where(low, vv, vv_sw)
            else:
                k2 = jnp.where(low, kk_sw, kk)
                v2 = jnp.where(low, vv_sw, vv)
            k2 = k2.astype(BF16)
            v2 = v2.astype(BF16)
            qs = []
            for j in range(ATTN_GROUP):
                hq = kvh * ATTN_GROUP + j
                qc = slice((hq // 2) * 2 * dh, (hq // 2 + 1) * 2 * dh)
                q2 = q_ref[0, :, qc] * (dh ** -0.5)
                keep = low if hq % 2 == 0 else jnp.logical_not(low)
                qs.append(jnp.where(keep, q2, 0.0).astype(BF16))
            s_all = _dot_nt(jnp.concatenate(qs, axis=0), k2)
            ps = []
            rden = []
            for j in range(ATTN_GROUP):
                hq = kvh * ATTN_GROUP + j
                s = s_all[j * l:(j + 1) * l] + bias_ref[0, hq]
                sink = sink_ref[hq]
                m = jnp.maximum(jnp.max(s, axis=-1, keepdims=True), sink)
                p = jnp.exp(s - m)
                den = jnp.sum(p, axis=-1, keepdims=True) + jnp.exp(sink - m)
                ps.append(p.astype(BF16))
                rden.append(1.0 / den)
            o_all = _dot(jnp.concatenate(ps, axis=0), v2)
            for jp in range(ATTN_GROUP // 2):
                hq = kvh * ATTN_GROUP + 2 * jp
                oe = o_all[(2 * jp) * l:(2 * jp + 1) * l] * rden[2 * jp]
                oo = o_all[(2 * jp + 1) * l:(2 * jp + 2) * l] * rden[2 * jp + 1]
                oc = slice((hq // 2) * 2 * dh, (hq // 2 + 1) * 2 * dh)
                o_ref[0, :, oc] = jnp.where(low, oe, oo).astype(o_ref.dtype)


def _attn(proj, sinks, bias):
    b, s, _ = proj.shape
    l = ATTN_BLOCK
    q_blk = (4 * HGRN_WIDTH) // ATTN_WIDTH
    k_blk = (4 * HGRN_WIDTH + ATTN_WIDTH) // KV_WIDTH
    v_blk = k_blk + 1
    kv_spec = lambda blk, prev: pl.BlockSpec(
        (1, l, KV_WIDTH),
        (lambda bi, n: (bi, jnp.maximum(n - 1, 0), blk)) if prev else (lambda bi, n: (bi, n, blk)))
    return pl.pallas_call(
        _attn_kernel,
        grid=(b, s // l),
        in_specs=[pl.BlockSpec(memory_space=pltpu.SMEM),
                  pl.BlockSpec((1, l, ATTN_WIDTH), lambda bi, n: (bi, n, q_blk)),
                  kv_spec(k_blk, True), kv_spec(k_blk, False),
                  kv_spec(v_blk, True), kv_spec(v_blk, False),
                  pl.BlockSpec((1, ATTN_Q_HEADS, l, 2 * l),
                               lambda bi, n: (jnp.minimum(n, 1), 0, 0, 0))],
        out_specs=pl.BlockSpec((1, l, ATTN_WIDTH), lambda bi, n: (bi, n, 0)),
        out_shape=jax.ShapeDtypeStruct((b, s, ATTN_WIDTH), BF16),
        compiler_params=_params("parallel", "arbitrary"),
        name="attn",
    )(sinks, proj, proj, proj, proj, proj, bias)


def _outproj_kernel(orec_ref, oatt_ref, x_ref, mod_ref, nw_ref, w_ref, x1_ref, h2_ref, y_ref):
    y_ref[...] = (_dot(orec_ref[0], w_ref[0:HGRN_WIDTH, :])
                  + _dot(oatt_ref[0], w_ref[HGRN_WIDTH:, :]))
    gate = mod_ref[0, 2:3, :]
    shift = mod_ref[0, 3:4, :]
    scale = mod_ref[0, 4:5, :]
    nw = nw_ref[...]

    def body(i, carry):
        rows = pl.ds(pl.multiple_of(i * ROW_CHUNK, ROW_CHUNK), ROW_CHUNK)
        x1 = x_ref[0, rows, :] + gate * y_ref[rows, :]
        x1_ref[0, rows, :] = x1
        h2_ref[0, rows, :] = _modulated_norm(x1, nw, scale, shift).astype(BF16)
        return carry

    lax.fori_loop(0, OUTPROJ_TM // ROW_CHUNK, body, 0)


def _outproj(o_rec, o_att, x, mod, norm_w, w_bf):
    b, s, d = x.shape
    tm = OUTPROJ_TM
    return pl.pallas_call(
        _outproj_kernel,
        grid=(b, s // tm),
        in_specs=[pl.BlockSpec((1, tm, HGRN_WIDTH), lambda bi, m: (bi, m, 0)),
                  pl.BlockSpec((1, tm, ATTN_WIDTH), lambda bi, m: (bi, m, 0)),
                  pl.BlockSpec((1, tm, d), lambda bi, m: (bi, m, 0)),
                  pl.BlockSpec((1, 6, d), lambda bi, m: (bi, 0, 0)),
                  pl.BlockSpec((1, d), lambda bi, m: (0, 0)),
                  pl.BlockSpec(w_bf.shape, lambda bi, m: (0, 0), pipeline_mode=pl.Buffered(1))],
        out_specs=[pl.BlockSpec((1, tm, d), lambda bi, m: (bi, m, 0)),
                   pl.BlockSpec((1, tm, d), lambda bi, m: (bi, m, 0))],
        out_shape=[jax.ShapeDtypeStruct((b, s, d), F32),
                   jax.ShapeDtypeStruct((b, s, d), BF16)],
        scratch_shapes=[pltpu.VMEM((tm, d), F32)],
        compiler_params=_params("parallel", "parallel"),
        name="outproj",
    )(o_rec, o_att, x, mod, norm_w, w_bf)


def _ffn_kernel(h_ref, x1_ref, mod_ref, fw_ref, wg_ref, wu_ref, wd_ref, o_ref, acc_ref, *, final_norm):
    j = pl.program_id(2)
    h = h_ref[0]
    act = (_silu(_dot(h, wg_ref[...])) * _dot(h, wu_ref[...])).astype(BF16)
    part = _dot(act, wd_ref[...])

    @pl.when(j == 0)
    def _():
        acc_ref[...] = part

    @pl.when(j > 0)
    def _():
        acc_ref[...] += part

    @pl.when(j == pl.num_programs(2) - 1)
    def _():
        gate = mod_ref[0, 5:6, :]
        fw = fw_ref[...]

        def body(i, carry):
            rows = pl.ds(pl.multiple_of(i * ROW_CHUNK, ROW_CHUNK), ROW_CHUNK)
            x2 = x1_ref[0, rows, :] + gate * acc_ref[rows, :]
            if final_norm:
                x2 = x2 * lax.rsqrt(jnp.mean(x2 * x2, axis=-1, keepdims=True) + EPS) * fw
            o_ref[0, rows, :] = x2
            return carry

        lax.fori_loop(0, FFN_TM // ROW_CHUNK, body, 0)


def _ffn(h2, x1, mod, final_w, wg, wu, wd, final_norm):
    b, s, d = x1.shape
    tm, tf = FFN_TM, FFN_TF
    return pl.pallas_call(
        functools.partial(_ffn_kernel, final_norm=final_norm),
        grid=(b, s // tm, D_FF // tf),
        in_specs=[pl.BlockSpec((1, tm, d), lambda bi, m, j: (bi, m, 0)),
                  pl.BlockSpec((1, tm, d), lambda bi, m, j: (bi, m, 0)),
                  pl.BlockSpec((1, 6, d), lambda bi, m, j: (bi, 0, 0)),
                  pl.BlockSpec((1, d), lambda bi, m, j: (0, 0)),
                  pl.BlockSpec((d, tf), lambda bi, m, j: (0, j)),
                  pl.BlockSpec((d, tf), lambda bi, m, j: (0, j)),
                  pl.BlockSpec((tf, d), lambda bi, m, j: (j, 0))],
        out_specs=pl.BlockSpec((1, tm, d), lambda bi, m, j: (bi, m, 0)),
        out_shape=jax.ShapeDtypeStruct((b, s, d), F32),
        scratch_shapes=[pltpu.VMEM((tm, d), F32)],
        compiler_params=_params("parallel", "parallel", "arbitrary"),
        name="ffn",
    )(h2, x1, mod, final_w, wg, wu, wd)


def kernel(x, c, w_ada, b_ada, norm1_w, w_in, lower_bounds, hgrn_norm_w, attn_sinks,
           rel_bias_table, w_out, norm2_w, w_gate, w_up, w_down, final_norm_w):
    b, s, d = x.shape
    assert (d, w_in.shape[-1], w_gate.shape[-1]) == (D_MODEL, IN_WIDTH, D_FF)
    assert s % max(INPROJ_TM, HGRN_TC, OUTPROJ_TM, FFN_TM, ATTN_BLOCK) == 0
    depth = w_ada.shape[0]
    c8 = jnp.pad(c, ((0, 8 - b), (0, 0)))
    bias = _bias_table(rel_bias_table)
    for layer in range(depth):
        mod = _ada(c8, w_ada[layer], b_ada[layer][None, :])[:b].reshape(b, 6, d)
        proj = _inproj(x, mod, norm1_w[layer][None, :], w_in[layer].astype(BF16))
        o_rec = _hgrn(proj, lower_bounds, hgrn_norm_w[layer][None, :], layer)
        o_att = _attn(proj, attn_sinks[layer], bias)
        x1, h2 = _outproj(o_rec, o_att, x, mod, norm2_w[layer][None, :], w_out[layer].astype(BF16))
        x = _ffn(h2, x1, mod, final_norm_w[None, :], w_gate[layer].astype(BF16),
                 w_up[layer].astype(BF16), w_down[layer].astype(BF16),
                 final_norm=(layer == depth - 1))
    return x
```

```python
import functools

import numpy as np
import jax
import jax.numpy as jnp
from jax import lax
from jax.experimental import pallas as pl
from jax.experimental.pallas import tpu as pltpu

D_MODEL = 2048
DEPTH = 1
HGRN_WIDTH = 1024
HGRN_HEAD_DIM = 128
HGRN_HEADS = HGRN_WIDTH // HGRN_HEAD_DIM
HGRN_CHUNK = 64
ATTN_WIDTH = 1024
ATTN_HEAD_DIM = 64
ATTN_Q_HEADS = ATTN_WIDTH // ATTN_HEAD_DIM
ATTN_KV_HEADS = 4
ATTN_GROUP = ATTN_Q_HEADS // ATTN_KV_HEADS
WINDOW = 128
ATTN_BLOCK = 128
REL_BUCKETS = 32
REL_MAX_DIST = 128
D_FF = 5632
KV_WIDTH = ATTN_KV_HEADS * ATTN_HEAD_DIM
IN_WIDTH = 4 * HGRN_WIDTH + ATTN_WIDTH + 2 * KV_WIDTH
EPS = 1e-6
NEG_INF = -1e30

F32 = jnp.float32
BF16 = jnp.bfloat16

VMEM_LIMIT_BYTES = 56 * 1024 * 1024

ADA_TN = 1024
INPROJ_TM = 1024
INPROJ_TN = 512
HGRN_TC = 512
OUTPROJ_TM = 512
FFN_TM = 512
FFN_TF = 512
FFN_SUB = 256
ROW_CHUNK = 16
ROW_UNROLL = 8


def _params(*semantics):
    return pltpu.CompilerParams(dimension_semantics=semantics,
                                vmem_limit_bytes=VMEM_LIMIT_BYTES)


def _silu(v):
    return v * jax.nn.sigmoid(v)


def _dot(a, b):
    return jnp.dot(a, b, preferred_element_type=F32)


def _dot_nt(a, b):
    return lax.dot_general(a, b, (((1,), (1,)), ((), ())), preferred_element_type=F32)


def _dot_tn(a, b):
    return lax.dot_general(a, b, (((0,), (0,)), ((), ())), preferred_element_type=F32)


def _ada_kernel(c_ref, w_ref, b_ref, o_ref):
    c_act = _silu(c_ref[...])
    o_ref[...] = _dot(c_act.astype(BF16), w_ref[...].astype(BF16)) + b_ref[...]


def _ada(c8, w, b):
    n = w.shape[1]
    return pl.pallas_call(
        _ada_kernel,
        grid=(n // ADA_TN,),
        in_specs=[pl.BlockSpec((8, D_MODEL), lambda j: (0, 0)),
                  pl.BlockSpec((D_MODEL, ADA_TN), lambda j: (0, j)),
                  pl.BlockSpec((1, ADA_TN), lambda j: (0, j))],
        out_specs=pl.BlockSpec((8, ADA_TN), lambda j: (0, j)),
        out_shape=jax.ShapeDtypeStruct((8, n), F32),
        compiler_params=_params("arbitrary"),
        name="ada",
    )(c8, w, b)


def _modulated_norm(x, wmod, shift):
    return x * lax.rsqrt(jnp.mean(x * x, axis=-1, keepdims=True) + EPS) * wmod + shift


def _for_row_chunks(n_rows, fn):
    def body(i, carry):
        fn(pl.ds(pl.multiple_of(i * ROW_CHUNK, ROW_CHUNK), ROW_CHUNK))
        return carry

    lax.fori_loop(0, n_rows // ROW_CHUNK, body, 0, unroll=ROW_UNROLL)


def _inproj_kernel(x_ref, mod_ref, nw_ref, w_ref, o_ref, h_ref):
    @pl.when(pl.program_id(2) == 0)
    def _():
        shift = mod_ref[0, 0:1, :]
        wmod = nw_ref[...] * (1.0 + mod_ref[0, 1:2, :])

        def chunk(rows):
            h_ref[rows, :] = _modulated_norm(x_ref[0, rows, :], wmod, shift).astype(BF16)

        _for_row_chunks(INPROJ_TM, chunk)

    o_ref[0] = _dot(h_ref[...], w_ref[...])


def _inproj(x, mod, norm_w, w_bf):
    b, s, d = x.shape
    n = w_bf.shape[1]
    return pl.pallas_call(
        _inproj_kernel,
        grid=(b, s // INPROJ_TM, n // INPROJ_TN),
        in_specs=[pl.BlockSpec((1, INPROJ_TM, d), lambda bi, m, j: (bi, m, 0)),
                  pl.BlockSpec((1, 6, d), lambda bi, m, j: (bi, 0, 0)),
                  pl.BlockSpec((1, d), lambda bi, m, j: (0, 0)),
                  pl.BlockSpec((d, INPROJ_TN), lambda bi, m, j: (0, j))],
        out_specs=pl.BlockSpec((1, INPROJ_TM, INPROJ_TN), lambda bi, m, j: (bi, m, j)),
        out_shape=jax.ShapeDtypeStruct((b, s, n), F32),
        scratch_shapes=[pltpu.VMEM((INPROJ_TM, d), BF16)],
        compiler_params=_params("parallel", "parallel", "arbitrary"),
        name="inproj",
    )(x, mod, norm_w, w_bf)


def _split3(v):
    hi = v.astype(BF16)
    r1 = v - hi.astype(F32)
    mid = r1.astype(BF16)
    lo = (r1 - mid.astype(F32)).astype(BF16)
    return hi, mid, lo


def _hgrn_kernel(q_ref, f_ref, i_ref, g_ref, lb_ref, nw_ref, o_ref, st_ref, *, layer):
    c = HGRN_CHUNK
    dk = HGRN_HEAD_DIM

    @pl.when(pl.program_id(1) == 0)
    def _():
        st_ref[...] = jnp.zeros_like(st_ref)

    lb_rows = [lb_ref[r:r + 1, :] for r in range(lb_ref.shape[0])]
    lb_max = functools.reduce(jnp.maximum, lb_rows)
    lb_exp = [jnp.exp(r - lb_max) for r in lb_rows]
    lb_all = sum(lb_exp[:layer + 1]) / sum(lb_exp)

    row = lax.broadcasted_iota(jnp.int32, (c, c), 0)
    col = lax.broadcasted_iota(jnp.int32, (c, c), 1)
    causal = row >= col
    tril = jnp.where(causal, 1.0, 0.0).astype(BF16)
    tril3 = jnp.concatenate([tril, tril, tril], axis=1)
    heads = [slice(h * dk, (h + 1) * dk) for h in range(HGRN_HEADS)]

    def chunk_body(ci, carry):
        rows = pl.ds(pl.multiple_of(ci * c, c), c)
        f = lb_all + (1.0 - lb_all) * jax.nn.sigmoid(f_ref[0, rows, :])
        k = 1.0 - f
        parts = jnp.concatenate(_split3(jnp.log2(f)), axis=0)
        b = _dot(tril3, parts)
        b_mid = b[c // 2 - 1:c // 2, :]
        b_last = b[c - 1:c, :]
        q = q_ref[0, rows, :]
        q_rel = (q * jnp.exp2(b - b_mid)).astype(BF16)
        k_rel = (k * jnp.exp2(b_mid - b)).astype(BF16)
        q_dec = (q * jnp.exp2(b)).astype(BF16)
        k_dec = (k * jnp.exp2(b_last - b)).astype(BF16)
        decay = jnp.exp2(b_last)
        v = i_ref[0, rows, :].astype(BF16)
        gate = nw_ref[...] * _silu(g_ref[0, rows, :])
        a = [jnp.where(causal, _dot_nt(q_rel[:, hs], k_rel[:, hs]), 0.0).astype(BF16) for hs in heads]
        st = [st_ref[h] for h in range(HGRN_HEADS)]
        o = [_dot(a[h], v[:, hs]) + _dot_nt(q_dec[:, hs], st[h].astype(BF16))
             for h, hs in enumerate(heads)]
        for h, hs in enumerate(heads):
            st_ref[h] = decay[:, hs] * st[h] + _dot_tn(v[:, hs], k_dec[:, hs])
        for h, hs in enumerate(heads):
            oh = o[h] * lax.rsqrt(jnp.mean(o[h] * o[h], axis=-1, keepdims=True) + EPS)
            o_ref[0, rows, hs] = (oh * gate[:, hs]).astype(o_ref.dtype)
        return carry

    lax.fori_loop(0, HGRN_TC // c, chunk_body, 0)


def _hgrn(proj, lower_bounds, norm_w, layer):
    b, s, _ = proj.shape
    blk = lambda j: pl.BlockSpec((1, HGRN_TC, HGRN_WIDTH), lambda bi, t, j=j: (bi, t, j))
    return pl.pallas_call(
        functools.partial(_hgrn_kernel, layer=layer),
        grid=(b, s // HGRN_TC),
        in_specs=[blk(0), blk(1), blk(2), blk(3),
                  pl.BlockSpec(lower_bounds.shape, lambda bi, t: (0, 0)),
                  pl.BlockSpec((1, HGRN_WIDTH), lambda bi, t: (0, 0))],
        out_specs=pl.BlockSpec((1, HGRN_TC, HGRN_WIDTH), lambda bi, t: (bi, t, 0)),
        out_shape=jax.ShapeDtypeStruct((b, s, HGRN_WIDTH), BF16),
        scratch_shapes=[pltpu.VMEM((HGRN_HEADS, HGRN_HEAD_DIM, HGRN_HEAD_DIM), F32)],
        compiler_params=_params("parallel", "arbitrary"),
        name="hgrn",
    )(proj, proj, proj, proj, lower_bounds, norm_w)


def _t5_causal_buckets(dist):
    max_exact = REL_BUCKETS // 2
    d = np.maximum(dist, 0)
    log_b = max_exact + (np.log(np.maximum(d, 1) / max_exact)
                         / np.log(REL_MAX_DIST / max_exact)
                         * (REL_BUCKETS - max_exact)).astype(np.int32)
    log_b = np.minimum(log_b, REL_BUCKETS - 1)
    return np.where(d < max_exact, d, log_b).astype(np.int32)


def _bias_kernel(tab_ref, bucket_ref, valid_ref, o_ref):
    h = pl.program_id(0)
    bucket = bucket_ref[...]
    acc = jnp.zeros(bucket.shape, F32)
    for bk in range(REL_BUCKETS):
        acc = jnp.where(bucket == bk, tab_ref[bk, h], acc)
    for variant in range(2):
        o_ref[variant, 0] = jnp.where(valid_ref[variant] != 0, acc, NEG_INF)


def _bias_table(rel_table):
    l = ATTN_BLOCK
    qi = np.arange(l)[:, None]
    kj = np.arange(2 * l)[None, :]
    dist = qi + l - kj
    in_window = (dist >= 0) & (dist < WINDOW)
    valid = np.stack([in_window & (kj >= l), in_window]).astype(np.int32)
    bucket = _t5_causal_buckets(dist)
    return pl.pallas_call(
        _bias_kernel,
        grid=(ATTN_Q_HEADS,),
        in_specs=[pl.BlockSpec(memory_space=pltpu.SMEM),
                  pl.BlockSpec((l, 2 * l), lambda h: (0, 0)),
                  pl.BlockSpec((2, l, 2 * l), lambda h: (0, 0, 0))],
        out_specs=pl.BlockSpec((2, 1, l, 2 * l), lambda h: (0, h, 0, 0)),
        out_shape=jax.ShapeDtypeStruct((2, ATTN_Q_HEADS, l, 2 * l), F32),
        compiler_params=_params("arbitrary"),
        name="bias",
    )(rel_table, jnp.asarray(bucket), jnp.asarray(valid))


def _attn_kernel(sink_ref, q_ref, kp_ref, kc_ref, vp_ref, vc_ref, bias_ref, o_ref):
    l = ATTN_BLOCK
    dh = ATTN_HEAD_DIM
    lane = lax.broadcasted_iota(jnp.int32, (1, 2 * dh), 1)
    low = lane < dh

    for pair in range(ATTN_KV_HEADS // 2):
        pc = slice(pair * 2 * dh, (pair + 1) * 2 * dh)
        kk = jnp.concatenate([kp_ref[0, :, pc], kc_ref[0, :, pc]], axis=0)
        vv = jnp.concatenate([vp_ref[0, :, pc], vc_ref[0, :, pc]], axis=0)
        kk_sw = pltpu.roll(kk, dh, axis=1)
        vv_sw = pltpu.roll(vv, dh, axis=1)
        for sub in range(2):
            kvh = pair * 2 + sub
            if sub == 0:
                k2 = jnp.where(low, kk, kk_sw)
                v2 = jnp.where(low, vv, vv_sw)
            else:
                k2 = jnp.where(low, kk_sw, kk)
                v2 = jnp.where(low, vv_sw, vv)
            k2 = k2.astype(BF16)
            v2 = v2.astype(BF16)
            qs = []
            for j in range(ATTN_GROUP):
                hq = kvh * ATTN_GROUP + j
                qc = slice((hq // 2) * 2 * dh, (hq // 2 + 1) * 2 * dh)
                q2 = q_ref[0, :, qc] * (dh ** -0.5)
                keep = low if hq % 2 == 0 else jnp.logical_not(low)
                qs.append(jnp.where(keep, q2, 0.0).astype(BF16))
            s_all = _dot_nt(jnp.concatenate(qs, axis=0), k2)
            ps = []
            rden = []
            for j in range(ATTN_GROUP):
                hq = kvh * ATTN_GROUP + j
                s = s_all[j * l:(j + 1) * l] + bias_ref[0, hq]
                sink = sink_ref[hq]
                m = jnp.maximum(jnp.max(s, axis=-1, keepdims=True), sink)
                p = jnp.exp(s - m)
                den = jnp.sum(p, axis=-1, keepdims=True) + jnp.exp(sink - m)
                ps.append(p.astype(BF16))
                rden.append(1.0 / den)
            o_all = _dot(jnp.concatenate(ps, axis=0), v2)
            for jp in range(ATTN_GROUP // 2):
                hq = kvh * ATTN_GROUP + 2 * jp
                oe = o_all[(2 * jp) * l:(2 * jp + 1) * l] * rden[2 * jp]
                oo = o_all[(2 * jp + 1) * l:(2 * jp + 2) * l] * rden[2 * jp + 1]
                oc = slice((hq // 2) * 2 * dh, (hq // 2 + 1) * 2 * dh)
                o_ref[0, :, oc] = jnp.where(low, oe, oo).astype(o_ref.dtype)


def _attn(proj, sinks, bias):
    b, s, _ = proj.shape
    l = ATTN_BLOCK
    q_blk = (4 * HGRN_WIDTH) // ATTN_WIDTH
    k_blk = (4 * HGRN_WIDTH + ATTN_WIDTH) // KV_WIDTH
    v_blk = k_blk + 1
    kv_spec = lambda blk, prev: pl.BlockSpec(
        (1, l, KV_WIDTH),
        (lambda bi, n: (bi, jnp.maximum(n - 1, 0), blk)) if prev else (lambda bi, n: (bi, n, blk)))
    return pl.pallas_call(
        _attn_kernel,
        grid=(b, s // l),
        in_specs=[pl.BlockSpec(memory_space=pltpu.SMEM),
                  pl.BlockSpec((1, l, ATTN_WIDTH), lambda bi, n: (bi, n, q_blk)),
                  kv_spec(k_blk, True), kv_spec(k_blk, False),
                  kv_spec(v_blk, True), kv_spec(v_blk, False),
                  pl.BlockSpec((1, ATTN_Q_HEADS, l, 2 * l),
                               lambda bi, n: (jnp.minimum(n, 1), 0, 0, 0))],
        out_specs=pl.BlockSpec((1, l, ATTN_WIDTH), lambda bi, n: (bi, n, 0)),
        out_shape=jax.ShapeDtypeStruct((b, s, ATTN_WIDTH), BF16),
        compiler_params=_params("parallel", "arbitrary"),
        name="attn",
    )(sinks, proj, proj, proj, proj, proj, bias)


def _outproj_kernel(orec_ref, oatt_ref, x_ref, mod_ref, nw_ref, w_ref, x1_ref, h2_ref, y_ref):
    y_ref[...] = (_dot(orec_ref[0], w_ref[0:HGRN_WIDTH, :])
                  + _dot(oatt_ref[0], w_ref[HGRN_WIDTH:, :]))
    gate = mod_ref[0, 2:3, :]
    shift = mod_ref[0, 3:4, :]
    wmod = nw_ref[...] * (1.0 + mod_ref[0, 4:5, :])

    def chunk(rows):
        x1 = x_ref[0, rows, :] + gate * y_ref[rows, :]
        x1_ref[0, rows, :] = x1
        h2_ref[0, rows, :] = _modulated_norm(x1, wmod, shift).astype(BF16)

    _for_row_chunks(OUTPROJ_TM, chunk)


def _outproj(o_rec, o_att, x, mod, norm_w, w_bf):
    b, s, d = x.shape
    tm = OUTPROJ_TM
    return pl.pallas_call(
        _outproj_kernel,
        grid=(b, s // tm),
        in_specs=[pl.BlockSpec((1, tm, HGRN_WIDTH), lambda bi, m: (bi, m, 0)),
                  pl.BlockSpec((1, tm, ATTN_WIDTH), lambda bi, m: (bi, m, 0)),
                  pl.BlockSpec((1, tm, d), lambda bi, m: (bi, m, 0)),
                  pl.BlockSpec((1, 6, d), lambda bi, m: (bi, 0, 0)),
                  pl.BlockSpec((1, d), lambda bi, m: (0, 0)),
                  pl.BlockSpec(w_bf.shape, lambda bi, m: (0, 0), pipeline_mode=pl.Buffered(1))],
        out_specs=[pl.BlockSpec((1, tm, d), lambda bi, m: (bi, m, 0)),
                   pl.BlockSpec((1, tm, d), lambda bi, m: (bi, m, 0))],
        out_shape=[jax.ShapeDtypeStruct((b, s, d), F32),
                   jax.ShapeDtypeStruct((b, s, d), BF16)],
        scratch_shapes=[pltpu.VMEM((tm, d), F32)],
        compiler_params=_params("parallel", "parallel"),
        name="outproj",
    )(o_rec, o_att, x, mod, norm_w, w_bf)


def _ffn_kernel(h_ref, x1_ref, mod_ref, fw_ref, wg_ref, wu_ref, wd_ref, o_ref, acc_ref, *, final_norm):
    j = pl.program_id(2)

    @pl.when(j == 0)
    def _():
        acc_ref[...] = jnp.zeros_like(acc_ref)

    part = None
    for s in range(FFN_TF // FFN_SUB):
        cs = slice(s * FFN_SUB, (s + 1) * FFN_SUB)
        act = (_silu(_dot(h_ref[0], wg_ref[:, cs])) * _dot(h_ref[0], wu_ref[:, cs])).astype(BF16)
        d = _dot(act, wd_ref[cs, :])
        part = d if part is None else part + d
    acc_ref[...] += part

    @pl.when(j == pl.num_programs(2) - 1)
    def _():
        gate = mod_ref[0, 5:6, :]
        fw = fw_ref[...]

        def chunk(rows):
            x2 = x1_ref[0, rows, :] + gate * acc_ref[rows, :]
            if final_norm:
                x2 = x2 * lax.rsqrt(jnp.mean(x2 * x2, axis=-1, keepdims=True) + EPS) * fw
            o_ref[0, rows, :] = x2

        _for_row_chunks(FFN_TM, chunk)


def _ffn(h2, x1, mod, final_w, wg, wu, wd, final_norm):
    b, s, d = x1.shape
    tm, tf = FFN_TM, FFN_TF
    return pl.pallas_call(
        functools.partial(_ffn_kernel, final_norm=final_norm),
        grid=(b, s // tm, D_FF // tf),
        in_specs=[pl.BlockSpec((1, tm, d), lambda bi, m, j: (bi, m, 0)),
                  pl.BlockSpec((1, tm, d), lambda bi, m, j: (bi, m, 0)),
                  pl.BlockSpec((1, 6, d), lambda bi, m, j: (bi, 0, 0)),
                  pl.BlockSpec((1, d), lambda bi, m, j: (0, 0)),
                  pl.BlockSpec((d, tf), lambda bi, m, j: (0, j)),
                  pl.BlockSpec((d, tf), lambda bi, m, j: (0, j)),
                  pl.BlockSpec((tf, d), lambda bi, m, j: (j, 0))],
        out_specs=pl.BlockSpec((1, tm, d), lambda bi, m, j: (bi, m, 0)),
        out_shape=jax.ShapeDtypeStruct((b, s, d), F32),
        scratch_shapes=[pltpu.VMEM((tm, d), F32)],
        compiler_params=_params("parallel", "parallel", "arbitrary"),
        name="ffn",
    )(h2, x1, mod, final_w, wg, wu, wd)


def kernel(x, c, w_ada, b_ada, norm1_w, w_in, lower_bounds, hgrn_norm_w, attn_sinks,
           rel_bias_table, w_out, norm2_w, w_gate, w_up, w_down, final_norm_w):
    b, s, d = x.shape
    assert (d, w_in.shape[-1], w_gate.shape[-1]) == (D_MODEL, IN_WIDTH, D_FF)
    assert s % max(INPROJ_TM, HGRN_TC, OUTPROJ_TM, FFN_TM, ATTN_BLOCK) == 0
    depth = w_ada.shape[0]
    c8 = jnp.pad(c, ((0, 8 - b), (0, 0)))
    bias = _bias_table(rel_bias_table)
    for layer in range(depth):
        mod = _ada(c8, w_ada[layer], b_ada[layer][None, :])[:b].reshape(b, 6, d)
        proj = _inproj(x, mod, norm1_w[layer][None, :], w_in[layer].astype(BF16))
        o_rec = _hgrn(proj, lower_bounds, hgrn_norm_w[layer][None, :], layer)
        o_att = _attn(proj, attn_sinks[layer], bias)
        x1, h2 = _outproj(o_rec, o_att, x, mod, norm2_w[layer][None, :], w_out[layer].astype(BF16))
        x = _ffn(h2, x1, mod, final_norm_w[None, :], w_gate[layer].astype(BF16),
                 w_up[layer].astype(BF16), w_down[layer].astype(BF16),
                 final_norm=(layer == depth - 1))
    return x
```

```python
import functools

import numpy as np
import jax
import jax.numpy as jnp
from jax import lax
from jax.experimental import pallas as pl
from jax.experimental.pallas import tpu as pltpu

D_MODEL = 2048
DEPTH = 1
HGRN_WIDTH = 1024
HGRN_HEAD_DIM = 128
HGRN_HEADS = HGRN_WIDTH // HGRN_HEAD_DIM
HGRN_CHUNK = 64
ATTN_WIDTH = 1024
ATTN_HEAD_DIM = 64
ATTN_Q_HEADS = ATTN_WIDTH // ATTN_HEAD_DIM
ATTN_KV_HEADS = 4
ATTN_GROUP = ATTN_Q_HEADS // ATTN_KV_HEADS
WINDOW = 128
ATTN_BLOCK = 128
REL_BUCKETS = 32
REL_MAX_DIST = 128
D_FF = 5632
KV_WIDTH = ATTN_KV_HEADS * ATTN_HEAD_DIM
IN_WIDTH = 4 * HGRN_WIDTH + ATTN_WIDTH + 2 * KV_WIDTH
P32_WIDTH = 3 * HGRN_WIDTH
P16_WIDTH = HGRN_WIDTH + ATTN_WIDTH + 2 * KV_WIDTH
EPS = 1e-6
NEG_INF = -1e30

F32 = jnp.float32
BF16 = jnp.bfloat16

VMEM_LIMIT_BYTES = 56 * 1024 * 1024

ADA_TN = 1024
INPROJ_TM = 1024
INPROJ_TN = 512
INPROJ_N32 = P32_WIDTH // INPROJ_TN
HGRN_TC = 512
OUTPROJ_TM = 512
FFN_TM = 1024
FFN_TF = 512
FFN_SUB = 256
ROW_CHUNK = 16
ROW_UNROLL = 8


def _params(*semantics):
    return pltpu.CompilerParams(dimension_semantics=semantics,
                                vmem_limit_bytes=VMEM_LIMIT_BYTES)


def _silu(v):
    return v * jax.nn.sigmoid(v)


def _dot(a, b):
    return jnp.dot(a, b, preferred_element_type=F32)


def _dot_nt(a, b):
    return lax.dot_general(a, b, (((1,), (1,)), ((), ())), preferred_element_type=F32)


def _dot_tn(a, b):
    return lax.dot_general(a, b, (((0,), (0,)), ((), ())), preferred_element_type=F32)


def _ada_kernel(c_ref, w_ref, b_ref, o_ref):
    c_act = _silu(c_ref[...])
    o_ref[...] = _dot(c_act.astype(BF16), w_ref[...].astype(BF16)) + b_ref[...]


def _ada(c8, w, b):
    n = w.shape[1]
    return pl.pallas_call(
        _ada_kernel,
        grid=(n // ADA_TN,),
        in_specs=[pl.BlockSpec((8, D_MODEL), lambda j: (0, 0)),
                  pl.BlockSpec((D_MODEL, ADA_TN), lambda j: (0, j)),
                  pl.BlockSpec((1, ADA_TN), lambda j: (0, j))],
        out_specs=pl.BlockSpec((8, ADA_TN), lambda j: (0, j)),
        out_shape=jax.ShapeDtypeStruct((8, n), F32),
        compiler_params=_params("arbitrary"),
        name="ada",
    )(c8, w, b)


def _modulated_norm(x, wmod, shift):
    return x * lax.rsqrt(jnp.mean(x * x, axis=-1, keepdims=True) + EPS) * wmod + shift


def _for_row_chunks(n_rows, fn):
    def body(i, carry):
        fn(pl.ds(pl.multiple_of(i * ROW_CHUNK, ROW_CHUNK), ROW_CHUNK))
        return carry

    lax.fori_loop(0, n_rows // ROW_CHUNK, body, 0, unroll=ROW_UNROLL)


def _inproj_kernel(x_ref, mod_ref, nw_ref, w_ref, o32_ref, o16_ref, h_ref):
    j = pl.program_id(2)

    @pl.when(j == 0)
    def _():
        shift = mod_ref[0, 0:1, :]
        wmod = nw_ref[...] * (1.0 + mod_ref[0, 1:2, :])

        def chunk(rows):
            h_ref[rows, :] = _modulated_norm(x_ref[0, rows, :], wmod, shift).astype(BF16)

        _for_row_chunks(INPROJ_TM, chunk)

    @pl.when(j < INPROJ_N32)
    def _():
        o32_ref[0] = _dot(h_ref[...], w_ref[...])

    @pl.when(j >= INPROJ_N32)
    def _():
        o16_ref[0] = _dot(h_ref[...], w_ref[...]).astype(BF16)


def _inproj(x, mod, norm_w, w_bf):
    b, s, d = x.shape
    tm, tn = INPROJ_TM, INPROJ_TN
    n32, n16 = INPROJ_N32, P16_WIDTH // INPROJ_TN
    return pl.pallas_call(
        _inproj_kernel,
        grid=(b, s // tm, n32 + n16),
        in_specs=[pl.BlockSpec((1, tm, d), lambda bi, m, j: (bi, m, 0)),
                  pl.BlockSpec((1, 6, d), lambda bi, m, j: (bi, 0, 0)),
                  pl.BlockSpec((1, d), lambda bi, m, j: (0, 0)),
                  pl.BlockSpec((d, tn), lambda bi, m, j: (0, j))],
        out_specs=[pl.BlockSpec((1, tm, tn), lambda bi, m, j: (bi, m, jnp.minimum(j, n32 - 1))),
                   pl.BlockSpec((1, tm, tn), lambda bi, m, j: (bi, m, jnp.maximum(j - n32, 0)))],
        out_shape=[jax.ShapeDtypeStruct((b, s, P32_WIDTH), F32),
                   jax.ShapeDtypeStruct((b, s, P16_WIDTH), BF16)],
        scratch_shapes=[pltpu.VMEM((tm, d), BF16)],
        compiler_params=_params("arbitrary", "arbitrary", "arbitrary"),
        name="inproj",
    )(x, mod, norm_w, w_bf)


def _split_in_weight(w):
    hw = HGRN_WIDTH
    return jnp.concatenate([w[:, :2 * hw], w[:, 3 * hw:4 * hw], w[:, 2 * hw:3 * hw], w[:, 4 * hw:]],
                           axis=1).astype(BF16)


def _split3(v):
    hi = v.astype(BF16)
    r1 = v - hi.astype(F32)
    mid = r1.astype(BF16)
    lo = (r1 - mid.astype(F32)).astype(BF16)
    return hi, mid, lo


def _hgrn_kernel(q_ref, f_ref, i_ref, g_ref, lb_ref, nw_ref, o_ref, st_ref, *, layer):
    c = HGRN_CHUNK
    dk = HGRN_HEAD_DIM

    @pl.when(pl.program_id(1) == 0)
    def _():
        st_ref[...] = jnp.zeros_like(st_ref)

    lb_rows = [lb_ref[r:r + 1, :] for r in range(lb_ref.shape[0])]
    lb_max = functools.reduce(jnp.maximum, lb_rows)
    lb_exp = [jnp.exp(r - lb_max) for r in lb_rows]
    lb_all = sum(lb_exp[:layer + 1]) / sum(lb_exp)

    row = lax.broadcasted_iota(jnp.int32, (c, c), 0)
    col = lax.broadcasted_iota(jnp.int32, (c, c), 1)
    causal = row >= col
    tril = jnp.where(causal, 1.0, 0.0).astype(BF16)
    tril3 = jnp.concatenate([tril, tril, tril], axis=1)
    heads = [slice(h * dk, (h + 1) * dk) for h in range(HGRN_HEADS)]

    def chunk_body(ci, carry):
        rows = pl.ds(pl.multiple_of(ci * c, c), c)
        f = lb_all + (1.0 - lb_all) * jax.nn.sigmoid(f_ref[0, rows, :])
        k = 1.0 - f
        parts = jnp.concatenate(_split3(jnp.log2(f)), axis=0)
        b = _dot(tril3, parts)
        b_mid = b[c // 2 - 1:c // 2, :]
        b_last = b[c - 1:c, :]
        q = q_ref[0, rows, :]
        q_rel = (q * jnp.exp2(b - b_mid)).astype(BF16)
        k_rel = (k * jnp.exp2(b_mid - b)).astype(BF16)
        q_dec = (q * jnp.exp2(b)).astype(BF16)
        k_dec = (k * jnp.exp2(b_last - b)).astype(BF16)
        decay = jnp.exp2(b_last)
        v = i_ref[0, rows, :].astype(BF16)
        gate = nw_ref[...] * _silu(g_ref[0, rows, :])
        a = [jnp.where(causal, _dot_nt(q_rel[:, hs], k_rel[:, hs]), 0.0).astype(BF16) for hs in heads]
        st = [st_ref[h] for h in range(HGRN_HEADS)]
        o = [_dot(a[h], v[:, hs]) + _dot_nt(q_dec[:, hs], st[h].astype(BF16))
             for h, hs in enumerate(heads)]
        for h, hs in enumerate(heads):
            st_ref[h] = decay[:, hs] * st[h] + _dot_tn(v[:, hs], k_dec[:, hs])
        for h, hs in enumerate(heads):
            oh = o[h] * lax.rsqrt(jnp.mean(o[h] * o[h], axis=-1, keepdims=True) + EPS)
            o_ref[0, rows, hs] = (oh * gate[:, hs]).astype(o_ref.dtype)
        return carry

    lax.fori_loop(0, HGRN_TC // c, chunk_body, 0)


def _hgrn(p32, p16, lower_bounds, norm_w, layer):
    b, s, _ = p32.shape
    blk = lambda j: pl.BlockSpec((1, HGRN_TC, HGRN_WIDTH), lambda bi, t, j=j: (bi, t, j))
    return pl.pallas_call(
        functools.partial(_hgrn_kernel, layer=layer),
        grid=(b, s // HGRN_TC),
        in_specs=[blk(0), blk(1), blk(0), blk(2),
                  pl.BlockSpec(lower_bounds.shape, lambda bi, t: (0, 0)),
                  pl.BlockSpec((1, HGRN_WIDTH), lambda bi, t: (0, 0))],
        out_specs=pl.BlockSpec((1, HGRN_TC, HGRN_WIDTH), lambda bi, t: (bi, t, 0)),
        out_shape=jax.ShapeDtypeStruct((b, s, HGRN_WIDTH), BF16),
        scratch_shapes=[pltpu.VMEM((HGRN_HEADS, HGRN_HEAD_DIM, HGRN_HEAD_DIM), F32)],
        compiler_params=_params("parallel", "arbitrary"),
        name="hgrn",
    )(p32, p32, p16, p32, lower_bounds, norm_w)


def _t5_causal_buckets(dist):
    max_exact = REL_BUCKETS // 2
    d = np.maximum(dist, 0)
    log_b = max_exact + (np.log(np.maximum(d, 1) / max_exact)
                         / np.log(REL_MAX_DIST / max_exact)
                         * (REL_BUCKETS - max_exact)).astype(np.int32)
    log_b = np.minimum(log_b, REL_BUCKETS - 1)
    return np.where(d < max_exact, d, log_b).astype(np.int32)


def _bias_kernel(tab_ref, bucket_ref, valid_ref, o_ref):
    h = pl.program_id(0)
    bucket = bucket_ref[...]
    acc = jnp.zeros(bucket.shape, F32)
    for bk in range(REL_BUCKETS):
        acc = jnp.where(bucket == bk, tab_ref[bk, h], acc)
    for variant in range(2):
        o_ref[variant, 0] = jnp.where(valid_ref[variant] != 0, acc, NEG_INF)


def _bias_table(rel_table):
    l = ATTN_BLOCK
    qi = np.arange(l)[:, None]
    kj = np.arange(2 * l)[None, :]
    dist = qi + l - kj
    in_window = (dist >= 0) & (dist < WINDOW)
    valid = np.stack([in_window & (kj >= l), in_window]).astype(np.int32)
    bucket = _t5_causal_buckets(dist)
    return pl.pallas_call(
        _bias_kernel,
        grid=(ATTN_Q_HEADS,),
        in_specs=[pl.BlockSpec(memory_space=pltpu.SMEM),
                  pl.BlockSpec((l, 2 * l), lambda h: (0, 0)),
                  pl.BlockSpec((2, l, 2 * l), lambda h: (0, 0, 0))],
        out_specs=pl.BlockSpec((2, 1, l, 2 * l), lambda h: (0, h, 0, 0)),
        out_shape=jax.ShapeDtypeStruct((2, ATTN_Q_HEADS, l, 2 * l), F32),
        compiler_params=_params("arbitrary"),
        name="bias",
    )(rel_table, jnp.asarray(bucket), jnp.asarray(valid))


def _attn_kernel(sink_ref, q_ref, kp_ref, kc_ref, vp_ref, vc_ref, bias_ref, o_ref):
    l = ATTN_BLOCK
    dh = ATTN_HEAD_DIM
    lane = lax.broadcasted_iota(jnp.int32, (1, 2 * dh), 1)
    low = lane < dh

    for pair in range(ATTN_KV_HEADS // 2):
        pc = slice(pair * 2 * dh, (pair + 1) * 2 * dh)
        kk = pltpu.bitcast(jnp.concatenate([kp_ref[0, :, pc], kc_ref[0, :, pc]], axis=0), jnp.int32)
        vv = pltpu.bitcast(jnp.concatenate([vp_ref[0, :, pc], vc_ref[0, :, pc]], axis=0), jnp.int32)
        kk_sw = pltpu.roll(kk, dh, axis=1)
        vv_sw = pltpu.roll(vv, dh, axis=1)
        for sub in range(2):
            kvh = pair * 2 + sub
            if sub == 0:
                k2 = jnp.where(low, kk, kk_sw)
                v2 = jnp.where(low, vv, vv_sw)
            else:
                k2 = jnp.where(low, kk_sw, kk)
                v2 = jnp.where(low, vv_sw, vv)
            k2 = pltpu.bitcast(k2, BF16)
            v2 = pltpu.bitcast(v2, BF16)
            qs = []
            for j in range(ATTN_GROUP):
                hq = kvh * ATTN_GROUP + j
                qc = slice((hq // 2) * 2 * dh, (hq // 2 + 1) * 2 * dh)
                q2 = q_ref[0, :, qc] * (dh ** -0.5)
                keep = low if hq % 2 == 0 else jnp.logical_not(low)
                qs.append(jnp.where(keep, q2, 0.0).astype(BF16))
            s_all = _dot_nt(jnp.concatenate(qs, axis=0), k2)
            ps = []
            rden = []
            for j in range(ATTN_GROUP):
                hq = kvh * ATTN_GROUP + j
                s = s_all[j * l:(j + 1) * l] + bias_ref[0, hq]
                sink = sink_ref[hq]
                m = jnp.maximum(jnp.max(s, axis=-1, keepdims=True), sink)
                p = jnp.exp(s - m)
                den = jnp.sum(p, axis=-1, keepdims=True) + jnp.exp(sink - m)
                ps.append(p.astype(BF16))
                rden.append(1.0 / den)
            o_all = _dot(jnp.concatenate(ps, axis=0), v2)
            for jp in range(ATTN_GROUP // 2):
                hq = kvh * ATTN_GROUP + 2 * jp
                oe = o_all[(2 * jp) * l:(2 * jp + 1) * l] * rden[2 * jp]
                oo = o_all[(2 * jp + 1) * l:(2 * jp + 2) * l] * rden[2 * jp + 1]
                oc = slice((hq // 2) * 2 * dh, (hq // 2 + 1) * 2 * dh)
                o_ref[0, :, oc] = jnp.where(low, oe, oo).astype(o_ref.dtype)


def _attn(p16, sinks, bias):
    b, s, _ = p16.shape
    l = ATTN_BLOCK
    q_blk = HGRN_WIDTH // ATTN_WIDTH
    k_blk = (HGRN_WIDTH + ATTN_WIDTH) // KV_WIDTH
    v_blk = k_blk + 1
    kv_spec = lambda blk, prev: pl.BlockSpec(
        (1, l, KV_WIDTH),
        (lambda bi, n: (bi, jnp.maximum(n - 1, 0), blk)) if prev else (lambda bi, n: (bi, n, blk)))
    return pl.pallas_call(
        _attn_kernel,
        grid=(b, s // l),
        in_specs=[pl.BlockSpec(memory_space=pltpu.SMEM),
                  pl.BlockSpec((1, l, ATTN_WIDTH), lambda bi, n: (bi, n, q_blk)),
                  kv_spec(k_blk, True), kv_spec(k_blk, False),
                  kv_spec(v_blk, True), kv_spec(v_blk, False),
                  pl.BlockSpec((1, ATTN_Q_HEADS, l, 2 * l),
                               lambda bi, n: (jnp.minimum(n, 1), 0, 0, 0))],
        out_specs=pl.BlockSpec((1, l, ATTN_WIDTH), lambda bi, n: (bi, n, 0)),
        out_shape=jax.ShapeDtypeStruct((b, s, ATTN_WIDTH), BF16),
        compiler_params=_params("parallel", "arbitrary"),
        name="attn",
    )(sinks, p16, p16, p16, p16, p16, bias)


def _outproj_kernel(orec_ref, oatt_ref, x_ref, mod_ref, nw_ref, w_ref, x1_ref, h2_ref, y_ref):
    y_ref[...] = (_dot(orec_ref[0], w_ref[0:HGRN_WIDTH, :])
                  + _dot(oatt_ref[0], w_ref[HGRN_WIDTH:, :]))
    gate = mod_ref[0, 2:3, :]
    shift = mod_ref[0, 3:4, :]
    wmod = nw_ref[...] * (1.0 + mod_ref[0, 4:5, :])

    def chunk(rows):
        x1 = x_ref[0, rows, :] + gate * y_ref[rows, :]
        x1_ref[0, rows, :] = x1
        h2_ref[0, rows, :] = _modulated_norm(x1, wmod, shift).astype(BF16)

    _for_row_chunks(OUTPROJ_TM, chunk)


def _outproj(o_rec, o_att, x, mod, norm_w, w_bf):
    b, s, d = x.shape
    tm = OUTPROJ_TM
    return pl.pallas_call(
        _outproj_kernel,
        grid=(b, s // tm),
        in_specs=[pl.BlockSpec((1, tm, HGRN_WIDTH), lambda bi, m: (bi, m, 0)),
                  pl.BlockSpec((1, tm, ATTN_WIDTH), lambda bi, m: (bi, m, 0)),
                  pl.BlockSpec((1, tm, d), lambda bi, m: (bi, m, 0)),
                  pl.BlockSpec((1, 6, d), lambda bi, m: (bi, 0, 0)),
                  pl.BlockSpec((1, d), lambda bi, m: (0, 0)),
                  pl.BlockSpec(w_bf.shape, lambda bi, m: (0, 0), pipeline_mode=pl.Buffered(1))],
        out_specs=[pl.BlockSpec((1, tm, d), lambda bi, m: (bi, m, 0)),
                   pl.BlockSpec((1, tm, d), lambda bi, m: (bi, m, 0))],
        out_shape=[jax.ShapeDtypeStruct((b, s, d), F32),
                   jax.ShapeDtypeStruct((b, s, d), BF16)],
        scratch_shapes=[pltpu.VMEM((tm, d), F32)],
        compiler_params=_params("parallel", "parallel"),
        name="outproj",
    )(o_rec, o_att, x, mod, norm_w, w_bf)


def _ffn_kernel(h_ref, x1_hbm, mod_ref, fw_ref, wg_ref, wu_ref, wd_ref, o_ref, x1_buf, stash_ref, x1_sem,
                *, final_norm):
    bi, m, j = pl.program_id(0), pl.program_id(1), pl.program_id(2)

    def x1_copy():
        rows = pl.ds(pl.multiple_of(m * FFN_TM, FFN_TM), FFN_TM)
        return pltpu.make_async_copy(x1_hbm.at[bi, rows, :], x1_buf, x1_sem)

    @pl.when(j == 0)
    def _():
        x1_copy().start()
        o_ref[...] = jnp.zeros_like(o_ref)

    part = None
    for s in range(FFN_TF // FFN_SUB):
        cs = slice(s * FFN_SUB, (s + 1) * FFN_SUB)
        act = (_silu(_dot(h_ref[0], wg_ref[:, cs])) * _dot(h_ref[0], wu_ref[:, cs])).astype(BF16)
        d = _dot(act, wd_ref[cs, :])
        part = d if part is None else part + d
    o_ref[0] += part

    @pl.when(j == pl.num_programs(2) - 1)
    def _():
        x1_copy().wait()
        gate = mod_ref[0, 5:6, :]
        fw = fw_ref[...]
        group = ROW_CHUNK * ROW_UNROLL

        def body(i, carry):
            base = pl.multiple_of(i * group, group)
            scales = []
            for k in range(ROW_UNROLL):
                rows = pl.ds(base + k * ROW_CHUNK, ROW_CHUNK)
                x2 = x1_buf[rows, :] + gate * o_ref[0, rows, :]
                stash_ref[k * ROW_CHUNK:(k + 1) * ROW_CHUNK, :] = x2
                if final_norm:
                    scales.append(lax.rsqrt(jnp.mean(x2 * x2, axis=-1, keepdims=True) + EPS))
            for k in range(ROW_UNROLL):
                rows = pl.ds(base + k * ROW_CHUNK, ROW_CHUNK)
                x2 = stash_ref[k * ROW_CHUNK:(k + 1) * ROW_CHUNK, :]
                o_ref[0, rows, :] = x2 * scales[k] * fw if final_norm else x2
            return carry

        lax.fori_loop(0, FFN_TM // group, body, 0)


def _ffn(h2, x1, mod, final_w, wg, wu, wd, final_norm):
    b, s, d = x1.shape
    tm, tf = FFN_TM, FFN_TF
    return pl.pallas_call(
        functools.partial(_ffn_kernel, final_norm=final_norm),
        grid=(b, s // tm, D_FF // tf),
        in_specs=[pl.BlockSpec((1, tm, d), lambda bi, m, j: (bi, m, 0)),
                  pl.BlockSpec(memory_space=pl.ANY),
                  pl.BlockSpec((1, 6, d), lambda bi, m, j: (bi, 0, 0)),
                  pl.BlockSpec((1, d), lambda bi, m, j: (0, 0)),
                  pl.BlockSpec((d, tf), lambda bi, m, j: (0, j)),
                  pl.BlockSpec((d, tf), lambda bi, m, j: (0, j)),
                  pl.BlockSpec((tf, d), lambda bi, m, j: (j, 0))],
        out_specs=pl.BlockSpec((1, tm, d), lambda bi, m, j: (bi, m, 0)),
        out_shape=jax.ShapeDtypeStruct((b, s, d), F32),
        scratch_shapes=[pltpu.VMEM((tm, d), F32), pltpu.VMEM((ROW_CHUNK * ROW_UNROLL, d), F32),
                        pltpu.SemaphoreType.DMA(())],
        compiler_params=_params("arbitrary", "arbitrary", "arbitrary"),
        name="ffn",
    )(h2, x1, mod, final_w, wg, wu, wd)


def kernel(x, c, w_ada, b_ada, norm1_w, w_in, lower_bounds, hgrn_norm_w, attn_sinks,
           rel_bias_table, w_out, norm2_w, w_gate, w_up, w_down, final_norm_w):
    b, s, d = x.shape
    assert (d, w_in.shape[-1], w_gate.shape[-1]) == (D_MODEL, IN_WIDTH, D_FF)
    assert s % max(INPROJ_TM, HGRN_TC, OUTPROJ_TM, FFN_TM, ATTN_BLOCK) == 0
    depth = w_ada.shape[0]
    c8 = jnp.pad(c, ((0, 8 - b), (0, 0)))
    bias = _bias_table(rel_bias_table)
    for layer in range(depth):
        mod = _ada(c8, w_ada[layer], b_ada[layer][None, :])[:b].reshape(b, 6, d)
        p32, p16 = _inproj(x, mod, norm1_w[layer][None, :], _split_in_weight(w_in[layer]))
        o_rec = _hgrn(p32, p16, lower_bounds, hgrn_norm_w[layer][None, :], layer)
        o_att = _attn(p16, attn_sinks[layer], bias)
        x1, h2 = _outproj(o_rec, o_att, x, mod, norm2_w[layer][None, :], w_out[layer].astype(BF16))
        x = _ffn(h2, x1, mod, final_norm_w[None, :], w_gate[layer].astype(BF16),
                 w_up[layer].astype(BF16), w_down[layer].astype(BF16),
                 final_norm=(layer == depth - 1))
    return x
```

```python
import functools

import numpy as np
import jax
import jax.numpy as jnp
from jax import lax
from jax.experimental import pallas as pl
from jax.experimental.pallas import tpu as pltpu

D_MODEL = 2048
DEPTH = 1
HGRN_WIDTH = 1024
HGRN_HEAD_DIM = 128
HGRN_HEADS = HGRN_WIDTH // HGRN_HEAD_DIM
HGRN_CHUNK = 64
ATTN_WIDTH = 1024
ATTN_HEAD_DIM = 64
ATTN_Q_HEADS = ATTN_WIDTH // ATTN_HEAD_DIM
ATTN_KV_HEADS = 4
ATTN_GROUP = ATTN_Q_HEADS // ATTN_KV_HEADS
WINDOW = 128
ATTN_BLOCK = 128
REL_BUCKETS = 32
REL_MAX_DIST = 128
D_FF = 5632
KV_WIDTH = ATTN_KV_HEADS * ATTN_HEAD_DIM
IN_WIDTH = 4 * HGRN_WIDTH + ATTN_WIDTH + 2 * KV_WIDTH
P32_WIDTH = 3 * HGRN_WIDTH
P16_WIDTH = HGRN_WIDTH + ATTN_WIDTH + 2 * KV_WIDTH
EPS = 1e-6
NEG_INF = -1e30
LOG2E = 1.4426950408889634

F32 = jnp.float32
BF16 = jnp.bfloat16

VMEM_LIMIT_BYTES = 56 * 1024 * 1024

ADA_TN = 1024
INPROJ_TM = 512
INPROJ_TN = 512
HGRN_TC = 512
HGRN_UNROLL = 2
OUTPROJ_TM = 512
FFN_TM = 1024
FFN_TF = 512
FFN_SUB = 256
ROW_CHUNK = 16
ROW_UNROLL = 8


def _params(*semantics):
    return pltpu.CompilerParams(dimension_semantics=semantics,
                                vmem_limit_bytes=VMEM_LIMIT_BYTES)


def _sigmoid(v):
    return 1.0 / (1.0 + jnp.exp2(v * (-LOG2E)))


def _silu(v):
    return v * _sigmoid(v)


def _dot(a, b):
    return jnp.dot(a, b, preferred_element_type=F32)


def _dot_nt(a, b):
    return lax.dot_general(a, b, (((1,), (1,)), ((), ())), preferred_element_type=F32)


def _dot_tn(a, b):
    return lax.dot_general(a, b, (((0,), (0,)), ((), ())), preferred_element_type=F32)


def _ada_kernel(c_ref, w_ref, b_ref, o_ref):
    c_act = _silu(c_ref[...])
    o_ref[...] = _dot(c_act.astype(BF16), w_ref[...].astype(BF16)) + b_ref[...]


def _ada(c8, w, b):
    n = w.shape[1]
    return pl.pallas_call(
        _ada_kernel,
        grid=(n // ADA_TN,),
        in_specs=[pl.BlockSpec((8, D_MODEL), lambda j: (0, 0)),
                  pl.BlockSpec((D_MODEL, ADA_TN), lambda j: (0, j)),
                  pl.BlockSpec((1, ADA_TN), lambda j: (0, j))],
        out_specs=pl.BlockSpec((8, ADA_TN), lambda j: (0, j)),
        out_shape=jax.ShapeDtypeStruct((8, n), F32),
        compiler_params=_params("arbitrary"),
        name="ada",
    )(c8, w, b)


def _modulated_norm(x, wmod, shift):
    return x * lax.rsqrt(jnp.mean(x * x, axis=-1, keepdims=True) + EPS) * wmod + shift


def _for_row_chunks(n_rows, fn):
    def body(i, carry):
        fn(pl.ds(pl.multiple_of(i * ROW_CHUNK, ROW_CHUNK), ROW_CHUNK))
        return carry

    lax.fori_loop(0, n_rows // ROW_CHUNK, body, 0, unroll=ROW_UNROLL)


_INPROJ_SEGMENTS = (
    (0, 0, 0, 2 * HGRN_WIDTH),
    (0, 2 * HGRN_WIDTH, 3 * HGRN_WIDTH, HGRN_WIDTH),
    (1, 0, 2 * HGRN_WIDTH, HGRN_WIDTH),
    (1, HGRN_WIDTH, 4 * HGRN_WIDTH, ATTN_WIDTH + 2 * KV_WIDTH),
)


def _inproj_kernel(x_ref, mod_ref, nw_ref, w_ref, o32_ref, o16_ref, h_ref):
    shift = mod_ref[0, 0:1, :]
    wmod = nw_ref[...] * (1.0 + mod_ref[0, 1:2, :])

    def chunk(rows):
        h_ref[rows, :] = _modulated_norm(x_ref[0, rows, :], wmod, shift).astype(BF16)

    _for_row_chunks(INPROJ_TM, chunk)

    outs = (o32_ref, o16_ref)
    for dst, dst_col, src_col, width in _INPROJ_SEGMENTS:
        for off in range(0, width, INPROJ_TN):
            res = _dot(h_ref[...], w_ref[:, src_col + off:src_col + off + INPROJ_TN])
            outs[dst][0, :, dst_col + off:dst_col + off + INPROJ_TN] = res.astype(outs[dst].dtype)


def _inproj(x, mod, norm_w, w_bf):
    b, s, d = x.shape
    tm = INPROJ_TM
    return pl.pallas_call(
        _inproj_kernel,
        grid=(b, s // tm),
        in_specs=[pl.BlockSpec((1, tm, d), lambda bi, m: (bi, m, 0)),
                  pl.BlockSpec((1, 6, d), lambda bi, m: (bi, 0, 0)),
                  pl.BlockSpec((1, d), lambda bi, m: (0, 0)),
                  pl.BlockSpec(w_bf.shape, lambda bi, m: (0, 0), pipeline_mode=pl.Buffered(1))],
        out_specs=[pl.BlockSpec((1, tm, P32_WIDTH), lambda bi, m: (bi, m, 0)),
                   pl.BlockSpec((1, tm, P16_WIDTH), lambda bi, m: (bi, m, 0))],
        out_shape=[jax.ShapeDtypeStruct((b, s, P32_WIDTH), F32),
                   jax.ShapeDtypeStruct((b, s, P16_WIDTH), BF16)],
        scratch_shapes=[pltpu.VMEM((tm, d), BF16)],
        compiler_params=_params("parallel", "parallel"),
        name="inproj",
    )(x, mod, norm_w, w_bf)


def _split3(v):
    hi = v.astype(BF16)
    r1 = v - hi.astype(F32)
    mid = r1.astype(BF16)
    lo = (r1 - mid.astype(F32)).astype(BF16)
    return hi, mid, lo


def _hgrn_kernel(q_ref, f_ref, i_ref, g_ref, lb_ref, nw_ref, o_ref, st_ref, *, layer):
    c = HGRN_CHUNK
    dk = HGRN_HEAD_DIM

    @pl.when(pl.program_id(1) == 0)
    def _():
        st_ref[...] = jnp.zeros_like(st_ref)

    lb_rows = [lb_ref[r:r + 1, :] for r in range(lb_ref.shape[0])]
    lb_max = functools.reduce(jnp.maximum, lb_rows)
    lb_exp = [jnp.exp(r - lb_max) for r in lb_rows]
    lb_all = sum(lb_exp[:layer + 1]) / sum(lb_exp)

    row = lax.broadcasted_iota(jnp.int32, (c, c), 0)
    col = lax.broadcasted_iota(jnp.int32, (c, c), 1)
    causal = row >= col
    tril = jnp.where(causal, 1.0, 0.0).astype(BF16)
    tril3 = jnp.concatenate([tril, tril, tril], axis=1)
    heads = [slice(h * dk, (h + 1) * dk) for h in range(HGRN_HEADS)]

    def chunk(rows, st):
        f = lb_all + (1.0 - lb_all) * _sigmoid(f_ref[0, rows, :])
        k = 1.0 - f
        parts = jnp.concatenate(_split3(jnp.log2(f)), axis=0)
        b = _dot(tril3, parts)
        b_mid = b[c // 2 - 1:c // 2, :]
        b_last = b[c - 1:c, :]
        d_mid = b - b_mid
        qr = q_ref[0, rows, :] * jnp.exp2(d_mid)
        kr = k * jnp.exp2(-d_mid)
        q_rel = qr.astype(BF16)
        k_rel = kr.astype(BF16)
        q_dec = (qr * jnp.exp2(b_mid)).astype(BF16)
        k_dec = (kr * jnp.exp2(b_last - b_mid)).astype(BF16)
        decay = jnp.exp2(b_last)
        v = i_ref[0, rows, :]
        gate = nw_ref[...] * _silu(g_ref[0, rows, :])
        a = [jnp.where(causal, _dot_nt(q_rel[:, hs], k_rel[:, hs]), 0.0).astype(BF16) for hs in heads]
        o = [_dot(a[h], v[:, hs]) + _dot_nt(q_dec[:, hs], st[h].astype(BF16))
             for h, hs in enumerate(heads)]
        st = [decay[:, hs] * st[h] + _dot_tn(v[:, hs], k_dec[:, hs]) for h, hs in enumerate(heads)]
        for h, hs in enumerate(heads):
            oh = o[h] * lax.rsqrt(jnp.mean(o[h] * o[h], axis=-1, keepdims=True) + EPS)
            o_ref[0, rows, hs] = (oh * gate[:, hs]).astype(o_ref.dtype)
        return st

    def body(i, carry):
        st = [st_ref[h] for h in range(HGRN_HEADS)]
        base = pl.multiple_of(i * (HGRN_UNROLL * c), HGRN_UNROLL * c)
        for u in range(HGRN_UNROLL):
            st = chunk(pl.ds(base + u * c, c), st)
        for h in range(HGRN_HEADS):
            st_ref[h] = st[h]
        return carry

    lax.fori_loop(0, HGRN_TC // (HGRN_UNROLL * c), body, 0)


def _hgrn(p32, p16, lower_bounds, norm_w, layer):
    b, s, _ = p32.shape
    blk = lambda j: pl.BlockSpec((1, HGRN_TC, HGRN_WIDTH), lambda bi, t, j=j: (bi, t, j))
    return pl.pallas_call(
        functools.partial(_hgrn_kernel, layer=layer),
        grid=(b, s // HGRN_TC),
        in_specs=[blk(0), blk(1), blk(0), blk(2),
                  pl.BlockSpec(lower_bounds.shape, lambda bi, t: (0, 0)),
                  pl.BlockSpec((1, HGRN_WIDTH), lambda bi, t: (0, 0))],
        out_specs=pl.BlockSpec((1, HGRN_TC, HGRN_WIDTH), lambda bi, t: (bi, t, 0)),
        out_shape=jax.ShapeDtypeStruct((b, s, HGRN_WIDTH), BF16),
        scratch_shapes=[pltpu.VMEM((HGRN_HEADS, HGRN_HEAD_DIM, HGRN_HEAD_DIM), F32)],
        compiler_params=_params("parallel", "arbitrary"),
        name="hgrn",
    )(p32, p32, p16, p32, lower_bounds, norm_w)


def _t5_causal_buckets(dist):
    max_exact = REL_BUCKETS // 2
    d = np.maximum(dist, 0)
    log_b = max_exact + (np.log(np.maximum(d, 1) / max_exact)
                         / np.log(REL_MAX_DIST / max_exact)
                         * (REL_BUCKETS - max_exact)).astype(np.int32)
    log_b = np.minimum(log_b, REL_BUCKETS - 1)
    return np.where(d < max_exact, d, log_b).astype(np.int32)


def _bias_kernel(tab_ref, bucket_ref, valid_ref, o_ref):
    h = pl.program_id(0)
    bucket = bucket_ref[...]
    acc = jnp.zeros(bucket.shape, F32)
    for bk in range(REL_BUCKETS):
        acc = jnp.where(bucket == bk, tab_ref[bk, h], acc)
    for variant in range(2):
        o_ref[variant, 0] = jnp.where(valid_ref[variant] != 0, acc, NEG_INF)


def _bias_table(rel_table):
    l = ATTN_BLOCK
    qi = np.arange(l)[:, None]
    kj = np.arange(2 * l)[None, :]
    dist = qi + l - kj
    in_window = (dist >= 0) & (dist < WINDOW)
    valid = np.stack([in_window & (kj >= l), in_window]).astype(np.int32)
    bucket = _t5_causal_buckets(dist)
    return pl.pallas_call(
        _bias_kernel,
        grid=(ATTN_Q_HEADS,),
        in_specs=[pl.BlockSpec(memory_space=pltpu.SMEM),
                  pl.BlockSpec((l, 2 * l), lambda h: (0, 0)),
                  pl.BlockSpec((2, l, 2 * l), lambda h: (0, 0, 0))],
        out_specs=pl.BlockSpec((2, 1, l, 2 * l), lambda h: (0, h, 0, 0)),
        out_shape=jax.ShapeDtypeStruct((2, ATTN_Q_HEADS, l, 2 * l), F32),
        compiler_params=_params("arbitrary"),
        name="bias",
    )(rel_table, jnp.asarray(bucket), jnp.asarray(valid))


def _attn_kernel(sink_ref, q_ref, kp_ref, kc_ref, vp_ref, vc_ref, bias_ref, o_ref):
    l = ATTN_BLOCK
    dh = ATTN_HEAD_DIM
    lane = lax.broadcasted_iota(jnp.int32, (1, 2 * dh), 1)
    low = lane < dh

    for pair in range(ATTN_KV_HEADS // 2):
        pc = slice(pair * 2 * dh, (pair + 1) * 2 * dh)
        kk = pltpu.bitcast(jnp.concatenate([kp_ref[0, :, pc], kc_ref[0, :, pc]], axis=0), jnp.int32)
        vv = pltpu.bitcast(jnp.concatenate([vp_ref[0, :, pc], vc_ref[0, :, pc]], axis=0), jnp.int32)
        kk_sw = pltpu.roll(kk, dh, axis=1)
        vv_sw = pltpu.roll(vv, dh, axis=1)
        for sub in range(2):
            kvh = pair * 2 + sub
            if sub == 0:
                k2 = jnp.where(low, kk, kk_sw)
                v2 = jnp.where(low, vv, vv_sw)
            else:
                k2 = jnp.where(low, kk_sw, kk)
                v2 = jnp.where(low, vv_sw, vv)
            k2 = pltpu.bitcast(k2, BF16)
            v2 = pltpu.bitcast(v2, BF16)
            qs = []
            for j in range(ATTN_GROUP):
                hq = kvh * ATTN_GROUP + j
                qc = slice((hq // 2) * 2 * dh, (hq // 2 + 1) * 2 * dh)
                q2 = q_ref[0, :, qc] * (dh ** -0.5)
                keep = low if hq % 2 == 0 else jnp.logical_not(low)
                qs.append(jnp.where(keep, q2, 0.0).astype(BF16))
            s_all = _dot_nt(jnp.concatenate(qs, axis=0), k2)
            ps = []
            rden = []
            for j in range(ATTN_GROUP):
                hq = kvh * ATTN_GROUP + j
                s = s_all[j * l:(j + 1) * l] + bias_ref[0, hq]
                sink = sink_ref[hq]
                m = jnp.maximum(jnp.max(s, axis=-1, keepdims=True), sink)
                p = jnp.exp(s - m)
                den = jnp.sum(p, axis=-1, keepdims=True) + jnp.exp(sink - m)
                ps.append(p.astype(BF16))
                rden.append(1.0 / den)
            o_all = _dot(jnp.concatenate(ps, axis=0), v2)
            for jp in range(ATTN_GROUP // 2):
                hq = kvh * ATTN_GROUP + 2 * jp
                oe = o_all[(2 * jp) * l:(2 * jp + 1) * l] * rden[2 * jp]
                oo = o_all[(2 * jp + 1) * l:(2 * jp + 2) * l] * rden[2 * jp + 1]
                oc = slice((hq // 2) * 2 * dh, (hq // 2 + 1) * 2 * dh)
                o_ref[0, :, oc] = jnp.where(low, oe, oo).astype(o_ref.dtype)


def _attn(p16, sinks, bias):
    b, s, _ = p16.shape
    l = ATTN_BLOCK
    q_blk = HGRN_WIDTH // ATTN_WIDTH
    k_blk = (HGRN_WIDTH + ATTN_WIDTH) // KV_WIDTH
    v_blk = k_blk + 1
    kv_spec = lambda blk, prev: pl.BlockSpec(
        (1, l, KV_WIDTH),
        (lambda bi, n: (bi, jnp.maximum(n - 1, 0), blk)) if prev else (lambda bi, n: (bi, n, blk)))
    return pl.pallas_call(
        _attn_kernel,
        grid=(b, s // l),
        in_specs=[pl.BlockSpec(memory_space=pltpu.SMEM),
                  pl.BlockSpec((1, l, ATTN_WIDTH), lambda bi, n: (bi, n, q_blk)),
                  kv_spec(k_blk, True), kv_spec(k_blk, False),
                  kv_spec(v_blk, True), kv_spec(v_blk, False),
                  pl.BlockSpec((1, ATTN_Q_HEADS, l, 2 * l),
                               lambda bi, n: (jnp.minimum(n, 1), 0, 0, 0))],
        out_specs=pl.BlockSpec((1, l, ATTN_WIDTH), lambda bi, n: (bi, n, 0)),
        out_shape=jax.ShapeDtypeStruct((b, s, ATTN_WIDTH), BF16),
        compiler_params=_params("parallel", "arbitrary"),
        name="attn",
    )(sinks, p16, p16, p16, p16, p16, bias)


def _outproj_kernel(orec_ref, oatt_ref, x_ref, mod_ref, nw_ref, w_ref, x1_ref, h2_ref, y_ref):
    y_ref[...] = (_dot(orec_ref[0], w_ref[0:HGRN_WIDTH, :])
                  + _dot(oatt_ref[0], w_ref[HGRN_WIDTH:, :]))
    gate = mod_ref[0, 2:3, :]
    shift = mod_ref[0, 3:4, :]
    wmod = nw_ref[...] * (1.0 + mod_ref[0, 4:5, :])

    def chunk(rows):
        x1 = x_ref[0, rows, :] + gate * y_ref[rows, :]
        x1_ref[0, rows, :] = x1
        h2_ref[0, rows, :] = _modulated_norm(x1, wmod, shift).astype(BF16)

    _for_row_chunks(OUTPROJ_TM, chunk)


def _outproj(o_rec, o_att, x, mod, norm_w, w_bf):
    b, s, d = x.shape
    tm = OUTPROJ_TM
    return pl.pallas_call(
        _outproj_kernel,
        grid=(b, s // tm),
        in_specs=[pl.BlockSpec((1, tm, HGRN_WIDTH), lambda bi, m: (bi, m, 0)),
                  pl.BlockSpec((1, tm, ATTN_WIDTH), lambda bi, m: (bi, m, 0)),
                  pl.BlockSpec((1, tm, d), lambda bi, m: (bi, m, 0)),
                  pl.BlockSpec((1, 6, d), lambda bi, m: (bi, 0, 0)),
                  pl.BlockSpec((1, d), lambda bi, m: (0, 0)),
                  pl.BlockSpec(w_bf.shape, lambda bi, m: (0, 0), pipeline_mode=pl.Buffered(1))],
        out_specs=[pl.BlockSpec((1, tm, d), lambda bi, m: (bi, m, 0)),
                   pl.BlockSpec((1, tm, d), lambda bi, m: (bi, m, 0))],
        out_shape=[jax.ShapeDtypeStruct((b, s, d), F32),
                   jax.ShapeDtypeStruct((b, s, d), BF16)],
        scratch_shapes=[pltpu.VMEM((tm, d), F32)],
        compiler_params=_params("parallel", "parallel"),
        name="outproj",
    )(o_rec, o_att, x, mod, norm_w, w_bf)


def _ffn_kernel(h_ref, x1_hbm, mod_ref, fw_ref, wg_ref, wu_ref, wd_ref, o_ref, x1_buf, stash_ref, x1_sem,
                *, final_norm):
    bi, m, j = pl.program_id(0), pl.program_id(1), pl.program_id(2)

    def x1_copy():
        rows = pl.ds(pl.multiple_of(m * FFN_TM, FFN_TM), FFN_TM)
        return pltpu.make_async_copy(x1_hbm.at[bi, rows, :], x1_buf, x1_sem)

    @pl.when(j == 0)
    def _():
        x1_copy().start()
        o_ref[...] = jnp.zeros_like(o_ref)

    part = None
    for s in range(FFN_TF // FFN_SUB):
        cs = slice(s * FFN_SUB, (s + 1) * FFN_SUB)
        act = (_silu(_dot(h_ref[0], wg_ref[:, cs])) * _dot(h_ref[0], wu_ref[:, cs])).astype(BF16)
        d = _dot(act, wd_ref[cs, :])
        part = d if part is None else part + d
    o_ref[0] += part

    @pl.when(j == pl.num_programs(2) - 1)
    def _():
        x1_copy().wait()
        gate = mod_ref[0, 5:6, :]
        fw = fw_ref[...]
        group = ROW_CHUNK * ROW_UNROLL

        def body(i, carry):
            base = pl.multiple_of(i * group, group)
            scales = []
            for k in range(ROW_UNROLL):
                rows = pl.ds(base + k * ROW_CHUNK, ROW_CHUNK)
                x2 = x1_buf[rows, :] + gate * o_ref[0, rows, :]
                stash_ref[k * ROW_CHUNK:(k + 1) * ROW_CHUNK, :] = x2
                if final_norm:
                    scales.append(lax.rsqrt(jnp.mean(x2 * x2, axis=-1, keepdims=True) + EPS))
            for k in range(ROW_UNROLL):
                rows = pl.ds(base + k * ROW_CHUNK, ROW_CHUNK)
                x2 = stash_ref[k * ROW_CHUNK:(k + 1) * ROW_CHUNK, :]
                o_ref[0, rows, :] = x2 * scales[k] * fw if final_norm else x2
            return carry

        lax.fori_loop(0, FFN_TM // group, body, 0)


def _ffn(h2, x1, mod, final_w, wg, wu, wd, final_norm):
    b, s, d = x1.shape
    tm, tf = FFN_TM, FFN_TF
    return pl.pallas_call(
        functools.partial(_ffn_kernel, final_norm=final_norm),
        grid=(b, s // tm, D_FF // tf),
        in_specs=[pl.BlockSpec((1, tm, d), lambda bi, m, j: (bi, m, 0)),
                  pl.BlockSpec(memory_space=pl.ANY),
                  pl.BlockSpec((1, 6, d), lambda bi, m, j: (bi, 0, 0)),
                  pl.BlockSpec((1, d), lambda bi, m, j: (0, 0)),
                  pl.BlockSpec((d, tf), lambda bi, m, j: (0, j)),
                  pl.BlockSpec((d, tf), lambda bi, m, j: (0, j)),
                  pl.BlockSpec((tf, d), lambda bi, m, j: (j, 0))],
        out_specs=pl.BlockSpec((1, tm, d), lambda bi, m, j: (bi, m, 0)),
        out_shape=jax.ShapeDtypeStruct((b, s, d), F32),
        scratch_shapes=[pltpu.VMEM((tm, d), F32), pltpu.VMEM((ROW_CHUNK * ROW_UNROLL, d), F32),
                        pltpu.SemaphoreType.DMA(())],
        compiler_params=_params("arbitrary", "arbitrary", "arbitrary"),
        name="ffn",
    )(h2, x1, mod, final_w, wg, wu, wd)


def kernel(x, c, w_ada, b_ada, norm1_w, w_in, lower_bounds, hgrn_norm_w, attn_sinks,
           rel_bias_table, w_out, norm2_w, w_gate, w_up, w_down, final_norm_w):
    b, s, d = x.shape
    assert (d, w_in.shape[-1], w_gate.shape[-1]) == (D_MODEL, IN_WIDTH, D_FF)
    assert s % max(INPROJ_TM, HGRN_TC, OUTPROJ_TM, FFN_TM, ATTN_BLOCK) == 0
    depth = w_ada.shape[0]
    c8 = jnp.pad(c, ((0, 8 - b), (0, 0)))
    bias = _bias_table(rel_bias_table)
    for layer in range(depth):
        mod = _ada(c8, w_ada[layer], b_ada[layer][None, :])[:b].reshape(b, 6, d)
        p32, p16 = _inproj(x, mod, norm1_w[layer][None, :], w_in[layer].astype(BF16))
        o_rec = _hgrn(p32, p16, lower_bounds, hgrn_norm_w[layer][None, :], layer)
        o_att = _attn(p16, attn_sinks[layer], bias)
        x1, h2 = _outproj(o_rec, o_att, x, mod, norm2_w[layer][None, :], w_out[layer].astype(BF16))
        x = _ffn(h2, x1, mod, final_norm_w[None, :], w_gate[layer].astype(BF16),
                 w_up[layer].astype(BF16), w_down[layer].astype(BF16),
                 final_norm=(layer == depth - 1))
    return x
```

```python
import functools

import numpy as np
import jax
import jax.numpy as jnp
from jax import lax
from jax.experimental import pallas as pl
from jax.experimental.pallas import tpu as pltpu

D_MODEL = 2048
DEPTH = 1
HGRN_WIDTH = 1024
HGRN_HEAD_DIM = 128
HGRN_HEADS = HGRN_WIDTH // HGRN_HEAD_DIM
HGRN_CHUNK = 64
ATTN_WIDTH = 1024
ATTN_HEAD_DIM = 64
ATTN_Q_HEADS = ATTN_WIDTH // ATTN_HEAD_DIM
ATTN_KV_HEADS = 4
ATTN_GROUP = ATTN_Q_HEADS // ATTN_KV_HEADS
WINDOW = 128
ATTN_BLOCK = 128
ATTN_QB = 2
REL_BUCKETS = 32
REL_MAX_DIST = 128
D_FF = 5632
KV_WIDTH = ATTN_KV_HEADS * ATTN_HEAD_DIM
IN_WIDTH = 4 * HGRN_WIDTH + ATTN_WIDTH + 2 * KV_WIDTH
P32_WIDTH = 3 * HGRN_WIDTH
P16_WIDTH = HGRN_WIDTH + ATTN_WIDTH + 2 * KV_WIDTH
EPS = 1e-6
NEG_INF = -1e30
LOG2E = 1.4426950408889634

F32 = jnp.float32
BF16 = jnp.bfloat16

VMEM_LIMIT_BYTES = 56 * 1024 * 1024

ADA_TN = 1024
INPROJ_TM = 512
INPROJ_TN = 512
HGRN_TC = 512
HGRN_UNROLL = 2
OUTPROJ_TM = 512
FFN_TM = 1024
FFN_TF = 512
FFN_SUB = 256
ROW_CHUNK = 16
ROW_UNROLL = 8


def _params(*semantics):
    return pltpu.CompilerParams(dimension_semantics=semantics,
                                vmem_limit_bytes=VMEM_LIMIT_BYTES)


def _sigmoid(v):
    return 1.0 / (1.0 + jnp.exp2(v * (-LOG2E)))


def _silu(v):
    return v * _sigmoid(v)


def _dot(a, b):
    return jnp.dot(a, b, preferred_element_type=F32)


def _dot_nt(a, b):
    return lax.dot_general(a, b, (((1,), (1,)), ((), ())), preferred_element_type=F32)


def _dot_tn(a, b):
    return lax.dot_general(a, b, (((0,), (0,)), ((), ())), preferred_element_type=F32)


def _ada_kernel(c_ref, w_ref, b_ref, o_ref):
    c_act = _silu(c_ref[...])
    o_ref[...] = _dot(c_act.astype(BF16), w_ref[...].astype(BF16)) + b_ref[...]


def _ada(c8, w, b):
    n = w.shape[1]
    return pl.pallas_call(
        _ada_kernel,
        grid=(n // ADA_TN,),
        in_specs=[pl.BlockSpec((8, D_MODEL), lambda j: (0, 0)),
                  pl.BlockSpec((D_MODEL, ADA_TN), lambda j: (0, j)),
                  pl.BlockSpec((1, ADA_TN), lambda j: (0, j))],
        out_specs=pl.BlockSpec((8, ADA_TN), lambda j: (0, j)),
        out_shape=jax.ShapeDtypeStruct((8, n), F32),
        compiler_params=_params("arbitrary"),
        name="ada",
    )(c8, w, b)


def _modulated_norm(x, wmod, shift):
    return x * lax.rsqrt(jnp.mean(x * x, axis=-1, keepdims=True) + EPS) * wmod + shift


def _for_row_chunks(n_rows, fn):
    def body(i, carry):
        fn(pl.ds(pl.multiple_of(i * ROW_CHUNK, ROW_CHUNK), ROW_CHUNK))
        return carry

    lax.fori_loop(0, n_rows // ROW_CHUNK, body, 0, unroll=ROW_UNROLL)


_INPROJ_SEGMENTS = (
    (0, 0, 0, 2 * HGRN_WIDTH),
    (0, 2 * HGRN_WIDTH, 3 * HGRN_WIDTH, HGRN_WIDTH),
    (1, 0, 2 * HGRN_WIDTH, HGRN_WIDTH),
    (1, HGRN_WIDTH, 4 * HGRN_WIDTH, ATTN_WIDTH + 2 * KV_WIDTH),
)


def _inproj_kernel(x_ref, mod_ref, nw_ref, w_ref, o32_ref, o16_ref, h_ref):
    shift = mod_ref[0, 0:1, :]
    wmod = nw_ref[...] * (1.0 + mod_ref[0, 1:2, :])

    def chunk(rows):
        h_ref[rows, :] = _modulated_norm(x_ref[0, rows, :], wmod, shift).astype(BF16)

    _for_row_chunks(INPROJ_TM, chunk)

    outs = (o32_ref, o16_ref)
    for dst, dst_col, src_col, width in _INPROJ_SEGMENTS:
        for off in range(0, width, INPROJ_TN):
            res = _dot(h_ref[...], w_ref[:, src_col + off:src_col + off + INPROJ_TN])
            outs[dst][0, :, dst_col + off:dst_col + off + INPROJ_TN] = res.astype(outs[dst].dtype)


def _inproj(x, mod, norm_w, w_bf):
    b, s, d = x.shape
    tm = INPROJ_TM
    return pl.pallas_call(
        _inproj_kernel,
        grid=(b, s // tm),
        in_specs=[pl.BlockSpec((1, tm, d), lambda bi, m: (bi, m, 0)),
                  pl.BlockSpec((1, 6, d), lambda bi, m: (bi, 0, 0)),
                  pl.BlockSpec((1, d), lambda bi, m: (0, 0)),
                  pl.BlockSpec(w_bf.shape, lambda bi, m: (0, 0), pipeline_mode=pl.Buffered(1))],
        out_specs=[pl.BlockSpec((1, tm, P32_WIDTH), lambda bi, m: (bi, m, 0)),
                   pl.BlockSpec((1, tm, P16_WIDTH), lambda bi, m: (bi, m, 0))],
        out_shape=[jax.ShapeDtypeStruct((b, s, P32_WIDTH), F32),
                   jax.ShapeDtypeStruct((b, s, P16_WIDTH), BF16)],
        scratch_shapes=[pltpu.VMEM((tm, d), BF16)],
        compiler_params=_params("parallel", "parallel"),
        name="inproj",
    )(x, mod, norm_w, w_bf)


def _split3(v):
    hi = v.astype(BF16)
    r1 = v - hi.astype(F32)
    mid = r1.astype(BF16)
    lo = (r1 - mid.astype(F32)).astype(BF16)
    return hi, mid, lo


def _hgrn_kernel(q_ref, f_ref, i_ref, g_ref, lb_ref, nw_ref, wa_ref, wb_ref,
                 o_ref, wa16_ref, wb16_ref, st_ref, *, layer):
    c = HGRN_CHUNK
    dk = HGRN_HEAD_DIM
    wa16_ref[...] = wa_ref[...].astype(BF16)
    wb16_ref[...] = wb_ref[...].astype(BF16)

    @pl.when(pl.program_id(1) == 0)
    def _():
        st_ref[...] = jnp.zeros_like(st_ref)

    lb_rows = [lb_ref[r:r + 1, :] for r in range(lb_ref.shape[0])]
    lb_max = functools.reduce(jnp.maximum, lb_rows)
    lb_exp = [jnp.exp(r - lb_max) for r in lb_rows]
    lb_all = sum(lb_exp[:layer + 1]) / sum(lb_exp)

    row = lax.broadcasted_iota(jnp.int32, (c, c), 0)
    col = lax.broadcasted_iota(jnp.int32, (c, c), 1)
    causal = row >= col
    tril = jnp.where(causal, 1.0, 0.0).astype(BF16)
    tril3 = jnp.concatenate([tril, tril, tril], axis=1)
    heads = [slice(h * dk, (h + 1) * dk) for h in range(HGRN_HEADS)]

    def chunk(rows, st):
        f = lb_all + (1.0 - lb_all) * _sigmoid(f_ref[0, rows, :])
        k = 1.0 - f
        parts = jnp.concatenate(_split3(jnp.log2(f)), axis=0)
        b = _dot(tril3, parts)
        b_mid = b[c // 2 - 1:c // 2, :]
        b_last = b[c - 1:c, :]
        d_mid = b - b_mid
        qr = q_ref[0, rows, :] * jnp.exp2(d_mid)
        kr = k * jnp.exp2(-d_mid)
        q_rel = qr.astype(BF16)
        k_rel = kr.astype(BF16)
        q_dec = (qr * jnp.exp2(b_mid)).astype(BF16)
        k_dec = (kr * jnp.exp2(b_last - b_mid)).astype(BF16)
        decay = jnp.exp2(b_last)
        v = i_ref[0, rows, :]
        gate = nw_ref[...] * _silu(g_ref[0, rows, :])
        a = [jnp.where(causal, _dot_nt(q_rel[:, hs], k_rel[:, hs]), 0.0).astype(BF16) for hs in heads]
        o = [_dot(a[h], v[:, hs]) + _dot_nt(q_dec[:, hs], st[h].astype(BF16))
             for h, hs in enumerate(heads)]
        st = [decay[:, hs] * st[h] + _dot_tn(v[:, hs], k_dec[:, hs]) for h, hs in enumerate(heads)]
        for h, hs in enumerate(heads):
            oh = o[h] * lax.rsqrt(jnp.mean(o[h] * o[h], axis=-1, keepdims=True) + EPS)
            o_ref[0, rows, hs] = (oh * gate[:, hs]).astype(o_ref.dtype)
        return st

    def body(i, carry):
        st = [st_ref[h] for h in range(HGRN_HEADS)]
        base = pl.multiple_of(i * (HGRN_UNROLL * c), HGRN_UNROLL * c)
        for u in range(HGRN_UNROLL):
            st = chunk(pl.ds(base + u * c, c), st)
        for h in range(HGRN_HEADS):
            st_ref[h] = st[h]
        return carry

    lax.fori_loop(0, HGRN_TC // (HGRN_UNROLL * c), body, 0)


def _hgrn(p32, p16, lower_bounds, norm_w, layer, wa, wb):
    b, s, _ = p32.shape
    nt = s // HGRN_TC
    blk = lambda j: pl.BlockSpec((1, HGRN_TC, HGRN_WIDTH), lambda bi, t, j=j: (bi, t, j))
    wa_spec, wa_shape = _cast_rider_specs(wa, b * nt, nt)
    wb_spec, wb_shape = _cast_rider_specs(wb, b * nt, nt)
    return pl.pallas_call(
        functools.partial(_hgrn_kernel, layer=layer),
        grid=(b, nt),
        in_specs=[blk(0), blk(1), blk(0), blk(2),
                  pl.BlockSpec(lower_bounds.shape, lambda bi, t: (0, 0)),
                  pl.BlockSpec((1, HGRN_WIDTH), lambda bi, t: (0, 0)),
                  wa_spec, wb_spec],
        out_specs=[pl.BlockSpec((1, HGRN_TC, HGRN_WIDTH), lambda bi, t: (bi, t, 0)), wa_spec, wb_spec],
        out_shape=[jax.ShapeDtypeStruct((b, s, HGRN_WIDTH), BF16), wa_shape, wb_shape],
        scratch_shapes=[pltpu.VMEM((HGRN_HEADS, HGRN_HEAD_DIM, HGRN_HEAD_DIM), F32)],
        compiler_params=_params("arbitrary", "arbitrary"),
        name="hgrn",
    )(p32, p32, p16, p32, lower_bounds, norm_w, wa, wb)


def _t5_causal_buckets(dist):
    max_exact = REL_BUCKETS // 2
    d = np.maximum(dist, 0)
    log_b = max_exact + (np.log(np.maximum(d, 1) / max_exact)
                         / np.log(REL_MAX_DIST / max_exact)
                         * (REL_BUCKETS - max_exact)).astype(np.int32)
    log_b = np.minimum(log_b, REL_BUCKETS - 1)
    return np.where(d < max_exact, d, log_b).astype(np.int32)


def _bias_kernel(tab_ref, bucket_ref, valid_ref, o_ref):
    h = pl.program_id(0)
    bucket = bucket_ref[...]
    acc = jnp.zeros(bucket.shape, F32)
    for bk in range(REL_BUCKETS):
        acc = jnp.where(bucket == bk, tab_ref[bk, h], acc)
    for variant in range(2):
        o_ref[variant, 0] = jnp.where(valid_ref[variant] != 0, acc, NEG_INF)


def _bias_table(rel_table):
    l = ATTN_BLOCK
    qi = np.arange(l)[:, None]
    kj = np.arange(2 * l)[None, :]
    dist = qi + l - kj
    in_window = (dist >= 0) & (dist < WINDOW)
    valid = np.stack([in_window & (kj >= l), in_window]).astype(np.int32)
    bucket = _t5_causal_buckets(dist)
    return pl.pallas_call(
        _bias_kernel,
        grid=(ATTN_Q_HEADS,),
        in_specs=[pl.BlockSpec(memory_space=pltpu.SMEM),
                  pl.BlockSpec((l, 2 * l), lambda h: (0, 0)),
                  pl.BlockSpec((2, l, 2 * l), lambda h: (0, 0, 0))],
        out_specs=pl.BlockSpec((2, 1, l, 2 * l), lambda h: (0, h, 0, 0)),
        out_shape=jax.ShapeDtypeStruct((2, ATTN_Q_HEADS, l, 2 * l), F32),
        compiler_params=_params("arbitrary"),
        name="bias",
    )(rel_table, jnp.asarray(bucket), jnp.asarray(valid))


def _attn_kernel(sink_ref, q_ref, kp_ref, kc_ref, vp_ref, vc_ref, bias_ref, wa_ref, wb_ref,
                 o_ref, wa16_ref, wb16_ref):
    l = ATTN_BLOCK
    dh = ATTN_HEAD_DIM
    wa16_ref[...] = wa_ref[...].astype(BF16)
    wb16_ref[...] = wb_ref[...].astype(BF16)
    lane = lax.broadcasted_iota(jnp.int32, (1, 2 * dh), 1)
    low = lane < dh
    first = jnp.minimum(pl.program_id(1), 1)

    for pair in range(ATTN_KV_HEADS // 2):
        pc = slice(pair * 2 * dh, (pair + 1) * 2 * dh)
        kk = pltpu.bitcast(jnp.concatenate([kp_ref[0, :, pc], kc_ref[0, :, pc]], axis=0), jnp.int32)
        vv = pltpu.bitcast(jnp.concatenate([vp_ref[0, :, pc], vc_ref[0, :, pc]], axis=0), jnp.int32)
        kk_sw = pltpu.roll(kk, dh, axis=1)
        vv_sw = pltpu.roll(vv, dh, axis=1)
        for sub in range(2):
            kvh = pair * 2 + sub
            if sub == 0:
                k2_all = jnp.where(low, kk, kk_sw)
                v2_all = jnp.where(low, vv, vv_sw)
            else:
                k2_all = jnp.where(low, kk_sw, kk)
                v2_all = jnp.where(low, vv_sw, vv)
            k2_all = pltpu.bitcast(k2_all, BF16)
            v2_all = pltpu.bitcast(v2_all, BF16)
            for t in range(ATTN_QB):
                qrows = slice(t * l, (t + 1) * l)
                k2 = k2_all[t * l:(t + 2) * l]
                v2 = v2_all[t * l:(t + 2) * l]
                qs = []
                for j in range(ATTN_GROUP):
                    hq = kvh * ATTN_GROUP + j
                    qc = slice((hq // 2) * 2 * dh, (hq // 2 + 1) * 2 * dh)
                    q2 = q_ref[0, qrows, qc] * (dh ** -0.5)
                    keep = low if hq % 2 == 0 else jnp.logical_not(low)
                    qs.append(jnp.where(keep, q2, 0.0).astype(BF16))
                s_all = _dot_nt(jnp.concatenate(qs, axis=0), k2)
                ps = []
                rden = []
                for j in range(ATTN_GROUP):
                    hq = kvh * ATTN_GROUP + j
                    bias = bias_ref[first, hq] if t == 0 else bias_ref[1, hq]
                    s = s_all[j * l:(j + 1) * l] + bias
                    sink = sink_ref[hq]
                    m = jnp.maximum(jnp.max(s, axis=-1, keepdims=True), sink)
                    p = jnp.exp(s - m)
                    den = jnp.sum(p, axis=-1, keepdims=True) + jnp.exp(sink - m)
                    ps.append(p.astype(BF16))
                    rden.append(1.0 / den)
                o_all = _dot(jnp.concatenate(ps, axis=0), v2)
                for jp in range(ATTN_GROUP // 2):
                    hq = kvh * ATTN_GROUP + 2 * jp
                    oe = o_all[(2 * jp) * l:(2 * jp + 1) * l] * rden[2 * jp]
                    oo = o_all[(2 * jp + 1) * l:(2 * jp + 2) * l] * rden[2 * jp + 1]
                    oc = slice((hq // 2) * 2 * dh, (hq // 2 + 1) * 2 * dh)
                    o_ref[0, qrows, oc] = jnp.where(low, oe, oo).astype(o_ref.dtype)


def _cast_rider_specs(w, steps, steps_per_batch):
    rows = w.shape[0] // steps
    assert rows * steps == w.shape[0] and rows % 8 == 0
    spec = pl.BlockSpec((rows, w.shape[1]), lambda bi, n: (bi * steps_per_batch + n, 0))
    return spec, jax.ShapeDtypeStruct(w.shape, BF16)


def _attn(p16, sinks, bias, wa, wb):
    b, s, _ = p16.shape
    l = ATTN_BLOCK
    tq = ATTN_QB * l
    nq = s // tq
    q_blk = HGRN_WIDTH // ATTN_WIDTH
    k_blk = (HGRN_WIDTH + ATTN_WIDTH) // KV_WIDTH
    v_blk = k_blk + 1
    prev_spec = lambda blk: pl.BlockSpec(
        (1, l, KV_WIDTH), lambda bi, n: (bi, jnp.maximum(n * ATTN_QB - 1, 0), blk))
    cur_spec = lambda blk: pl.BlockSpec((1, tq, KV_WIDTH), lambda bi, n: (bi, n, blk))
    wa_spec, wa_shape = _cast_rider_specs(wa, b * nq, nq)
    wb_spec, wb_shape = _cast_rider_specs(wb, b * nq, nq)
    return pl.pallas_call(
        _attn_kernel,
        grid=(b, nq),
        in_specs=[pl.BlockSpec(memory_space=pltpu.SMEM),
                  pl.BlockSpec((1, tq, ATTN_WIDTH), lambda bi, n: (bi, n, q_blk)),
                  prev_spec(k_blk), cur_spec(k_blk), prev_spec(v_blk), cur_spec(v_blk),
                  pl.BlockSpec(bias.shape, lambda bi, n: (0, 0, 0, 0), pipeline_mode=pl.Buffered(1)),
                  wa_spec, wb_spec],
        out_specs=[pl.BlockSpec((1, tq, ATTN_WIDTH), lambda bi, n: (bi, n, 0)), wa_spec, wb_spec],
        out_shape=[jax.ShapeDtypeStruct((b, s, ATTN_WIDTH), BF16), wa_shape, wb_shape],
        compiler_params=_params("arbitrary", "arbitrary"),
        name="attn",
    )(sinks, p16, p16, p16, p16, p16, bias, wa, wb)


def _outproj_kernel(orec_ref, oatt_ref, x_ref, mod_ref, nw_ref, w_ref, x1_ref, h2_ref, y_ref):
    y_ref[...] = (_dot(orec_ref[0], w_ref[0:HGRN_WIDTH, :])
                  + _dot(oatt_ref[0], w_ref[HGRN_WIDTH:, :]))
    gate = mod_ref[0, 2:3, :]
    shift = mod_ref[0, 3:4, :]
    wmod = nw_ref[...] * (1.0 + mod_ref[0, 4:5, :])

    def chunk(rows):
        x1 = x_ref[0, rows, :] + gate * y_ref[rows, :]
        x1_ref[0, rows, :] = x1
        h2_ref[0, rows, :] = _modulated_norm(x1, wmod, shift).astype(BF16)

    _for_row_chunks(OUTPROJ_TM, chunk)


def _outproj(o_rec, o_att, x, mod, norm_w, w_bf):
    b, s, d = x.shape
    tm = OUTPROJ_TM
    return pl.pallas_call(
        _outproj_kernel,
        grid=(b, s // tm),
        in_specs=[pl.BlockSpec((1, tm, HGRN_WIDTH), lambda bi, m: (bi, m, 0)),
                  pl.BlockSpec((1, tm, ATTN_WIDTH), lambda bi, m: (bi, m, 0)),
                  pl.BlockSpec((1, tm, d), lambda bi, m: (bi, m, 0)),
                  pl.BlockSpec((1, 6, d), lambda bi, m: (bi, 0, 0)),
                  pl.BlockSpec((1, d), lambda bi, m: (0, 0)),
                  pl.BlockSpec(w_bf.shape, lambda bi, m: (0, 0), pipeline_mode=pl.Buffered(1))],
        out_specs=[pl.BlockSpec((1, tm, d), lambda bi, m: (bi, m, 0)),
                   pl.BlockSpec((1, tm, d), lambda bi, m: (bi, m, 0))],
        out_shape=[jax.ShapeDtypeStruct((b, s, d), F32),
                   jax.ShapeDtypeStruct((b, s, d), BF16)],
        scratch_shapes=[pltpu.VMEM((tm, d), F32)],
        compiler_params=_params("parallel", "parallel"),
        name="outproj",
    )(o_rec, o_att, x, mod, norm_w, w_bf)


def _ffn_kernel(h_ref, x1_hbm, mod_ref, fw_ref, wg_ref, wu_ref, wd_ref, o_ref, x1_buf, stash_ref, x1_sem,
                *, final_norm):
    bi, m, j = pl.program_id(0), pl.program_id(1), pl.program_id(2)

    def x1_copy():
        rows = pl.ds(pl.multiple_of(m * FFN_TM, FFN_TM), FFN_TM)
        return pltpu.make_async_copy(x1_hbm.at[bi, rows, :], x1_buf, x1_sem)

    @pl.when(j == 0)
    def _():
        x1_copy().start()
        o_ref[...] = jnp.zeros_like(o_ref)

    part = None
    for s in range(FFN_TF // FFN_SUB):
        cs = slice(s * FFN_SUB, (s + 1) * FFN_SUB)
        act = (_silu(_dot(h_ref[0], wg_ref[:, cs])) * _dot(h_ref[0], wu_ref[:, cs])).astype(BF16)
        d = _dot(act, wd_ref[cs, :])
        part = d if part is None else part + d
    o_ref[0] += part

    @pl.when(j == pl.num_programs(2) - 1)
    def _():
        x1_copy().wait()
        gate = mod_ref[0, 5:6, :]
        fw = fw_ref[...]
        group = ROW_CHUNK * ROW_UNROLL

        def body(i, carry):
            base = pl.multiple_of(i * group, group)
            scales = []
            for k in range(ROW_UNROLL):
                rows = pl.ds(base + k * ROW_CHUNK, ROW_CHUNK)
                x2 = x1_buf[rows, :] + gate * o_ref[0, rows, :]
                stash_ref[k * ROW_CHUNK:(k + 1) * ROW_CHUNK, :] = x2
                if final_norm:
                    scales.append(lax.rsqrt(jnp.mean(x2 * x2, axis=-1, keepdims=True) + EPS))
            for k in range(ROW_UNROLL):
                rows = pl.ds(base + k * ROW_CHUNK, ROW_CHUNK)
                x2 = stash_ref[k * ROW_CHUNK:(k + 1) * ROW_CHUNK, :]
                o_ref[0, rows, :] = x2 * scales[k] * fw if final_norm else x2
            return carry

        lax.fori_loop(0, FFN_TM // group, body, 0)


def _ffn(h2, x1, mod, final_w, wg, wu, wd, final_norm):
    b, s, d = x1.shape
    tm, tf = FFN_TM, FFN_TF
    return pl.pallas_call(
        functools.partial(_ffn_kernel, final_norm=final_norm),
        grid=(b, s // tm, D_FF // tf),
        in_specs=[pl.BlockSpec((1, tm, d), lambda bi, m, j: (bi, m, 0)),
                  pl.BlockSpec(memory_space=pl.ANY),
                  pl.BlockSpec((1, 6, d), lambda bi, m, j: (bi, 0, 0)),
                  pl.BlockSpec((1, d), lambda bi, m, j: (0, 0)),
                  pl.BlockSpec((d, tf), lambda bi, m, j: (0, j)),
                  pl.BlockSpec((d, tf), lambda bi, m, j: (0, j)),
                  pl.BlockSpec((tf, d), lambda bi, m, j: (j, 0))],
        out_specs=pl.BlockSpec((1, tm, d), lambda bi, m, j: (bi, m, 0)),
        out_shape=jax.ShapeDtypeStruct((b, s, d), F32),
        scratch_shapes=[pltpu.VMEM((tm, d), F32), pltpu.VMEM((ROW_CHUNK * ROW_UNROLL, d), F32),
                        pltpu.SemaphoreType.DMA(())],
        compiler_params=_params("arbitrary", "arbitrary", "arbitrary"),
        name="ffn",
    )(h2, x1, mod, final_w, wg, wu, wd)


def kernel(x, c, w_ada, b_ada, norm1_w, w_in, lower_bounds, hgrn_norm_w, attn_sinks,
           rel_bias_table, w_out, norm2_w, w_gate, w_up, w_down, final_norm_w):
    b, s, d = x.shape
    assert (d, w_in.shape[-1], w_gate.shape[-1]) == (D_MODEL, IN_WIDTH, D_FF)
    assert s % max(INPROJ_TM, HGRN_TC, OUTPROJ_TM, FFN_TM, ATTN_QB * ATTN_BLOCK) == 0
    depth = w_ada.shape[0]
    c8 = jnp.pad(c, ((0, 8 - b), (0, 0)))
    bias = _bias_table(rel_bias_table)
    for layer in range(depth):
        mod = _ada(c8, w_ada[layer], b_ada[layer][None, :])[:b].reshape(b, 6, d)
        p32, p16 = _inproj(x, mod, norm1_w[layer][None, :], w_in[layer].astype(BF16))
        o_rec, wd16, wo16 = _hgrn(p32, p16, lower_bounds, hgrn_norm_w[layer][None, :], layer,
                                  w_down[layer], w_out[layer])
        o_att, wg16, wu16 = _attn(p16, attn_sinks[layer], bias, w_gate[layer], w_up[layer])
        x1, h2 = _outproj(o_rec, o_att, x, mod, norm2_w[layer][None, :], wo16)
        x = _ffn(h2, x1, mod, final_norm_w[None, :], wg16, wu16, wd16,
                 final_norm=(layer == depth - 1))
    return x
```

```python
import functools

import numpy as np
import jax
import jax.numpy as jnp
from jax import lax
from jax.experimental import pallas as pl
from jax.experimental.pallas import tpu as pltpu

D_MODEL = 2048
DEPTH = 1
HGRN_WIDTH = 1024
HGRN_HEAD_DIM = 128
HGRN_HEADS = HGRN_WIDTH // HGRN_HEAD_DIM
HGRN_CHUNK = 64
ATTN_WIDTH = 1024
ATTN_HEAD_DIM = 64
ATTN_Q_HEADS = ATTN_WIDTH // ATTN_HEAD_DIM
ATTN_KV_HEADS = 4
ATTN_GROUP = ATTN_Q_HEADS // ATTN_KV_HEADS
WINDOW = 128
ATTN_BLOCK = 128
ATTN_QB = 2
REL_BUCKETS = 32
REL_MAX_DIST = 128
D_FF = 5632
KV_WIDTH = ATTN_KV_HEADS * ATTN_HEAD_DIM
IN_WIDTH = 4 * HGRN_WIDTH + ATTN_WIDTH + 2 * KV_WIDTH
P32_WIDTH = 3 * HGRN_WIDTH
P16_WIDTH = HGRN_WIDTH + ATTN_WIDTH + 2 * KV_WIDTH
EPS = 1e-6
NEG_INF = -1e30
LOG2E = 1.4426950408889634

F32 = jnp.float32
BF16 = jnp.bfloat16

VMEM_LIMIT_BYTES = 56 * 1024 * 1024

ADA_TN = 1024
INPROJ_TM = 512
INPROJ_TN = 512
HGRN_TC = 512
HGRN_UNROLL = 4
OUTPROJ_TM = 512
OUTPROJ_TN = 256
FFN_TM = 1024
FFN_TF = 512
FFN_SUB = 256
ROW_CHUNK = 16
ROW_UNROLL = 8


def _params(*semantics):
    return pltpu.CompilerParams(dimension_semantics=semantics,
                                vmem_limit_bytes=VMEM_LIMIT_BYTES)


def _sigmoid(v):
    return 1.0 / (1.0 + jnp.exp2(v * (-LOG2E)))


def _silu(v):
    return v * _sigmoid(v)


def _dot(a, b):
    return jnp.dot(a, b, preferred_element_type=F32)


def _dot_nt(a, b):
    return lax.dot_general(a, b, (((1,), (1,)), ((), ())), preferred_element_type=F32)


def _dot_tn(a, b):
    return lax.dot_general(a, b, (((0,), (0,)), ((), ())), preferred_element_type=F32)


def _ada_kernel(c_ref, w_ref, b_ref, o_ref):
    c_act = _silu(c_ref[...])
    o_ref[...] = _dot(c_act.astype(BF16), w_ref[...].astype(BF16)) + b_ref[...]


def _ada(c8, w, b):
    n = w.shape[1]
    return pl.pallas_call(
        _ada_kernel,
        grid=(n // ADA_TN,),
        in_specs=[pl.BlockSpec((8, D_MODEL), lambda j: (0, 0)),
                  pl.BlockSpec((D_MODEL, ADA_TN), lambda j: (0, j)),
                  pl.BlockSpec((1, ADA_TN), lambda j: (0, j))],
        out_specs=pl.BlockSpec((8, ADA_TN), lambda j: (0, j)),
        out_shape=jax.ShapeDtypeStruct((8, n), F32),
        compiler_params=_params("arbitrary"),
        name="ada",
    )(c8, w, b)


def _modulated_norm(x, wmod, shift):
    return x * lax.rsqrt(jnp.mean(x * x, axis=-1, keepdims=True) + EPS) * wmod + shift


def _for_row_chunks(n_rows, fn):
    def body(i, carry):
        fn(pl.ds(pl.multiple_of(i * ROW_CHUNK, ROW_CHUNK), ROW_CHUNK))
        return carry

    lax.fori_loop(0, n_rows // ROW_CHUNK, body, 0, unroll=ROW_UNROLL)


_INPROJ_SEGMENTS = (
    (0, 0, 0, 2 * HGRN_WIDTH),
    (0, 2 * HGRN_WIDTH, 3 * HGRN_WIDTH, HGRN_WIDTH),
    (1, 0, 2 * HGRN_WIDTH, HGRN_WIDTH),
    (1, HGRN_WIDTH, 4 * HGRN_WIDTH, ATTN_WIDTH + 2 * KV_WIDTH),
)


def _inproj_kernel(x_ref, mod_ref, nw_ref, w_ref, o32_ref, o16_ref, h_ref):
    shift = mod_ref[0, 0:1, :]
    wmod = nw_ref[...] * (1.0 + mod_ref[0, 1:2, :])

    def chunk(rows):
        h_ref[rows, :] = _modulated_norm(x_ref[0, rows, :], wmod, shift).astype(BF16)

    _for_row_chunks(INPROJ_TM, chunk)

    outs = (o32_ref, o16_ref)
    for dst, dst_col, src_col, width in _INPROJ_SEGMENTS:
        for off in range(0, width, INPROJ_TN):
            res = _dot(h_ref[...], w_ref[:, src_col + off:src_col + off + INPROJ_TN])
            outs[dst][0, :, dst_col + off:dst_col + off + INPROJ_TN] = res.astype(outs[dst].dtype)


def _inproj(x, mod, norm_w, w_bf):
    b, s, d = x.shape
    tm = INPROJ_TM
    return pl.pallas_call(
        _inproj_kernel,
        grid=(b, s // tm),
        in_specs=[pl.BlockSpec((1, tm, d), lambda bi, m: (bi, m, 0)),
                  pl.BlockSpec((1, 6, d), lambda bi, m: (bi, 0, 0)),
                  pl.BlockSpec((1, d), lambda bi, m: (0, 0)),
                  pl.BlockSpec(w_bf.shape, lambda bi, m: (0, 0), pipeline_mode=pl.Buffered(1))],
        out_specs=[pl.BlockSpec((1, tm, P32_WIDTH), lambda bi, m: (bi, m, 0)),
                   pl.BlockSpec((1, tm, P16_WIDTH), lambda bi, m: (bi, m, 0))],
        out_shape=[jax.ShapeDtypeStruct((b, s, P32_WIDTH), F32),
                   jax.ShapeDtypeStruct((b, s, P16_WIDTH), BF16)],
        scratch_shapes=[pltpu.VMEM((tm, d), BF16)],
        compiler_params=_params("parallel", "parallel"),
        name="inproj",
    )(x, mod, norm_w, w_bf)


def _split3(v):
    hi = v.astype(BF16)
    r1 = v - hi.astype(F32)
    mid = r1.astype(BF16)
    lo = (r1 - mid.astype(F32)).astype(BF16)
    return hi, mid, lo


def _hgrn_kernel(q_ref, f_ref, i_ref, g_ref, lb_ref, nw_ref, wa_ref, wb_ref,
                 o_ref, wa16_ref, wb16_ref, st_ref, *, layer):
    c = HGRN_CHUNK
    dk = HGRN_HEAD_DIM
    wa16_ref[...] = wa_ref[...].astype(BF16)
    wb16_ref[...] = wb_ref[...].astype(BF16)

    @pl.when(pl.program_id(1) == 0)
    def _():
        st_ref[...] = jnp.zeros_like(st_ref)

    lb_rows = [lb_ref[r:r + 1, :] for r in range(lb_ref.shape[0])]
    lb_max = functools.reduce(jnp.maximum, lb_rows)
    lb_exp = [jnp.exp(r - lb_max) for r in lb_rows]
    lb_all = sum(lb_exp[:layer + 1]) / sum(lb_exp)

    nu = HGRN_UNROLL
    r = nu * c
    row = lax.broadcasted_iota(jnp.int32, (r, r), 0)
    col = lax.broadcasted_iota(jnp.int32, (r, r), 1)
    same_chunk = functools.reduce(
        jnp.logical_or, [(row >= u * c) & (row < (u + 1) * c) & (col >= u * c) & (col < (u + 1) * c)
                         for u in range(nu)])
    tril = jnp.where(same_chunk & (row >= col), 1.0, 0.0).astype(BF16)
    tril3 = jnp.concatenate([tril, tril, tril], axis=1)
    causal = (lax.broadcasted_iota(jnp.int32, (c, c), 0) >= lax.broadcasted_iota(jnp.int32, (c, c), 1))
    heads = [slice(h * dk, (h + 1) * dk) for h in range(HGRN_HEADS)]

    def per_chunk_rows(x, offset):
        rid = lax.broadcasted_iota(jnp.int32, x.shape, 0)
        out = jnp.broadcast_to(x[offset:offset + 1, :], x.shape)
        for u in range(1, nu):
            out = jnp.where(rid >= u * c, jnp.broadcast_to(x[u * c + offset:u * c + offset + 1, :], x.shape), out)
        return out

    def body(i, carry):
        rows = pl.ds(pl.multiple_of(i * r, r), r)
        f = lb_all + (1.0 - lb_all) * _sigmoid(f_ref[0, rows, :])
        k = 1.0 - f
        parts = jnp.concatenate(_split3(jnp.log2(f)), axis=0)
        b = _dot(tril3, parts)
        b_mid = per_chunk_rows(b, c // 2 - 1)
        b_last = per_chunk_rows(b, c - 1)
        d_mid = b - b_mid
        qr = q_ref[0, rows, :] * jnp.exp2(d_mid)
        kr = k * jnp.exp2(-d_mid)
        q_rel = qr.astype(BF16)
        k_rel = kr.astype(BF16)
        q_dec = (qr * jnp.exp2(b_mid)).astype(BF16)
        k_dec = (kr * jnp.exp2(b_last - b_mid)).astype(BF16)
        v = i_ref[0, rows, :]
        gate = nw_ref[...] * _silu(g_ref[0, rows, :])
        st = [st_ref[h] for h in range(HGRN_HEADS)]
        for u in range(nu):
            cr = slice(u * c, (u + 1) * c)
            decay = jnp.exp2(b[(u + 1) * c - 1:(u + 1) * c, :])
            a = [jnp.where(causal, _dot_nt(q_rel[cr, hs], k_rel[cr, hs]), 0.0).astype(BF16)
                 for hs in heads]
            o = [_dot(a[h], v[cr, hs]) + _dot_nt(q_dec[cr, hs], st[h].astype(BF16))
                 for h, hs in enumerate(heads)]
            st = [decay[:, hs] * st[h] + _dot_tn(v[cr, hs], k_dec[cr, hs]) for h, hs in enumerate(heads)]
            for h, hs in enumerate(heads):
                oh = o[h] * lax.rsqrt(jnp.mean(o[h] * o[h], axis=-1, keepdims=True) + EPS)
                o_ref[0, pl.ds(pl.multiple_of(i * r + u * c, c), c), hs] = (
                    oh * gate[cr, hs]).astype(o_ref.dtype)
        for h in range(HGRN_HEADS):
            st_ref[h] = st[h]
        return carry

    lax.fori_loop(0, HGRN_TC // r, body, 0)


def _hgrn(p32, p16, lower_bounds, norm_w, layer, wa, wb):
    b, s, _ = p32.shape
    nt = s // HGRN_TC
    blk = lambda j: pl.BlockSpec((1, HGRN_TC, HGRN_WIDTH), lambda bi, t, j=j: (bi, t, j))
    wa_spec, wa_shape = _cast_rider_specs(wa, b * nt, nt)
    wb_spec, wb_shape = _cast_rider_specs(wb, b * nt, nt)
    return pl.pallas_call(
        functools.partial(_hgrn_kernel, layer=layer),
        grid=(b, nt),
        in_specs=[blk(0), blk(1), blk(0), blk(2),
                  pl.BlockSpec(lower_bounds.shape, lambda bi, t: (0, 0)),
                  pl.BlockSpec((1, HGRN_WIDTH), lambda bi, t: (0, 0)),
                  wa_spec, wb_spec],
        out_specs=[pl.BlockSpec((1, HGRN_TC, HGRN_WIDTH), lambda bi, t: (bi, t, 0)), wa_spec, wb_spec],
        out_shape=[jax.ShapeDtypeStruct((b, s, HGRN_WIDTH), BF16), wa_shape, wb_shape],
        scratch_shapes=[pltpu.VMEM((HGRN_HEADS, HGRN_HEAD_DIM, HGRN_HEAD_DIM), F32)],
        compiler_params=_params("arbitrary", "arbitrary"),
        name="hgrn",
    )(p32, p32, p16, p32, lower_bounds, norm_w, wa, wb)


def _t5_causal_buckets(dist):
    max_exact = REL_BUCKETS // 2
    d = np.maximum(dist, 0)
    log_b = max_exact + (np.log(np.maximum(d, 1) / max_exact)
                         / np.log(REL_MAX_DIST / max_exact)
                         * (REL_BUCKETS - max_exact)).astype(np.int32)
    log_b = np.minimum(log_b, REL_BUCKETS - 1)
    return np.where(d < max_exact, d, log_b).astype(np.int32)


def _bias_kernel(tab_ref, sink_ref, bucket_ref, valid_ref, o_ref):
    h = pl.program_id(0)
    bucket = bucket_ref[...]
    acc = jnp.zeros(bucket.shape, F32)
    for bk in range(REL_BUCKETS):
        acc = jnp.where(bucket == bk, tab_ref[bk, h], acc)
    sink_col = lax.broadcasted_iota(jnp.int32, bucket.shape, 1) == 0
    for variant in range(2):
        masked = jnp.where(valid_ref[variant] != 0, acc, NEG_INF)
        o_ref[variant, 0] = jnp.where(sink_col, sink_ref[h], masked)


def _bias_table(rel_table, sinks):
    l = ATTN_BLOCK
    qi = np.arange(l)[:, None]
    kj = np.arange(2 * l)[None, :]
    dist = qi + l - kj
    in_window = (dist >= 0) & (dist < WINDOW)
    valid = np.stack([in_window & (kj >= l), in_window]).astype(np.int32)
    assert not valid[:, :, 0].any()
    bucket = _t5_causal_buckets(dist)
    return pl.pallas_call(
        _bias_kernel,
        grid=(ATTN_Q_HEADS,),
        in_specs=[pl.BlockSpec(memory_space=pltpu.SMEM),
                  pl.BlockSpec(memory_space=pltpu.SMEM),
                  pl.BlockSpec((l, 2 * l), lambda h: (0, 0)),
                  pl.BlockSpec((2, l, 2 * l), lambda h: (0, 0, 0))],
        out_specs=pl.BlockSpec((2, 1, l, 2 * l), lambda h: (0, h, 0, 0)),
        out_shape=jax.ShapeDtypeStruct((2, ATTN_Q_HEADS, l, 2 * l), F32),
        compiler_params=_params("arbitrary"),
        name="bias",
    )(rel_table, sinks, jnp.asarray(bucket), jnp.asarray(valid))


BF16_TILE_ROWS = 16


def _zero_first_row(a):
    top = a[:BF16_TILE_ROWS]
    row = lax.broadcasted_iota(jnp.int32, top.shape, 0)
    return jnp.concatenate([jnp.where(row == 0, 0.0, top).astype(a.dtype), a[BF16_TILE_ROWS:]], axis=0)


def _attn_kernel(q_ref, kp_ref, kc_ref, vp_ref, vc_ref, bias_ref, wa_ref, wb_ref,
                 o_ref, wa16_ref, wb16_ref):
    l = ATTN_BLOCK
    dh = ATTN_HEAD_DIM
    wa16_ref[...] = wa_ref[...].astype(BF16)
    wb16_ref[...] = wb_ref[...].astype(BF16)
    lane = lax.broadcasted_iota(jnp.int32, (1, 2 * dh), 1)
    low = lane < dh
    first = jnp.minimum(pl.program_id(1), 1)

    for pair in range(ATTN_KV_HEADS // 2):
        pc = slice(pair * 2 * dh, (pair + 1) * 2 * dh)
        kk = pltpu.bitcast(jnp.concatenate([kp_ref[0, :, pc], kc_ref[0, :, pc]], axis=0), jnp.int32)
        vv = pltpu.bitcast(jnp.concatenate([vp_ref[0, :, pc], vc_ref[0, :, pc]], axis=0), jnp.int32)
        kk_sw = pltpu.roll(kk, dh, axis=1)
        vv_sw = pltpu.roll(vv, dh, axis=1)
        for sub in range(2):
            kvh = pair * 2 + sub
            if sub == 0:
                k2_all = jnp.where(low, kk, kk_sw)
                v2_all = jnp.where(low, vv, vv_sw)
            else:
                k2_all = jnp.where(low, kk_sw, kk)
                v2_all = jnp.where(low, vv_sw, vv)
            k2_all = pltpu.bitcast(k2_all, BF16)
            v2_all = pltpu.bitcast(v2_all, BF16)
            for t in range(ATTN_QB):
                qrows = slice(t * l, (t + 1) * l)
                k2 = _zero_first_row(k2_all[t * l:(t + 2) * l])
                v2 = _zero_first_row(v2_all[t * l:(t + 2) * l])
                qs = []
                for j in range(ATTN_GROUP):
                    hq = kvh * ATTN_GROUP + j
                    qc = slice((hq // 2) * 2 * dh, (hq // 2 + 1) * 2 * dh)
                    q2 = q_ref[0, qrows, qc] * (dh ** -0.5)
                    keep = low if hq % 2 == 0 else jnp.logical_not(low)
                    qs.append(jnp.where(keep, q2, 0.0).astype(BF16))
                s_all = _dot_nt(jnp.concatenate(qs, axis=0), k2)
                ps = []
                for j in range(ATTN_GROUP):
                    hq = kvh * ATTN_GROUP + j
                    bias = bias_ref[first, hq] if t == 0 else bias_ref[1, hq]
                    s = s_all[j * l:(j + 1) * l] + bias
                    p = jnp.exp(s - jnp.max(s, axis=-1, keepdims=True))
                    ps.append(p.astype(BF16))
                v2x = jnp.concatenate([v2, jnp.ones_like(v2)], axis=1)
                o_all = _dot(jnp.concatenate(ps, axis=0), v2x)
                o_all = o_all[:, :2 * dh] / o_all[:, 2 * dh:]
                for jp in range(ATTN_GROUP // 2):
                    hq = kvh * ATTN_GROUP + 2 * jp
                    oe = o_all[(2 * jp) * l:(2 * jp + 1) * l]
                    oo = o_all[(2 * jp + 1) * l:(2 * jp + 2) * l]
                    oc = slice((hq // 2) * 2 * dh, (hq // 2 + 1) * 2 * dh)
                    o_ref[0, qrows, oc] = jnp.where(low, oe, oo).astype(o_ref.dtype)


def _cast_rider_specs(w, steps, steps_per_batch):
    rows = w.shape[0] // steps
    assert rows * steps == w.shape[0] and rows % 8 == 0
    spec = pl.BlockSpec((rows, w.shape[1]), lambda bi, n: (bi * steps_per_batch + n, 0))
    return spec, jax.ShapeDtypeStruct(w.shape, BF16)


def _attn(p16, bias, wa, wb):
    b, s, _ = p16.shape
    l = ATTN_BLOCK
    tq = ATTN_QB * l
    nq = s // tq
    q_blk = HGRN_WIDTH // ATTN_WIDTH
    k_blk = (HGRN_WIDTH + ATTN_WIDTH) // KV_WIDTH
    v_blk = k_blk + 1
    prev_spec = lambda blk: pl.BlockSpec(
        (1, l, KV_WIDTH), lambda bi, n: (bi, jnp.maximum(n * ATTN_QB - 1, 0), blk))
    cur_spec = lambda blk: pl.BlockSpec((1, tq, KV_WIDTH), lambda bi, n: (bi, n, blk))
    wa_spec, wa_shape = _cast_rider_specs(wa, b * nq, nq)
    wb_spec, wb_shape = _cast_rider_specs(wb, b * nq, nq)
    return pl.pallas_call(
        _attn_kernel,
        grid=(b, nq),
        in_specs=[pl.BlockSpec((1, tq, ATTN_WIDTH), lambda bi, n: (bi, n, q_blk)),
                  prev_spec(k_blk), cur_spec(k_blk), prev_spec(v_blk), cur_spec(v_blk),
                  pl.BlockSpec(bias.shape, lambda bi, n: (0, 0, 0, 0), pipeline_mode=pl.Buffered(1)),
                  wa_spec, wb_spec],
        out_specs=[pl.BlockSpec((1, tq, ATTN_WIDTH), lambda bi, n: (bi, n, 0)), wa_spec, wb_spec],
        out_shape=[jax.ShapeDtypeStruct((b, s, ATTN_WIDTH), BF16), wa_shape, wb_shape],
        compiler_params=_params("arbitrary", "arbitrary"),
        name="attn",
    )(p16, p16, p16, p16, p16, bias, wa, wb)


def _outproj_kernel(orec_ref, oatt_ref, x_ref, mod_ref, nw_ref, w_ref, x1_ref, h2_ref, rs_ref):
    ssq = None
    for c0 in range(0, D_MODEL, OUTPROJ_TN):
        cs = slice(c0, c0 + OUTPROJ_TN)
        y = _dot(orec_ref[0], w_ref[0:HGRN_WIDTH, cs]) + _dot(oatt_ref[0], w_ref[HGRN_WIDTH:, cs])
        x1 = x_ref[0, :, cs] + mod_ref[0, 2:3, cs] * y
        x1_ref[0, :, cs] = x1
        sq = x1 * x1
        for k0 in range(0, OUTPROJ_TN, 128):
            ssq = sq[:, k0:k0 + 128] if ssq is None else ssq + sq[:, k0:k0 + 128]
    rs_ref[...] = lax.rsqrt(jnp.sum(ssq, axis=-1, keepdims=True) * (1.0 / D_MODEL) + EPS)

    shift = mod_ref[0, 3:4, :]
    wmod = nw_ref[...] * (1.0 + mod_ref[0, 4:5, :])

    def chunk(rows):
        h2_ref[0, rows, :] = (x1_ref[0, rows, :] * rs_ref[rows, :] * wmod + shift).astype(BF16)

    _for_row_chunks(OUTPROJ_TM, chunk)


def _outproj(o_rec, o_att, x, mod, norm_w, w_bf):
    b, s, d = x.shape
    tm = OUTPROJ_TM
    return pl.pallas_call(
        _outproj_kernel,
        grid=(b, s // tm),
        in_specs=[pl.BlockSpec((1, tm, HGRN_WIDTH), lambda bi, m: (bi, m, 0)),
                  pl.BlockSpec((1, tm, ATTN_WIDTH), lambda bi, m: (bi, m, 0)),
                  pl.BlockSpec((1, tm, d), lambda bi, m: (bi, m, 0)),
                  pl.BlockSpec((1, 6, d), lambda bi, m: (bi, 0, 0)),
                  pl.BlockSpec((1, d), lambda bi, m: (0, 0)),
                  pl.BlockSpec(w_bf.shape, lambda bi, m: (0, 0), pipeline_mode=pl.Buffered(1))],
        out_specs=[pl.BlockSpec((1, tm, d), lambda bi, m: (bi, m, 0)),
                   pl.BlockSpec((1, tm, d), lambda bi, m: (bi, m, 0))],
        out_shape=[jax.ShapeDtypeStruct((b, s, d), F32),
                   jax.ShapeDtypeStruct((b, s, d), BF16)],
        scratch_shapes=[pltpu.VMEM((tm, 1), F32)],
        compiler_params=_params("parallel", "parallel"),
        name="outproj",
    )(o_rec, o_att, x, mod, norm_w, w_bf)


def _ffn_kernel(h_ref, x1_hbm, mod_ref, fw_ref, wg_ref, wu_ref, wd_ref, o_ref, x1_buf, stash_ref, x1_sem,
                *, final_norm):
    bi, m, j = pl.program_id(0), pl.program_id(1), pl.program_id(2)

    def x1_copy():
        rows = pl.ds(pl.multiple_of(m * FFN_TM, FFN_TM), FFN_TM)
        return pltpu.make_async_copy(x1_hbm.at[bi, rows, :], x1_buf, x1_sem)

    @pl.when(j == 0)
    def _():
        x1_copy().start()
        o_ref[...] = jnp.zeros_like(o_ref)

    part = None
    for s in range(FFN_TF // FFN_SUB):
        cs = slice(s * FFN_SUB, (s + 1) * FFN_SUB)
        act = (_silu(_dot(h_ref[0], wg_ref[:, cs])) * _dot(h_ref[0], wu_ref[:, cs])).astype(BF16)
        d = _dot(act, wd_ref[cs, :])
        part = d if part is None else part + d
    o_ref[0] += part

    @pl.when(j == pl.num_programs(2) - 1)
    def _():
        x1_copy().wait()
        gate = mod_ref[0, 5:6, :]
        fw = fw_ref[...]
        group = ROW_CHUNK * ROW_UNROLL

        def body(i, carry):
            base = pl.multiple_of(i * group, group)
            scales = []
            for k in range(ROW_UNROLL):
                rows = pl.ds(base + k * ROW_CHUNK, ROW_CHUNK)
                x2 = x1_buf[rows, :] + gate * o_ref[0, rows, :]
                stash_ref[k * ROW_CHUNK:(k + 1) * ROW_CHUNK, :] = x2
                if final_norm:
                    scales.append(lax.rsqrt(jnp.mean(x2 * x2, axis=-1, keepdims=True) + EPS))
            for k in range(ROW_UNROLL):
                rows = pl.ds(base + k * ROW_CHUNK, ROW_CHUNK)
                x2 = stash_ref[k * ROW_CHUNK:(k + 1) * ROW_CHUNK, :]
                o_ref[0, rows, :] = x2 * scales[k] * fw if final_norm else x2
            return carry

        lax.fori_loop(0, FFN_TM // group, body, 0)


def _ffn(h2, x1, mod, final_w, wg, wu, wd, final_norm):
    b, s, d = x1.shape
    tm, tf = FFN_TM, FFN_TF
    return pl.pallas_call(
        functools.partial(_ffn_kernel, final_norm=final_norm),
        grid=(b, s // tm, D_FF // tf),
        in_specs=[pl.BlockSpec((1, tm, d), lambda bi, m, j: (bi, m, 0)),
                  pl.BlockSpec(memory_space=pl.ANY),
                  pl.BlockSpec((1, 6, d), lambda bi, m, j: (bi, 0, 0)),
                  pl.BlockSpec((1, d), lambda bi, m, j: (0, 0)),
                  pl.BlockSpec((d, tf), lambda bi, m, j: (0, j)),
                  pl.BlockSpec((d, tf), lambda bi, m, j: (0, j)),
                  pl.BlockSpec((tf, d), lambda bi, m, j: (j, 0))],
        out_specs=pl.BlockSpec((1, tm, d), lambda bi, m, j: (bi, m, 0)),
        out_shape=jax.ShapeDtypeStruct((b, s, d), F32),
        scratch_shapes=[pltpu.VMEM((tm, d), F32), pltpu.VMEM((ROW_CHUNK * ROW_UNROLL, d), F32),
                        pltpu.SemaphoreType.DMA(())],
        compiler_params=_params("arbitrary", "arbitrary", "arbitrary"),
        name="ffn",
    )(h2, x1, mod, final_w, wg, wu, wd)


def kernel(x, c, w_ada, b_ada, norm1_w, w_in, lower_bounds, hgrn_norm_w, attn_sinks,
           rel_bias_table, w_out, norm2_w, w_gate, w_up, w_down, final_norm_w):
    b, s, d = x.shape
    assert (d, w_in.shape[-1], w_gate.shape[-1]) == (D_MODEL, IN_WIDTH, D_FF)
    assert s % max(INPROJ_TM, HGRN_TC, OUTPROJ_TM, FFN_TM, ATTN_QB * ATTN_BLOCK) == 0
    depth = w_ada.shape[0]
    c8 = jnp.pad(c, ((0, 8 - b), (0, 0)))
    for layer in range(depth):
        bias = _bias_table(rel_bias_table, attn_sinks[layer])
        mod = _ada(c8, w_ada[layer], b_ada[layer][None, :])[:b].reshape(b, 6, d)
        p32, p16 = _inproj(x, mod, norm1_w[layer][None, :], w_in[layer].astype(BF16))
        o_rec, wd16, wo16 = _hgrn(p32, p16, lower_bounds, hgrn_norm_w[layer][None, :], layer,
                                  w_down[layer], w_out[layer])
        o_att, wg16, wu16 = _attn(p16, bias, w_gate[layer], w_up[layer])
        x1, h2 = _outproj(o_rec, o_att, x, mod, norm2_w[layer][None, :], wo16)
        x = _ffn(h2, x1, mod, final_norm_w[None, :], wg16, wu16, wd16,
                 final_norm=(layer == depth - 1))
    return x
```

```python
import functools

import numpy as np
import jax
import jax.numpy as jnp
from jax import lax
from jax.experimental import pallas as pl
from jax.experimental.pallas import tpu as pltpu

D_MODEL = 2048
DEPTH = 1
HGRN_WIDTH = 1024
HGRN_HEAD_DIM = 128
HGRN_HEADS = HGRN_WIDTH // HGRN_HEAD_DIM
HGRN_CHUNK = 64
ATTN_WIDTH = 1024
ATTN_HEAD_DIM = 64
ATTN_Q_HEADS = ATTN_WIDTH // ATTN_HEAD_DIM
ATTN_KV_HEADS = 4
ATTN_GROUP = ATTN_Q_HEADS // ATTN_KV_HEADS
WINDOW = 128
ATTN_BLOCK = 128
ATTN_QB = 2
REL_BUCKETS = 32
REL_MAX_DIST = 128
D_FF = 5632
KV_WIDTH = ATTN_KV_HEADS * ATTN_HEAD_DIM
IN_WIDTH = 4 * HGRN_WIDTH + ATTN_WIDTH + 2 * KV_WIDTH
P32_WIDTH = 3 * HGRN_WIDTH
P16_WIDTH = HGRN_WIDTH + ATTN_WIDTH + 2 * KV_WIDTH
EPS = 1e-6
NEG_INF = -1e30
LOG2E = 1.4426950408889634

F32 = jnp.float32
BF16 = jnp.bfloat16

VMEM_LIMIT_BYTES = 56 * 1024 * 1024
BF16_TILE_ROWS = 16

ADA_TN = 1024
INPROJ_TM = 512
INPROJ_TN = 512
HGRN_TC = 512
HGRN_UNROLL = 4
OUTPROJ_TM = 512
OUTPROJ_TN = 256
FFN_TM = 1024
FFN_TF = 512
FFN_SUB = 256
FFN_DN = 512
ROW_CHUNK = 16
ROW_UNROLL = 8


def _params(*semantics):
    return pltpu.CompilerParams(dimension_semantics=semantics,
                                vmem_limit_bytes=VMEM_LIMIT_BYTES)


def _sigmoid(v):
    return 1.0 / (1.0 + jnp.exp2(v * (-LOG2E)))


def _silu(v):
    return v * _sigmoid(v)


def _dot(a, b):
    return jnp.dot(a, b, preferred_element_type=F32)


def _dot_nt(a, b):
    return lax.dot_general(a, b, (((1,), (1,)), ((), ())), preferred_element_type=F32)


def _dot_tn(a, b):
    return lax.dot_general(a, b, (((0,), (0,)), ((), ())), preferred_element_type=F32)


def _ada_kernel(c_ref, w_ref, b_ref, o_ref):
    c_act = _silu(c_ref[...])
    o_ref[...] = _dot(c_act.astype(BF16), w_ref[...].astype(BF16)) + b_ref[...]


def _ada(c8, w, b):
    n = w.shape[1]
    return pl.pallas_call(
        _ada_kernel,
        grid=(n // ADA_TN,),
        in_specs=[pl.BlockSpec((8, D_MODEL), lambda j: (0, 0)),
                  pl.BlockSpec((D_MODEL, ADA_TN), lambda j: (0, j)),
                  pl.BlockSpec((1, ADA_TN), lambda j: (0, j))],
        out_specs=pl.BlockSpec((8, ADA_TN), lambda j: (0, j)),
        out_shape=jax.ShapeDtypeStruct((8, n), F32),
        compiler_params=_params("arbitrary"),
        name="ada",
    )(c8, w, b)


def _modulated_norm(x, wmod, shift):
    return x * lax.rsqrt(jnp.mean(x * x, axis=-1, keepdims=True) + EPS) * wmod + shift


def _for_row_chunks(n_rows, fn):
    def body(i, carry):
        fn(pl.ds(pl.multiple_of(i * ROW_CHUNK, ROW_CHUNK), ROW_CHUNK))
        return carry

    lax.fori_loop(0, n_rows // ROW_CHUNK, body, 0, unroll=ROW_UNROLL)


_INPROJ_SEGMENTS = (
    (0, 0, 0, 2 * HGRN_WIDTH),
    (0, 2 * HGRN_WIDTH, 3 * HGRN_WIDTH, HGRN_WIDTH),
    (1, 0, 2 * HGRN_WIDTH, HGRN_WIDTH),
    (1, HGRN_WIDTH, 4 * HGRN_WIDTH, ATTN_WIDTH + 2 * KV_WIDTH),
)


def _inproj_kernel(x_ref, mod_ref, nw_ref, w_ref, o32_ref, o16_ref, h_ref):
    shift = mod_ref[0, 0:1, :]
    wmod = nw_ref[...] * (1.0 + mod_ref[0, 1:2, :])

    def chunk(rows):
        h_ref[rows, :] = _modulated_norm(x_ref[0, rows, :], wmod, shift).astype(BF16)

    _for_row_chunks(INPROJ_TM, chunk)

    outs = (o32_ref, o16_ref)
    for dst, dst_col, src_col, width in _INPROJ_SEGMENTS:
        for off in range(0, width, INPROJ_TN):
            res = _dot(h_ref[...], w_ref[:, src_col + off:src_col + off + INPROJ_TN])
            outs[dst][0, :, dst_col + off:dst_col + off + INPROJ_TN] = res.astype(outs[dst].dtype)


def _inproj(x, mod, norm_w, w_bf):
    b, s, d = x.shape
    tm = INPROJ_TM
    return pl.pallas_call(
        _inproj_kernel,
        grid=(b, s // tm),
        in_specs=[pl.BlockSpec((1, tm, d), lambda bi, m: (bi, m, 0)),
                  pl.BlockSpec((1, 6, d), lambda bi, m: (bi, 0, 0)),
                  pl.BlockSpec((1, d), lambda bi, m: (0, 0)),
                  pl.BlockSpec(w_bf.shape, lambda bi, m: (0, 0), pipeline_mode=pl.Buffered(1))],
        out_specs=[pl.BlockSpec((1, tm, P32_WIDTH), lambda bi, m: (bi, m, 0)),
                   pl.BlockSpec((1, tm, P16_WIDTH), lambda bi, m: (bi, m, 0))],
        out_shape=[jax.ShapeDtypeStruct((b, s, P32_WIDTH), F32),
                   jax.ShapeDtypeStruct((b, s, P16_WIDTH), BF16)],
        scratch_shapes=[pltpu.VMEM((tm, d), BF16)],
        compiler_params=_params("parallel", "parallel"),
        name="inproj",
    )(x, mod, norm_w, w_bf)


def _split3(v):
    hi = v.astype(BF16)
    r1 = v - hi.astype(F32)
    mid = r1.astype(BF16)
    lo = (r1 - mid.astype(F32)).astype(BF16)
    return hi, mid, lo


def _hgrn_kernel(q_ref, f_ref, i_ref, g_ref, lb_ref, nw_ref, wa_ref, wb_ref,
                 o_ref, wa16_ref, wb16_ref, st_ref, *, layer):
    c = HGRN_CHUNK
    dk = HGRN_HEAD_DIM
    wa16_ref[...] = wa_ref[...].astype(BF16)
    wb16_ref[...] = wb_ref[...].astype(BF16)

    @pl.when(pl.program_id(1) == 0)
    def _():
        st_ref[...] = jnp.zeros_like(st_ref)

    lb_rows = [lb_ref[r:r + 1, :] for r in range(lb_ref.shape[0])]
    lb_max = functools.reduce(jnp.maximum, lb_rows)
    lb_exp = [jnp.exp(r - lb_max) for r in lb_rows]
    lb_all = sum(lb_exp[:layer + 1]) / sum(lb_exp)

    nu = HGRN_UNROLL
    r = nu * c
    row = lax.broadcasted_iota(jnp.int32, (r, r), 0)
    col = lax.broadcasted_iota(jnp.int32, (r, r), 1)
    same_chunk = functools.reduce(
        jnp.logical_or, [(row >= u * c) & (row < (u + 1) * c) & (col >= u * c) & (col < (u + 1) * c)
                         for u in range(nu)])
    tril = jnp.where(same_chunk & (row >= col), 1.0, 0.0).astype(BF16)
    tril3 = jnp.concatenate([tril, tril, tril], axis=1)
    causal = (lax.broadcasted_iota(jnp.int32, (c, c), 0) >= lax.broadcasted_iota(jnp.int32, (c, c), 1))
    heads = [slice(h * dk, (h + 1) * dk) for h in range(HGRN_HEADS)]

    def per_chunk_rows(x, offset):
        rid = lax.broadcasted_iota(jnp.int32, x.shape, 0)
        out = jnp.broadcast_to(x[offset:offset + 1, :], x.shape)
        for u in range(1, nu):
            out = jnp.where(rid >= u * c, jnp.broadcast_to(x[u * c + offset:u * c + offset + 1, :], x.shape), out)
        return out

    def body(i, carry):
        rows = pl.ds(pl.multiple_of(i * r, r), r)
        f = lb_all + (1.0 - lb_all) * _sigmoid(f_ref[0, rows, :])
        k = 1.0 - f
        parts = jnp.concatenate(_split3(jnp.log2(f)), axis=0)
        b = _dot(tril3, parts)
        b_mid = per_chunk_rows(b, c // 2 - 1)
        b_last = per_chunk_rows(b, c - 1)
        d_mid = b - b_mid
        qr = q_ref[0, rows, :] * jnp.exp2(d_mid)
        kr = k * jnp.exp2(-d_mid)
        q_rel = qr.astype(BF16)
        k_rel = kr.astype(BF16)
        q_dec = (qr * jnp.exp2(b_mid)).astype(BF16)
        k_dec = (kr * jnp.exp2(b_last - b_mid)).astype(BF16)
        v = i_ref[0, rows, :]
        gate = nw_ref[...] * _silu(g_ref[0, rows, :])
        st = [st_ref[h] for h in range(HGRN_HEADS)]
        for u in range(nu):
            cr = slice(u * c, (u + 1) * c)
            decay = jnp.exp2(b[(u + 1) * c - 1:(u + 1) * c, :])
            a = [jnp.where(causal, _dot_nt(q_rel[cr, hs], k_rel[cr, hs]), 0.0).astype(BF16)
                 for hs in heads]
            o = [_dot(a[h], v[cr, hs]) + _dot_nt(q_dec[cr, hs], st[h].astype(BF16))
                 for h, hs in enumerate(heads)]
            st = [decay[:, hs] * st[h] + _dot_tn(v[cr, hs], k_dec[cr, hs]) for h, hs in enumerate(heads)]
            for h, hs in enumerate(heads):
                oh = o[h] * lax.rsqrt(jnp.mean(o[h] * o[h], axis=-1, keepdims=True) + EPS)
                o_ref[0, pl.ds(pl.multiple_of(i * r + u * c, c), c), hs] = (
                    oh * gate[cr, hs]).astype(o_ref.dtype)
        for h in range(HGRN_HEADS):
            st_ref[h] = st[h]
        return carry

    lax.fori_loop(0, HGRN_TC // r, body, 0)


def _hgrn(p32, p16, lower_bounds, norm_w, layer, wa, wb):
    b, s, _ = p32.shape
    nt = s // HGRN_TC
    blk = lambda j: pl.BlockSpec((1, HGRN_TC, HGRN_WIDTH), lambda bi, t, j=j: (bi, t, j))
    wa_spec, wa_shape = _cast_rider_specs(wa, b * nt, nt)
    wb_spec, wb_shape = _cast_rider_specs(wb, b * nt, nt)
    return pl.pallas_call(
        functools.partial(_hgrn_kernel, layer=layer),
        grid=(b, nt),
        in_specs=[blk(0), blk(1), blk(0), blk(2),
                  pl.BlockSpec(lower_bounds.shape, lambda bi, t: (0, 0)),
                  pl.BlockSpec((1, HGRN_WIDTH), lambda bi, t: (0, 0)),
                  wa_spec, wb_spec],
        out_specs=[pl.BlockSpec((1, HGRN_TC, HGRN_WIDTH), lambda bi, t: (bi, t, 0)), wa_spec, wb_spec],
        out_shape=[jax.ShapeDtypeStruct((b, s, HGRN_WIDTH), BF16), wa_shape, wb_shape],
        scratch_shapes=[pltpu.VMEM((HGRN_HEADS, HGRN_HEAD_DIM, HGRN_HEAD_DIM), F32)],
        compiler_params=_params("arbitrary", "arbitrary"),
        name="hgrn",
    )(p32, p32, p16, p32, lower_bounds, norm_w, wa, wb)


def _t5_causal_buckets(dist):
    max_exact = REL_BUCKETS // 2
    d = np.maximum(dist, 0)
    log_b = max_exact + (np.log(np.maximum(d, 1) / max_exact)
                         / np.log(REL_MAX_DIST / max_exact)
                         * (REL_BUCKETS - max_exact)).astype(np.int32)
    log_b = np.minimum(log_b, REL_BUCKETS - 1)
    return np.where(d < max_exact, d, log_b).astype(np.int32)


def _bias_kernel(tab_ref, sink_ref, bucket_ref, valid_ref, w_ref, o_ref, w16_ref):
    w16_ref[...] = w_ref[...].astype(BF16)
    h = pl.program_id(0)
    bucket = bucket_ref[...]
    acc = jnp.zeros(bucket.shape, F32)
    for bk in range(REL_BUCKETS):
        acc = jnp.where(bucket == bk, tab_ref[bk, h], acc)
    sink_col = lax.broadcasted_iota(jnp.int32, bucket.shape, 1) == 0
    for variant in range(2):
        masked = jnp.where(valid_ref[variant] != 0, acc, NEG_INF)
        o_ref[variant, 0] = jnp.where(sink_col, sink_ref[h], masked)


def _bias_table(rel_table, sinks, w):
    l = ATTN_BLOCK
    qi = np.arange(l)[:, None]
    kj = np.arange(2 * l)[None, :]
    dist = qi + l - kj
    in_window = (dist >= 0) & (dist < WINDOW)
    valid = np.stack([in_window & (kj >= l), in_window]).astype(np.int32)
    assert not valid[:, :, 0].any()
    bucket = _t5_causal_buckets(dist)
    w_rows = w.shape[0] // ATTN_Q_HEADS
    assert w_rows * ATTN_Q_HEADS == w.shape[0] and w_rows % BF16_TILE_ROWS == 0
    w_spec = pl.BlockSpec((w_rows, w.shape[1]), lambda h: (h, 0))
    return pl.pallas_call(
        _bias_kernel,
        grid=(ATTN_Q_HEADS,),
        in_specs=[pl.BlockSpec(memory_space=pltpu.SMEM),
                  pl.BlockSpec(memory_space=pltpu.SMEM),
                  pl.BlockSpec((l, 2 * l), lambda h: (0, 0)),
                  pl.BlockSpec((2, l, 2 * l), lambda h: (0, 0, 0)),
                  w_spec],
        out_specs=[pl.BlockSpec((2, 1, l, 2 * l), lambda h: (0, h, 0, 0)), w_spec],
        out_shape=[jax.ShapeDtypeStruct((2, ATTN_Q_HEADS, l, 2 * l), F32),
                   jax.ShapeDtypeStruct(w.shape, BF16)],
        compiler_params=_params("arbitrary"),
        name="bias",
    )(rel_table, sinks, jnp.asarray(bucket), jnp.asarray(valid), w)


def _zero_first_row(a):
    top = a[:BF16_TILE_ROWS]
    row = lax.broadcasted_iota(jnp.int32, top.shape, 0)
    return jnp.concatenate([jnp.where(row == 0, 0.0, top).astype(a.dtype), a[BF16_TILE_ROWS:]], axis=0)


def _attn_kernel(q_ref, kp_ref, kc_ref, vp_ref, vc_ref, bias_ref, wa_ref, wb_ref,
                 o_ref, wa16_ref, wb16_ref):
    l = ATTN_BLOCK
    dh = ATTN_HEAD_DIM
    wa16_ref[...] = wa_ref[...].astype(BF16)
    wb16_ref[...] = wb_ref[...].astype(BF16)
    lane = lax.broadcasted_iota(jnp.int32, (1, 2 * dh), 1)
    low = lane < dh
    first = jnp.minimum(pl.program_id(1), 1)

    for pair in range(ATTN_KV_HEADS // 2):
        pc = slice(pair * 2 * dh, (pair + 1) * 2 * dh)
        kk = pltpu.bitcast(jnp.concatenate([kp_ref[0, :, pc], kc_ref[0, :, pc]], axis=0), jnp.int32)
        vv = pltpu.bitcast(jnp.concatenate([vp_ref[0, :, pc], vc_ref[0, :, pc]], axis=0), jnp.int32)
        kk_sw = pltpu.roll(kk, dh, axis=1)
        vv_sw = pltpu.roll(vv, dh, axis=1)
        for sub in range(2):
            kvh = pair * 2 + sub
            if sub == 0:
                k2_all = jnp.where(low, kk, kk_sw)
                v2_all = jnp.where(low, vv, vv_sw)
            else:
                k2_all = jnp.where(low, kk_sw, kk)
                v2_all = jnp.where(low, vv_sw, vv)
            k2_all = pltpu.bitcast(k2_all, BF16)
            v2_all = pltpu.bitcast(v2_all, BF16)
            for t in range(ATTN_QB):
                qrows = slice(t * l, (t + 1) * l)
                k2 = _zero_first_row(k2_all[t * l:(t + 2) * l])
                v2 = _zero_first_row(v2_all[t * l:(t + 2) * l])
                qs = []
                for j in range(ATTN_GROUP):
                    hq = kvh * ATTN_GROUP + j
                    qc = slice((hq // 2) * 2 * dh, (hq // 2 + 1) * 2 * dh)
                    q2 = q_ref[0, qrows, qc] * (dh ** -0.5)
                    keep = low if hq % 2 == 0 else jnp.logical_not(low)
                    qs.append(jnp.where(keep, q2, 0.0).astype(BF16))
                s_all = _dot_nt(jnp.concatenate(qs, axis=0), k2)
                ps = []
                for j in range(ATTN_GROUP):
                    hq = kvh * ATTN_GROUP + j
                    bias = bias_ref[first, hq] if t == 0 else bias_ref[1, hq]
                    s = s_all[j * l:(j + 1) * l] + bias
                    p = jnp.exp(s - jnp.max(s, axis=-1, keepdims=True))
                    ps.append(p.astype(BF16))
                v2x = jnp.concatenate([v2, jnp.ones_like(v2)], axis=1)
                o_all = _dot(jnp.concatenate(ps, axis=0), v2x)
                o_all = o_all[:, :2 * dh] / o_all[:, 2 * dh:]
                for jp in range(ATTN_GROUP // 2):
                    hq = kvh * ATTN_GROUP + 2 * jp
                    oe = o_all[(2 * jp) * l:(2 * jp + 1) * l]
                    oo = o_all[(2 * jp + 1) * l:(2 * jp + 2) * l]
                    oc = slice((hq // 2) * 2 * dh, (hq // 2 + 1) * 2 * dh)
                    o_ref[0, qrows, oc] = jnp.where(low, oe, oo).astype(o_ref.dtype)


def _cast_rider_specs(w, steps, steps_per_batch):
    rows = w.shape[0] // steps
    assert rows * steps == w.shape[0] and rows % 8 == 0
    spec = pl.BlockSpec((rows, w.shape[1]), lambda bi, n: (bi * steps_per_batch + n, 0))
    return spec, jax.ShapeDtypeStruct(w.shape, BF16)


def _attn(p16, bias, wa, wb):
    b, s, _ = p16.shape
    l = ATTN_BLOCK
    tq = ATTN_QB * l
    nq = s // tq
    q_blk = HGRN_WIDTH // ATTN_WIDTH
    k_blk = (HGRN_WIDTH + ATTN_WIDTH) // KV_WIDTH
    v_blk = k_blk + 1
    prev_spec = lambda blk: pl.BlockSpec(
        (1, l, KV_WIDTH), lambda bi, n: (bi, jnp.maximum(n * ATTN_QB - 1, 0), blk))
    cur_spec = lambda blk: pl.BlockSpec((1, tq, KV_WIDTH), lambda bi, n: (bi, n, blk))
    wa_spec, wa_shape = _cast_rider_specs(wa, b * nq, nq)
    wb_spec, wb_shape = _cast_rider_specs(wb, b * nq, nq)
    return pl.pallas_call(
        _attn_kernel,
        grid=(b, nq),
        in_specs=[pl.BlockSpec((1, tq, ATTN_WIDTH), lambda bi, n: (bi, n, q_blk)),
                  prev_spec(k_blk), cur_spec(k_blk), prev_spec(v_blk), cur_spec(v_blk),
                  pl.BlockSpec(bias.shape, lambda bi, n: (0, 0, 0, 0), pipeline_mode=pl.Buffered(1)),
                  wa_spec, wb_spec],
        out_specs=[pl.BlockSpec((1, tq, ATTN_WIDTH), lambda bi, n: (bi, n, 0)), wa_spec, wb_spec],
        out_shape=[jax.ShapeDtypeStruct((b, s, ATTN_WIDTH), BF16), wa_shape, wb_shape],
        compiler_params=_params("arbitrary", "arbitrary"),
        name="attn",
    )(p16, p16, p16, p16, p16, bias, wa, wb)


def _outproj_kernel(orec_ref, oatt_ref, x_ref, mod_ref, nw_ref, w_ref, x1_ref, h2_ref, rs_ref):
    ssq = None
    for c0 in range(0, D_MODEL, OUTPROJ_TN):
        cs = slice(c0, c0 + OUTPROJ_TN)
        y = _dot(orec_ref[0], w_ref[0:HGRN_WIDTH, cs]) + _dot(oatt_ref[0], w_ref[HGRN_WIDTH:, cs])
        x1 = x_ref[0, :, cs] + mod_ref[0, 2:3, cs] * y
        x1_ref[0, :, cs] = x1
        sq = x1 * x1
        for k0 in range(0, OUTPROJ_TN, 128):
            ssq = sq[:, k0:k0 + 128] if ssq is None else ssq + sq[:, k0:k0 + 128]
    rs_ref[...] = lax.rsqrt(jnp.sum(ssq, axis=-1, keepdims=True) * (1.0 / D_MODEL) + EPS)

    shift = mod_ref[0, 3:4, :]
    wmod = nw_ref[...] * (1.0 + mod_ref[0, 4:5, :])

    def chunk(rows):
        h2_ref[0, rows, :] = (x1_ref[0, rows, :] * rs_ref[rows, :] * wmod + shift).astype(BF16)

    _for_row_chunks(OUTPROJ_TM, chunk)


def _outproj(o_rec, o_att, x, mod, norm_w, w_bf):
    b, s, d = x.shape
    tm = OUTPROJ_TM
    return pl.pallas_call(
        _outproj_kernel,
        grid=(b, s // tm),
        in_specs=[pl.BlockSpec((1, tm, HGRN_WIDTH), lambda bi, m: (bi, m, 0)),
                  pl.BlockSpec((1, tm, ATTN_WIDTH), lambda bi, m: (bi, m, 0)),
                  pl.BlockSpec((1, tm, d), lambda bi, m: (bi, m, 0)),
                  pl.BlockSpec((1, 6, d), lambda bi, m: (bi, 0, 0)),
                  pl.BlockSpec((1, d), lambda bi, m: (0, 0)),
                  pl.BlockSpec(w_bf.shape, lambda bi, m: (0, 0), pipeline_mode=pl.Buffered(1))],
        out_specs=[pl.BlockSpec((1, tm, d), lambda bi, m: (bi, m, 0)),
                   pl.BlockSpec((1, tm, d), lambda bi, m: (bi, m, 0))],
        out_shape=[jax.ShapeDtypeStruct((b, s, d), F32),
                   jax.ShapeDtypeStruct((b, s, d), BF16)],
        scratch_shapes=[pltpu.VMEM((tm, 1), F32)],
        compiler_params=_params("parallel", "parallel"),
        name="outproj",
    )(o_rec, o_att, x, mod, norm_w, w_bf)


def _ffn_kernel(h_ref, x1_hbm, mod_ref, fw_ref, wg_ref, wu_ref, wd_ref, o_ref, x1_buf, rs_ref, x1_sem,
                *, final_norm):
    bi, m, j = pl.program_id(0), pl.program_id(1), pl.program_id(2)

    def x1_copy():
        rows = pl.ds(pl.multiple_of(m * FFN_TM, FFN_TM), FFN_TM)
        return pltpu.make_async_copy(x1_hbm.at[bi, rows, :], x1_buf, x1_sem)

    last = pl.num_programs(2) - 1
    ksubs = [slice(k0, k0 + FFN_SUB) for k0 in range(0, FFN_TF, FFN_SUB)]
    pieces = [slice(c0, c0 + FFN_DN) for c0 in range(0, D_MODEL, FFN_DN)]

    def acts():
        return [(_silu(_dot(h_ref[0], wg_ref[:, ks])) * _dot(h_ref[0], wu_ref[:, ks])).astype(BF16)
                for ks in ksubs]

    def down(a, cs):
        return functools.reduce(lambda x, y: x + y, [_dot(ai, wd_ref[ks, cs]) for ai, ks in zip(a, ksubs)])

    @pl.when(j == 0)
    def _():
        x1_copy().start()
        a = acts()
        for cs in pieces:
            o_ref[0, :, cs] = down(a, cs)

    @pl.when(jnp.logical_and(j > 0, j < last))
    def _():
        a = acts()
        for cs in pieces:
            o_ref[0, :, cs] += down(a, cs)

    @pl.when(j == last)
    def _():
        x1_copy().wait()
        a = acts()
        ssq = None
        for cs in pieces:
            x2 = x1_buf[:, cs] + mod_ref[0, 5:6, cs] * (o_ref[0, :, cs] + down(a, cs))
            x1_buf[:, cs] = x2
            if final_norm:
                sq = x2 * x2
                for k0 in range(0, FFN_DN, 128):
                    ssq = sq[:, k0:k0 + 128] if ssq is None else ssq + sq[:, k0:k0 + 128]
        if final_norm:
            rs_ref[...] = lax.rsqrt(jnp.sum(ssq, axis=-1, keepdims=True) * (1.0 / D_MODEL) + EPS)
        fw = fw_ref[...]

        def chunk(rows):
            x2 = x1_buf[rows, :]
            o_ref[0, rows, :] = x2 * rs_ref[rows, :] * fw if final_norm else x2

        _for_row_chunks(FFN_TM, chunk)


def _ffn(h2, x1, mod, final_w, wg, wu, wd, final_norm):
    b, s, d = x1.shape
    tm, tf = FFN_TM, FFN_TF
    return pl.pallas_call(
        functools.partial(_ffn_kernel, final_norm=final_norm),
        grid=(b, s // tm, D_FF // tf),
        in_specs=[pl.BlockSpec((1, tm, d), lambda bi, m, j: (bi, m, 0)),
                  pl.BlockSpec(memory_space=pl.ANY),
                  pl.BlockSpec((1, 6, d), lambda bi, m, j: (bi, 0, 0)),
                  pl.BlockSpec((1, d), lambda bi, m, j: (0, 0)),
                  pl.BlockSpec((d, tf), lambda bi, m, j: (0, j)),
                  pl.BlockSpec((d, tf), lambda bi, m, j: (0, j)),
                  pl.BlockSpec((tf, d), lambda bi, m, j: (j, 0))],
        out_specs=pl.BlockSpec((1, tm, d), lambda bi, m, j: (bi, m, 0)),
        out_shape=jax.ShapeDtypeStruct((b, s, d), F32),
        scratch_shapes=[pltpu.VMEM((tm, d), F32), pltpu.VMEM((tm, 1), F32), pltpu.SemaphoreType.DMA(())],
        compiler_params=_params("arbitrary", "arbitrary", "arbitrary"),
        name="ffn",
    )(h2, x1, mod, final_w, wg, wu, wd)


def kernel(x, c, w_ada, b_ada, norm1_w, w_in, lower_bounds, hgrn_norm_w, attn_sinks,
           rel_bias_table, w_out, norm2_w, w_gate, w_up, w_down, final_norm_w):
    b, s, d = x.shape
    assert (d, w_in.shape[-1], w_gate.shape[-1]) == (D_MODEL, IN_WIDTH, D_FF)
    assert s % max(INPROJ_TM, HGRN_TC, OUTPROJ_TM, FFN_TM, ATTN_QB * ATTN_BLOCK) == 0
    depth = w_ada.shape[0]
    c8 = jnp.pad(c, ((0, 8 - b), (0, 0)))
    for layer in range(depth):
        bias, wi16 = _bias_table(rel_bias_table, attn_sinks[layer], w_in[layer])
        mod = _ada(c8, w_ada[layer], b_ada[layer][None, :])[:b].reshape(b, 6, d)
        p32, p16 = _inproj(x, mod, norm1_w[layer][None, :], wi16)
        o_rec, wd16, wo16 = _hgrn(p32, p16, lower_bounds, hgrn_norm_w[layer][None, :], layer,
                                  w_down[layer], w_out[layer])
        o_att, wg16, wu16 = _attn(p16, bias, w_gate[layer], w_up[layer])
        x1, h2 = _outproj(o_rec, o_att, x, mod, norm2_w[layer][None, :], wo16)
        x = _ffn(h2, x1, mod, final_norm_w[None, :], wg16, wu16, wd16,
                 final_norm=(layer == depth - 1))
    return x
```

```python
import functools

import numpy as np
import jax
import jax.numpy as jnp
from jax import lax
from jax.experimental import pallas as pl
from jax.experimental.pallas import tpu as pltpu

D_MODEL = 2048
DEPTH = 1
HGRN_WIDTH = 1024
HGRN_HEAD_DIM = 128
HGRN_HEADS = HGRN_WIDTH // HGRN_HEAD_DIM
HGRN_CHUNK = 64
ATTN_WIDTH = 1024
ATTN_HEAD_DIM = 64
ATTN_Q_HEADS = ATTN_WIDTH // ATTN_HEAD_DIM
ATTN_KV_HEADS = 4
ATTN_GROUP = ATTN_Q_HEADS // ATTN_KV_HEADS
WINDOW = 128
ATTN_BLOCK = 128
ATTN_QB = 2
REL_BUCKETS = 32
REL_MAX_DIST = 128
D_FF = 5632
KV_WIDTH = ATTN_KV_HEADS * ATTN_HEAD_DIM
IN_WIDTH = 4 * HGRN_WIDTH + ATTN_WIDTH + 2 * KV_WIDTH
EPS = 1e-6
NEG_INF = -1e30
LOG2E = 1.4426950408889634

F32 = jnp.float32
BF16 = jnp.bfloat16

VMEM_LIMIT_BYTES = 56 * 1024 * 1024
MIXER_VMEM_LIMIT_BYTES = 60 * 1024 * 1024
BF16_TILE_ROWS = 16

ADA_TN = 1024
MIX_TM = 512
MIX_TN = 1024
MIX_PN = 256
MIX_STEPS = -(-IN_WIDTH // MIX_TN)
W_IN_PAD = MIX_STEPS * MIX_TN
HGRN_UNROLL = 4
HGRN_ROWS = HGRN_UNROLL * HGRN_CHUNK
OUTPROJ_TM = 512
OUTPROJ_TN = 256
FFN_TM = 1024
FFN_TF = 512
FFN_SUB = 256
FFN_DN = 512
ROW_CHUNK = 16
ROW_UNROLL = 8
NORM_TN = 256


def _params(*semantics, vmem_limit_bytes=VMEM_LIMIT_BYTES):
    return pltpu.CompilerParams(dimension_semantics=semantics, vmem_limit_bytes=vmem_limit_bytes)


def _sigmoid(v):
    return 1.0 / (1.0 + jnp.exp2(v * (-LOG2E)))


def _silu(v):
    return v * _sigmoid(v)


def _dot(a, b):
    return jnp.dot(a, b, preferred_element_type=F32)


def _dot_nt(a, b):
    return lax.dot_general(a, b, (((1,), (1,)), ((), ())), preferred_element_type=F32)


def _dot_tn(a, b):
    return lax.dot_general(a, b, (((0,), (0,)), ((), ())), preferred_element_type=F32)


def _cast_rider_specs(w, steps, steps_per_batch, index_map_args):
    rows = w.shape[0] // steps
    assert rows * steps == w.shape[0] and rows % BF16_TILE_ROWS == 0
    if index_map_args == 2:
        index_map = lambda bi, n: (bi * steps_per_batch + n, 0)
    else:
        index_map = lambda bi, n, j: (bi * steps_per_batch + jnp.minimum(n, steps_per_batch - 1), 0)
    return pl.BlockSpec((rows, w.shape[1]), index_map), jax.ShapeDtypeStruct(w.shape, BF16)


def _ada_kernel(c_ref, w_ref, b_ref, o_ref):
    c_act = _silu(c_ref[...])
    o_ref[...] = _dot(c_act.astype(BF16), w_ref[...].astype(BF16)) + b_ref[...]


def _ada(c8, w, b):
    n = w.shape[1]
    return pl.pallas_call(
        _ada_kernel,
        grid=(n // ADA_TN,),
        in_specs=[pl.BlockSpec((8, D_MODEL), lambda j: (0, 0)),
                  pl.BlockSpec((D_MODEL, ADA_TN), lambda j: (0, j)),
                  pl.BlockSpec((1, ADA_TN), lambda j: (0, j))],
        out_specs=pl.BlockSpec((8, ADA_TN), lambda j: (0, j)),
        out_shape=jax.ShapeDtypeStruct((8, n), F32),
        compiler_params=_params("arbitrary"),
        name="ada",
    )(c8, w, b)


def _for_row_chunks(n_rows, fn):
    def body(i, carry):
        fn(pl.ds(pl.multiple_of(i * ROW_CHUNK, ROW_CHUNK), ROW_CHUNK))
        return carry

    lax.fori_loop(0, n_rows // ROW_CHUNK, body, 0, unroll=ROW_UNROLL)


def _lane_folded_sumsq(ssq, v):
    sq = v * v
    for k0 in range(0, v.shape[1], 128):
        ssq = sq[:, k0:k0 + 128] if ssq is None else ssq + sq[:, k0:k0 + 128]
    return ssq


def _rsqrt_mean(ssq):
    return lax.rsqrt(jnp.sum(ssq, axis=-1, keepdims=True) * (1.0 / D_MODEL) + EPS)


def _t5_causal_buckets(dist):
    max_exact = REL_BUCKETS // 2
    d = np.maximum(dist, 0)
    log_b = max_exact + (np.log(np.maximum(d, 1) / max_exact)
                         / np.log(REL_MAX_DIST / max_exact)
                         * (REL_BUCKETS - max_exact)).astype(np.int32)
    log_b = np.minimum(log_b, REL_BUCKETS - 1)
    return np.where(d < max_exact, d, log_b).astype(np.int32)


def _bias_kernel(tab_ref, sink_ref, bucket_ref, valid_ref, w_ref, o_ref, w16_ref):
    w16_ref[:, :IN_WIDTH] = w_ref[...].astype(BF16)
    w16_ref[:, IN_WIDTH:] = jnp.zeros((w16_ref.shape[0], W_IN_PAD - IN_WIDTH), BF16)
    h = pl.program_id(0)
    bucket = bucket_ref[...]
    acc = jnp.zeros(bucket.shape, F32)
    for bk in range(REL_BUCKETS):
        acc = jnp.where(bucket == bk, tab_ref[bk, h], acc)
    sink_col = lax.broadcasted_iota(jnp.int32, bucket.shape, 1) == 0
    for variant in range(2):
        masked = jnp.where(valid_ref[variant] != 0, acc, NEG_INF)
        o_ref[variant, 0] = jnp.where(sink_col, sink_ref[h], masked)


def _bias_table(rel_table, sinks, w_in):
    l = ATTN_BLOCK
    qi = np.arange(l)[:, None]
    kj = np.arange(2 * l)[None, :]
    dist = qi + l - kj
    in_window = (dist >= 0) & (dist < WINDOW)
    valid = np.stack([in_window & (kj >= l), in_window]).astype(np.int32)
    assert not valid[:, :, 0].any()
    bucket = _t5_causal_buckets(dist)
    w_rows = w_in.shape[0] // ATTN_Q_HEADS
    assert w_rows * ATTN_Q_HEADS == w_in.shape[0] and w_rows % BF16_TILE_ROWS == 0
    return pl.pallas_call(
        _bias_kernel,
        grid=(ATTN_Q_HEADS,),
        in_specs=[pl.BlockSpec(memory_space=pltpu.SMEM),
                  pl.BlockSpec(memory_space=pltpu.SMEM),
                  pl.BlockSpec((l, 2 * l), lambda h: (0, 0)),
                  pl.BlockSpec((2, l, 2 * l), lambda h: (0, 0, 0)),
                  pl.BlockSpec((w_rows, IN_WIDTH), lambda h: (h, 0))],
        out_specs=[pl.BlockSpec((2, 1, l, 2 * l), lambda h: (0, h, 0, 0)),
                   pl.BlockSpec((w_rows, W_IN_PAD), lambda h: (h, 0))],
        out_shape=[jax.ShapeDtypeStruct((2, ATTN_Q_HEADS, l, 2 * l), F32),
                   jax.ShapeDtypeStruct((w_in.shape[0], W_IN_PAD), BF16)],
        compiler_params=_params("arbitrary"),
        name="bias",
    )(rel_table, sinks, jnp.asarray(bucket), jnp.asarray(valid), w_in)


def _split3(v):
    hi = v.astype(BF16)
    r1 = v - hi.astype(F32)
    mid = r1.astype(BF16)
    lo = (r1 - mid.astype(F32)).astype(BF16)
    return hi, mid, lo


def _hgrn_tril3():
    r, c = HGRN_ROWS, HGRN_CHUNK
    i = np.arange(r)
    tril = ((i[:, None] // c == i[None, :] // c) & (i[:, None] >= i[None, :])).astype(np.float32)
    return jnp.asarray(np.concatenate([tril, tril, tril], axis=1), dtype=BF16)


def _hgrn_rows(q, f_logit, v, g, lb_all, nw, tril3, st_ref, store, interleave):
    c, dk, nu, r = HGRN_CHUNK, HGRN_HEAD_DIM, HGRN_UNROLL, HGRN_ROWS
    causal = lax.broadcasted_iota(jnp.int32, (c, c), 0) >= lax.broadcasted_iota(jnp.int32, (c, c), 1)
    heads = [slice(h * dk, (h + 1) * dk) for h in range(HGRN_HEADS)]

    def per_chunk_rows(x, offset):
        rid = lax.broadcasted_iota(jnp.int32, x.shape, 0)
        out = jnp.broadcast_to(x[offset:offset + 1, :], x.shape)
        for u in range(1, nu):
            out = jnp.where(rid >= u * c, jnp.broadcast_to(x[u * c + offset:u * c + offset + 1, :], x.shape), out)
        return out

    interleave[0]()
    f = lb_all + (1.0 - lb_all) * _sigmoid(f_logit)
    k = 1.0 - f
    parts = jnp.concatenate(_split3(jnp.log2(f)), axis=0)
    b = _dot(tril3, parts)
    interleave[1]()
    b_mid = per_chunk_rows(b, c // 2 - 1)
    b_last = per_chunk_rows(b, c - 1)
    d_mid = b - b_mid
    qr = q * jnp.exp2(d_mid)
    kr = k * jnp.exp2(-d_mid)
    q_rel = qr.astype(BF16)
    k_rel = kr.astype(BF16)
    q_dec = (qr * jnp.exp2(b_mid)).astype(BF16)
    k_dec = (kr * jnp.exp2(b_last - b_mid)).astype(BF16)
    gate = nw * _silu(g)
    st = [st_ref[h] for h in range(HGRN_HEADS)]
    for u in range(nu):
        if u % 2 == 0:
            interleave[2 + u // 2]()
        cr = slice(u * c, (u + 1) * c)
        decay = jnp.exp2(b[(u + 1) * c - 1:(u + 1) * c, :])
        a = [jnp.where(causal, _dot_nt(q_rel[cr, hs], k_rel[cr, hs]), 0.0).astype(BF16) for hs in heads]
        o = [_dot(a[h], v[cr, hs]) + _dot_nt(q_dec[cr, hs], st[h].astype(BF16))
             for h, hs in enumerate(heads)]
        st = [decay[:, hs] * st[h] + _dot_tn(v[cr, hs], k_dec[cr, hs]) for h, hs in enumerate(heads)]
        for h, hs in enumerate(heads):
            oh = o[h] * lax.rsqrt(jnp.mean(o[h] * o[h], axis=-1, keepdims=True) + EPS)
            store(u, h, oh * gate[cr, hs])
    for h in range(HGRN_HEADS):
        st_ref[h] = st[h]


def _zero_first_row(a):
    top = a[:BF16_TILE_ROWS]
    row = lax.broadcasted_iota(jnp.int32, top.shape, 0)
    return jnp.concatenate([jnp.where(row == 0, 0.0, top).astype(a.dtype), a[BF16_TILE_ROWS:]], axis=0)


def _attn_rows(q_of, kk_all, vv_all, bias_of, store, interleave):
    l, dh = ATTN_BLOCK, ATTN_HEAD_DIM
    lane = lax.broadcasted_iota(jnp.int32, (1, 2 * dh), 1)
    low = lane < dh

    for pair in range(ATTN_KV_HEADS // 2):
        pc = slice(pair * 2 * dh, (pair + 1) * 2 * dh)
        kk = pltpu.bitcast(kk_all[:, pc], jnp.int32)
        vv = pltpu.bitcast(vv_all[:, pc], jnp.int32)
        kk_sw = pltpu.roll(kk, dh, axis=1)
        vv_sw = pltpu.roll(vv, dh, axis=1)
        for sub in range(2):
            kvh = pair * 2 + sub
            if sub == 0:
                k2_all = jnp.where(low, kk, kk_sw)
                v2_all = jnp.where(low, vv, vv_sw)
            else:
                k2_all = jnp.where(low, kk_sw, kk)
                v2_all = jnp.where(low, vv_sw, vv)
            k2_all = pltpu.bitcast(k2_all, BF16)
            v2_all = pltpu.bitcast(v2_all, BF16)
            for t in range(ATTN_QB):
                group = (pair * 2 + sub) * ATTN_QB + t
                if group % 2 == 0:
                    interleave[group // 2]()
                k2 = _zero_first_row(k2_all[t * l:(t + 2) * l])
                v2 = _zero_first_row(v2_all[t * l:(t + 2) * l])
                qs = []
                for j in range(ATTN_GROUP):
                    hq = kvh * ATTN_GROUP + j
                    qc = slice((hq // 2) * 2 * dh, (hq // 2 + 1) * 2 * dh)
                    q2 = q_of(t, qc) * (dh ** -0.5)
                    keep = low if hq % 2 == 0 else jnp.logical_not(low)
                    qs.append(jnp.where(keep, q2, 0.0).astype(BF16))
                s_all = _dot_nt(jnp.concatenate(qs, axis=0), k2)
                ps = []
                for j in range(ATTN_GROUP):
                    s = s_all[j * l:(j + 1) * l] + bias_of(t, kvh * ATTN_GROUP + j)
                    p = jnp.exp(s - jnp.max(s, axis=-1, keepdims=True))
                    ps.append(p.astype(BF16))
                v2x = jnp.concatenate([v2, jnp.ones_like(v2)], axis=1)
                o_all = _dot(jnp.concatenate(ps, axis=0), v2x)
                o_all = o_all[:, :2 * dh] / o_all[:, 2 * dh:]
                for jp in range(ATTN_GROUP // 2):
                    hq = kvh * ATTN_GROUP + 2 * jp
                    oe = o_all[(2 * jp) * l:(2 * jp + 1) * l]
                    oo = o_all[(2 * jp + 1) * l:(2 * jp + 2) * l]
                    oc = slice((hq // 2) * 2 * dh, (hq // 2 + 1) * 2 * dh)
                    store(t, oc, jnp.where(low, oe, oo))


def _mixer_kernel(x_ref, mod_ref, nw1_ref, w_ref, lb_ref, nwh_ref, tril_ref, bias_ref, wa_ref, wb_ref,
                  orec_ref, oatt_ref, wa16_ref, wb16_ref,
                  h_ref, hn_ref, phq_ref, phf_ref, phg_ref, phi_ref, pq_ref, pkv_ref, st_ref, carry_ref,
                  *, layer, n_tiles):
    m, j = pl.program_id(1), pl.program_id(2)
    tq = ATTN_QB * ATTN_BLOCK
    assert MIX_TM == 2 * HGRN_ROWS == 2 * tq and MIX_STEPS == 6 and MIX_TN == HGRN_WIDTH == ATTN_WIDTH
    real_tile = m < n_tiles

    def norm_rows_into(dst_ref, interleave):
        n_pieces = D_MODEL // NORM_TN
        ssq = None
        for i, c0 in enumerate(range(0, D_MODEL, NORM_TN)):
            if i % (n_pieces // 2) == 0:
                interleave[i // (n_pieces // 2)]()
            ssq = _lane_folded_sumsq(ssq, x_ref[0, :, c0:c0 + NORM_TN])
        rs = _rsqrt_mean(ssq)
        for i, c0 in enumerate(range(0, D_MODEL, NORM_TN)):
            if i % (n_pieces // 2) == 0:
                interleave[2 + i // (n_pieces // 2)]()
            cs = slice(c0, c0 + NORM_TN)
            wmod = nw1_ref[:, cs] * (1.0 + mod_ref[0, 1:2, cs])
            dst_ref[:, cs] = (x_ref[0, :, cs] * rs * wmod + mod_ref[0, 0:1, cs]).astype(BF16)

    def project_pieces(dst_ref, width=MIX_TN):
        def piece(c0):
            def run():
                if c0 < width:
                    cs = slice(c0, c0 + MIX_PN)
                    dst_ref[:, cs] = _dot(h_ref[...], w_ref[:, cs]).astype(dst_ref.dtype)
            return run

        return [piece(c0) for c0 in range(0, MIX_TN, MIX_PN)]

    def run_all(thunks):
        for t in thunks:
            t()

    def lower_bound():
        lb_rows = [lb_ref[r:r + 1, :] for r in range(lb_ref.shape[0])]
        lb_max = functools.reduce(jnp.maximum, lb_rows)
        lb_exp = [jnp.exp(r - lb_max) for r in lb_rows]
        return sum(lb_exp[:layer + 1]) / sum(lb_exp)

    def hgrn_half(half, interleave):
        rows = slice(half * HGRN_ROWS, (half + 1) * HGRN_ROWS)

        def store(u, h, tile):
            r0 = half * HGRN_ROWS + u * HGRN_CHUNK
            orec_ref[0, r0:r0 + HGRN_CHUNK, h * HGRN_HEAD_DIM:(h + 1) * HGRN_HEAD_DIM] = (
                tile.astype(orec_ref.dtype))

        _hgrn_rows(phq_ref[rows, :], phf_ref[rows, :], phi_ref[rows, :], phg_ref[rows, :],
                   lower_bound(), nwh_ref[...], tril_ref[...], st_ref, store, interleave)

    def attn_half(half, interleave):
        l = ATTN_BLOCK
        r0 = half * tq
        if half == 0:
            kv_all = jnp.concatenate([carry_ref[...], pkv_ref[0:tq, :]], axis=0)
        else:
            kv_all = pkv_ref[r0 - l:r0 + tq, :]
        first = jnp.clip(m - 1, 0, 1)

        def bias_of(t, hq):
            return bias_ref[first, hq] if (half == 0 and t == 0) else bias_ref[1, hq]

        def store(t, cols, tile):
            oatt_ref[0, r0 + t * l:r0 + (t + 1) * l, cols] = tile.astype(oatt_ref.dtype)

        _attn_rows(lambda t, cols: pq_ref[r0 + t * l:r0 + (t + 1) * l, cols],
                   kv_all[:, :KV_WIDTH], kv_all[:, KV_WIDTH:], bias_of, store, interleave)

    @pl.when(jnp.logical_and(m == 0, j == 0))
    def _():
        norm_rows_into(h_ref, [lambda: None] * 4)
        pq_ref[...] = jnp.zeros_like(pq_ref)
        pkv_ref[...] = jnp.zeros_like(pkv_ref)
        carry_ref[...] = jnp.zeros_like(carry_ref)
        st_ref[...] = jnp.zeros_like(st_ref)

    @pl.when(j == 0)
    def _():
        attn_half(0, project_pieces(phq_ref))

    @pl.when(j == 1)
    def _():
        attn_half(1, project_pieces(phf_ref))
        carry_ref[...] = pkv_ref[MIX_TM - ATTN_BLOCK:MIX_TM, :]

    @pl.when(jnp.logical_and(j == 2, real_tile))
    def _():
        norm_rows_into(hn_ref, project_pieces(phi_ref))

    @pl.when(jnp.logical_and(j == 3, real_tile))
    def _():
        run_all(project_pieces(phg_ref))
        wa16_ref[...] = wa_ref[...].astype(BF16)
        wb16_ref[...] = wb_ref[...].astype(BF16)

    @pl.when(jnp.logical_and(j == 4, real_tile))
    def _():
        hgrn_half(0, project_pieces(pq_ref))

    @pl.when(jnp.logical_and(j == 5, real_tile))
    def _():
        hgrn_half(1, project_pieces(pkv_ref, 2 * KV_WIDTH))
        h_ref[...] = hn_ref[...]


def _mixer(x, mod, norm1_w, w16, lower_bounds, hgrn_norm_w, bias, layer, wa, wb):
    b, s, d = x.shape
    nt = s // MIX_TM
    wa_spec, wa_shape = _cast_rider_specs(wa, b * nt, nt, 3)
    wb_spec, wb_shape = _cast_rider_specs(wb, b * nt, nt, 3)
    tril3 = _hgrn_tril3()
    whole = lambda a: pl.BlockSpec(a.shape, lambda bi, m, j: (0,) * a.ndim, pipeline_mode=pl.Buffered(1))
    tile_spec = lambda index_of_m: pl.BlockSpec((1, MIX_TM, HGRN_WIDTH),
                                                lambda bi, m, j: (bi, index_of_m(m), 0))
    return pl.pallas_call(
        functools.partial(_mixer_kernel, layer=layer, n_tiles=nt),
        grid=(b, nt + 1, MIX_STEPS),
        in_specs=[
            pl.BlockSpec((1, MIX_TM, d),
                         lambda bi, m, j: (bi, jnp.minimum(m + jnp.where(j >= 1, 1, 0), nt - 1), 0)),
            pl.BlockSpec((1, 6, d), lambda bi, m, j: (bi, 0, 0)),
            pl.BlockSpec((1, d), lambda bi, m, j: (0, 0)),
            pl.BlockSpec((d, MIX_TN), lambda bi, m, j: (0, j)),
            pl.BlockSpec(lower_bounds.shape, lambda bi, m, j: (0, 0)),
            pl.BlockSpec((1, HGRN_WIDTH), lambda bi, m, j: (0, 0)),
            whole(tril3), whole(bias), wa_spec, wb_spec],
        out_specs=[tile_spec(lambda m: jnp.minimum(m, nt - 1)),
                   tile_spec(lambda m: jnp.maximum(m - 1, 0)),
                   wa_spec, wb_spec],
        out_shape=[jax.ShapeDtypeStruct((b, s, HGRN_WIDTH), BF16),
                   jax.ShapeDtypeStruct((b, s, ATTN_WIDTH), BF16), wa_shape, wb_shape],
        scratch_shapes=[pltpu.VMEM((MIX_TM, d), BF16),
                        pltpu.VMEM((MIX_TM, d), BF16),
                        pltpu.VMEM((MIX_TM, HGRN_WIDTH), F32),
                        pltpu.VMEM((MIX_TM, HGRN_WIDTH), F32),
                        pltpu.VMEM((MIX_TM, HGRN_WIDTH), F32),
                        pltpu.VMEM((MIX_TM, HGRN_WIDTH), BF16),
                        pltpu.VMEM((MIX_TM, ATTN_WIDTH), BF16),
                        pltpu.VMEM((MIX_TM, 2 * KV_WIDTH), BF16),
                        pltpu.VMEM((HGRN_HEADS, HGRN_HEAD_DIM, HGRN_HEAD_DIM), F32),
                        pltpu.VMEM((ATTN_BLOCK, 2 * KV_WIDTH), BF16)],
        compiler_params=_params("arbitrary", "arbitrary", "arbitrary"),
        name="mixer",
    )(x, mod, norm1_w, w16, lower_bounds, hgrn_norm_w, tril3, bias, wa, wb)


def _outproj_kernel(orec_ref, oatt_ref, x_ref, mod_ref, nw_ref, w_ref, wa_ref, wb_ref,
                    x1_ref, h2_ref, wa16_ref, wb16_ref, rs_ref):
    wa16_ref[...] = wa_ref[...].astype(BF16)
    wb16_ref[...] = wb_ref[...].astype(BF16)
    ssq = None
    for c0 in range(0, D_MODEL, OUTPROJ_TN):
        cs = slice(c0, c0 + OUTPROJ_TN)
        y = _dot(orec_ref[0], w_ref[0:HGRN_WIDTH, cs]) + _dot(oatt_ref[0], w_ref[HGRN_WIDTH:, cs])
        x1 = x_ref[0, :, cs] + mod_ref[0, 2:3, cs] * y
        x1_ref[0, :, cs] = x1
        ssq = _lane_folded_sumsq(ssq, x1)
    rs_ref[...] = _rsqrt_mean(ssq)

    shift = mod_ref[0, 3:4, :]
    wmod = nw_ref[...] * (1.0 + mod_ref[0, 4:5, :])

    def chunk(rows):
        h2_ref[0, rows, :] = (x1_ref[0, rows, :] * rs_ref[rows, :] * wmod + shift).astype(BF16)

    _for_row_chunks(OUTPROJ_TM, chunk)


def _outproj(o_rec, o_att, x, mod, norm_w, w_bf, wa, wb):
    b, s, d = x.shape
    tm = OUTPROJ_TM
    nt = s // tm
    wa_spec, wa_shape = _cast_rider_specs(wa, b * nt, nt, 2)
    wb_spec, wb_shape = _cast_rider_specs(wb, b * nt, nt, 2)
    return pl.pallas_call(
        _outproj_kernel,
        grid=(b, nt),
        in_specs=[pl.BlockSpec((1, tm, HGRN_WIDTH), lambda bi, m: (bi, m, 0)),
                  pl.BlockSpec((1, tm, ATTN_WIDTH), lambda bi, m: (bi, m, 0)),
                  pl.BlockSpec((1, tm, d), lambda bi, m: (bi, m, 0)),
                  pl.BlockSpec((1, 6, d), lambda bi, m: (bi, 0, 0)),
                  pl.BlockSpec((1, d), lambda bi, m: (0, 0)),
                  pl.BlockSpec(w_bf.shape, lambda bi, m: (0, 0), pipeline_mode=pl.Buffered(1)),
                  wa_spec, wb_spec],
        out_specs=[pl.BlockSpec((1, tm, d), lambda bi, m: (bi, m, 0)),
                   pl.BlockSpec((1, tm, d), lambda bi, m: (bi, m, 0)), wa_spec, wb_spec],
        out_shape=[jax.ShapeDtypeStruct((b, s, d), F32),
                   jax.ShapeDtypeStruct((b, s, d), BF16), wa_shape, wb_shape],
        scratch_shapes=[pltpu.VMEM((tm, 1), F32)],
        compiler_params=_params("arbitrary", "arbitrary"),
        name="outproj",
    )(o_rec, o_att, x, mod, norm_w, w_bf, wa, wb)


def _ffn_kernel(h_ref, x1_hbm, mod_ref, fw_ref, wg_ref, wu_ref, wd_ref, o_ref, x1_buf, rs_ref, x1_sem,
                *, final_norm):
    bi, m, j = pl.program_id(0), pl.program_id(1), pl.program_id(2)

    def x1_copy():
        rows = pl.ds(pl.multiple_of(m * FFN_TM, FFN_TM), FFN_TM)
        return pltpu.make_async_copy(x1_hbm.at[bi, rows, :], x1_buf, x1_sem)

    last = pl.num_programs(2) - 1
    ksubs = [slice(k0, k0 + FFN_SUB) for k0 in range(0, FFN_TF, FFN_SUB)]
    pieces = [slice(c0, c0 + FFN_DN) for c0 in range(0, D_MODEL, FFN_DN)]

    def acts():
        return [(_silu(_dot(h_ref[0], wg_ref[:, ks])) * _dot(h_ref[0], wu_ref[:, ks])).astype(BF16)
                for ks in ksubs]

    def down(a, cs):
        return functools.reduce(lambda x, y: x + y, [_dot(ai, wd_ref[ks, cs]) for ai, ks in zip(a, ksubs)])

    @pl.when(j == 0)
    def _():
        x1_copy().start()
        a = acts()
        for cs in pieces:
            o_ref[0, :, cs] = down(a, cs)

    @pl.when(jnp.logical_and(j > 0, j < last))
    def _():
        a = acts()
        for cs in pieces:
            o_ref[0, :, cs] += down(a, cs)

    @pl.when(j == last)
    def _():
        x1_copy().wait()
        a = acts()
        ssq = None
        for cs in pieces:
            x2 = x1_buf[:, cs] + mod_ref[0, 5:6, cs] * (o_ref[0, :, cs] + down(a, cs))
            x1_buf[:, cs] = x2
            if final_norm:
                ssq = _lane_folded_sumsq(ssq, x2)
        if final_norm:
            rs_ref[...] = _rsqrt_mean(ssq)
        fw = fw_ref[...]

        def chunk(rows):
            x2 = x1_buf[rows, :]
            o_ref[0, rows, :] = x2 * rs_ref[rows, :] * fw if final_norm else x2

        _for_row_chunks(FFN_TM, chunk)


def _ffn(h2, x1, mod, final_w, wg, wu, wd, final_norm):
    b, s, d = x1.shape
    tm, tf = FFN_TM, FFN_TF
    return pl.pallas_call(
        functools.partial(_ffn_kernel, final_norm=final_norm),
        grid=(b, s // tm, D_FF // tf),
        in_specs=[pl.BlockSpec((1, tm, d), lambda bi, m, j: (bi, m, 0)),
                  pl.BlockSpec(memory_space=pl.ANY),
                  pl.BlockSpec((1, 6, d), lambda bi, m, j: (bi, 0, 0)),
                  pl.BlockSpec((1, d), lambda bi, m, j: (0, 0)),
                  pl.BlockSpec((d, tf), lambda bi, m, j: (0, j)),
                  pl.BlockSpec((d, tf), lambda bi, m, j: (0, j)),
                  pl.BlockSpec((tf, d), lambda bi, m, j: (j, 0))],
        out_specs=pl.BlockSpec((1, tm, d), lambda bi, m, j: (bi, m, 0)),
        out_shape=jax.ShapeDtypeStruct((b, s, d), F32),
        scratch_shapes=[pltpu.VMEM((tm, d), F32), pltpu.VMEM((tm, 1), F32), pltpu.SemaphoreType.DMA(())],
        compiler_params=_params("arbitrary", "arbitrary", "arbitrary"),
        name="ffn",
    )(h2, x1, mod, final_w, wg, wu, wd)


def kernel(x, c, w_ada, b_ada, norm1_w, w_in, lower_bounds, hgrn_norm_w, attn_sinks,
           rel_bias_table, w_out, norm2_w, w_gate, w_up, w_down, final_norm_w):
    b, s, d = x.shape
    assert (d, w_in.shape[-1], w_gate.shape[-1]) == (D_MODEL, IN_WIDTH, D_FF)
    assert s % max(MIX_TM, OUTPROJ_TM, FFN_TM) == 0
    depth = w_ada.shape[0]
    c8 = jnp.pad(c, ((0, 8 - b), (0, 0)))
    for layer in range(depth):
        bias, wi16 = _bias_table(rel_bias_table, attn_sinks[layer], w_in[layer])
        mod = _ada(c8, w_ada[layer], b_ada[layer][None, :])[:b].reshape(b, 6, d)
        o_rec, o_att, wd16, wo16 = _mixer(x, mod, norm1_w[layer][None, :], wi16, lower_bounds,
                                          hgrn_norm_w[layer][None, :], bias, layer,
                                          w_down[layer], w_out[layer])
        x1, h2, wg16, wu16 = _outproj(o_rec, o_att, x, mod, norm2_w[layer][None, :], wo16,
                                      w_gate[layer], w_up[layer])
        x = _ffn(h2, x1, mod, final_norm_w[None, :], wg16, wu16, wd16,
                 final_norm=(layer == depth - 1))
    return x
```

```python
import functools

import numpy as np
import jax
import jax.numpy as jnp
from jax import lax
from jax.experimental import pallas as pl
from jax.experimental.pallas import tpu as pltpu

D_MODEL = 2048
DEPTH = 1
HGRN_WIDTH = 1024
HGRN_HEAD_DIM = 128
HGRN_HEADS = HGRN_WIDTH // HGRN_HEAD_DIM
HGRN_CHUNK = 64
ATTN_WIDTH = 1024
ATTN_HEAD_DIM = 64
ATTN_Q_HEADS = ATTN_WIDTH // ATTN_HEAD_DIM
ATTN_KV_HEADS = 4
ATTN_GROUP = ATTN_Q_HEADS // ATTN_KV_HEADS
WINDOW = 128
ATTN_BLOCK = 128
ATTN_QB = 2
REL_BUCKETS = 32
REL_MAX_DIST = 128
D_FF = 5632
KV_WIDTH = ATTN_KV_HEADS * ATTN_HEAD_DIM
IN_WIDTH = 4 * HGRN_WIDTH + ATTN_WIDTH + 2 * KV_WIDTH
P16_BLOCKS = ("v", "aq", "q_rel", "k_rel", "q_dec", "k_dec")
P16_KV_COL = len(P16_BLOCKS) * HGRN_WIDTH
P16_WIDTH = P16_KV_COL + 2 * KV_WIDTH
EPS = 1e-6
NEG_INF = -1e30
LOG2E = 1.4426950408889634

F32 = jnp.float32
BF16 = jnp.bfloat16

VMEM_LIMIT_BYTES = 56 * 1024 * 1024
BF16_TILE_ROWS = 16

ADA_TN = 1024
INPROJ_TM = 512
INPROJ_TN = 512
HGRN_TC = 512
HGRN_UNROLL = 4
HGRN_ROWS = HGRN_UNROLL * HGRN_CHUNK
OUTPROJ_TM = 512
OUTPROJ_TN = 256
FFN_TM = 1024
FFN_TF = 512
FFN_SUB = 256
FFN_DN = 512
ROW_CHUNK = 16
ROW_UNROLL = 8


def _params(*semantics):
    return pltpu.CompilerParams(dimension_semantics=semantics,
                                vmem_limit_bytes=VMEM_LIMIT_BYTES)


def _sigmoid(v):
    return 1.0 / (1.0 + jnp.exp2(v * (-LOG2E)))


def _silu(v):
    return v * _sigmoid(v)


def _dot(a, b):
    return jnp.dot(a, b, preferred_element_type=F32)


def _dot_nt(a, b):
    return lax.dot_general(a, b, (((1,), (1,)), ((), ())), preferred_element_type=F32)


def _dot_tn(a, b):
    return lax.dot_general(a, b, (((0,), (0,)), ((), ())), preferred_element_type=F32)


def _ada_kernel(c_ref, w_ref, b_ref, o_ref):
    c_act = _silu(c_ref[...])
    o_ref[...] = _dot(c_act.astype(BF16), w_ref[...].astype(BF16)) + b_ref[...]


def _ada(c8, w, b):
    n = w.shape[1]
    return pl.pallas_call(
        _ada_kernel,
        grid=(n // ADA_TN,),
        in_specs=[pl.BlockSpec((8, D_MODEL), lambda j: (0, 0)),
                  pl.BlockSpec((D_MODEL, ADA_TN), lambda j: (0, j)),
                  pl.BlockSpec((1, ADA_TN), lambda j: (0, j))],
        out_specs=pl.BlockSpec((8, ADA_TN), lambda j: (0, j)),
        out_shape=jax.ShapeDtypeStruct((8, n), F32),
        compiler_params=_params("arbitrary"),
        name="ada",
    )(c8, w, b)


def _modulated_norm(x, wmod, shift):
    return x * lax.rsqrt(jnp.mean(x * x, axis=-1, keepdims=True) + EPS) * wmod + shift


def _for_row_chunks(n_rows, fn):
    def body(i, carry):
        fn(pl.ds(pl.multiple_of(i * ROW_CHUNK, ROW_CHUNK), ROW_CHUNK))
        return carry

    lax.fori_loop(0, n_rows // ROW_CHUNK, body, 0, unroll=ROW_UNROLL)


def _split3(v):
    hi = v.astype(BF16)
    r1 = v - hi.astype(F32)
    mid = r1.astype(BF16)
    lo = (r1 - mid.astype(F32)).astype(BF16)
    return hi, mid, lo


def _hgrn_tril3():
    tril = np.tril(np.ones((HGRN_CHUNK, HGRN_CHUNK), np.float32))
    return jnp.asarray(np.concatenate([tril, tril, tril], axis=1), dtype=BF16)


def _hgrn_prepare(q, f_logit, g, lb, nw, tril3):
    c, nu = HGRN_CHUNK, HGRN_UNROLL

    def per_chunk_rows(x, offset):
        rid = lax.broadcasted_iota(jnp.int32, x.shape, 0)
        out = jnp.broadcast_to(x[offset:offset + 1, :], x.shape)
        for u in range(1, nu):
            out = jnp.where(rid >= u * c, jnp.broadcast_to(x[u * c + offset:u * c + offset + 1, :], x.shape), out)
        return out

    f = lb + (1.0 - lb) * _sigmoid(f_logit)
    k = 1.0 - f
    parts = _split3(jnp.log2(f))
    b = jnp.concatenate(
        [_dot(tril3, jnp.concatenate([p[u * c:(u + 1) * c] for p in parts], axis=0)) for u in range(nu)],
        axis=0)
    b_mid = per_chunk_rows(b, c // 2 - 1)
    b_last = per_chunk_rows(b, c - 1)
    d_mid = b - b_mid
    qr = q * jnp.exp2(d_mid)
    kr = k * jnp.exp2(-d_mid)
    q_dec = (qr * jnp.exp2(b_mid)).astype(BF16)
    k_dec = (kr * jnp.exp2(b_last - b_mid)).astype(BF16)
    decays = [jnp.exp2(b[(u + 1) * c - 1:(u + 1) * c, :]) for u in range(nu)]
    return qr.astype(BF16), kr.astype(BF16), q_dec, k_dec, nw * _silu(g), decays


def _inproj_kernel(x_ref, mod_ref, nw_ref, w_ref, lb_ref, nwh_ref, tril_ref, gate_ref, o16_ref, dec_ref,
                   h_ref, *, layer):
    shift = mod_ref[0, 0:1, :]
    wmod = nw_ref[...] * (1.0 + mod_ref[0, 1:2, :])

    def chunk(rows):
        h_ref[rows, :] = _modulated_norm(x_ref[0, rows, :], wmod, shift).astype(BF16)

    _for_row_chunks(INPROJ_TM, chunk)

    lb_rows = [lb_ref[r:r + 1, :] for r in range(lb_ref.shape[0])]
    lb_max = functools.reduce(jnp.maximum, lb_rows)
    lb_exp = [jnp.exp(r - lb_max) for r in lb_rows]
    lb_all = sum(lb_exp[:layer + 1]) / sum(lb_exp)

    tn, hw = INPROJ_TN, HGRN_WIDTH
    p16_col = {name: i * hw for i, name in enumerate(P16_BLOCKS)}

    def project(w_col):
        return _dot(h_ref[...], w_ref[:, w_col:w_col + tn])

    def to_p16(name, c0, value):
        o16_ref[0, :, p16_col[name] + c0:p16_col[name] + c0 + tn] = value.astype(BF16)

    for c0 in range(0, hw, tn):
        cs = slice(c0, c0 + tn)
        hq, hf, hg = project(c0), project(hw + c0), project(3 * hw + c0)
        followers = [lambda: to_p16("v", c0, project(2 * hw + c0)),
                     lambda: to_p16("aq", c0, project(4 * hw + c0))]
        for half in range(INPROJ_TM // HGRN_ROWS):
            rows = slice(half * HGRN_ROWS, (half + 1) * HGRN_ROWS)
            q_rel, k_rel, q_dec, k_dec, gate, decays = _hgrn_prepare(
                hq[rows], hf[rows], hg[rows], lb_all[:, cs], nwh_ref[:, cs], tril_ref[...])
            for name, value in (("q_rel", q_rel), ("k_rel", k_rel), ("q_dec", q_dec), ("k_dec", k_dec)):
                o16_ref[0, rows, p16_col[name] + c0:p16_col[name] + c0 + tn] = value
            gate_ref[0, rows, cs] = gate
            for u, decay in enumerate(decays):
                dec_ref[0, half * HGRN_UNROLL + u:half * HGRN_UNROLL + u + 1, cs] = decay
            followers[half]()
    o16_ref[0, :, P16_KV_COL:] = project(4 * hw + ATTN_WIDTH).astype(BF16)


def _inproj(x, mod, norm_w, w_bf, lower_bounds, hgrn_norm_w, layer):
    b, s, d = x.shape
    tm = INPROJ_TM
    assert tm == 2 * HGRN_ROWS and 2 * KV_WIDTH == INPROJ_TN
    chunks = tm // HGRN_CHUNK
    tril3 = _hgrn_tril3()
    return pl.pallas_call(
        functools.partial(_inproj_kernel, layer=layer),
        grid=(b, s // tm),
        in_specs=[pl.BlockSpec((1, tm, d), lambda bi, m: (bi, m, 0)),
                  pl.BlockSpec((1, 6, d), lambda bi, m: (bi, 0, 0)),
                  pl.BlockSpec((1, d), lambda bi, m: (0, 0)),
                  pl.BlockSpec(w_bf.shape, lambda bi, m: (0, 0), pipeline_mode=pl.Buffered(1)),
                  pl.BlockSpec(lower_bounds.shape, lambda bi, m: (0, 0)),
                  pl.BlockSpec((1, HGRN_WIDTH), lambda bi, m: (0, 0)),
                  pl.BlockSpec(tril3.shape, lambda bi, m: (0, 0))],
        out_specs=[pl.BlockSpec((1, tm, HGRN_WIDTH), lambda bi, m: (bi, m, 0)),
                   pl.BlockSpec((1, tm, P16_WIDTH), lambda bi, m: (bi, m, 0)),
                   pl.BlockSpec((1, chunks, HGRN_WIDTH), lambda bi, m: (bi, m, 0))],
        out_shape=[jax.ShapeDtypeStruct((b, s, HGRN_WIDTH), F32),
                   jax.ShapeDtypeStruct((b, s, P16_WIDTH), BF16),
                   jax.ShapeDtypeStruct((b, s // HGRN_CHUNK, HGRN_WIDTH), F32)],
        scratch_shapes=[pltpu.VMEM((tm, d), BF16)],
        compiler_params=_params("parallel", "parallel"),
        name="inproj",
    )(x, mod, norm_w, w_bf, lower_bounds, hgrn_norm_w, tril3)


def _hgrn_kernel(v_ref, qrel_ref, krel_ref, qdec_ref, kdec_ref, gate_ref, dec_ref, wa_ref, wb_ref,
                 o_ref, wa16_ref, wb16_ref, st_ref):
    c, dk, nu = HGRN_CHUNK, HGRN_HEAD_DIM, HGRN_UNROLL
    wa16_ref[...] = wa_ref[...].astype(BF16)
    wb16_ref[...] = wb_ref[...].astype(BF16)

    @pl.when(pl.program_id(1) == 0)
    def _():
        st_ref[...] = jnp.zeros_like(st_ref)

    causal = (lax.broadcasted_iota(jnp.int32, (c, c), 0) >= lax.broadcasted_iota(jnp.int32, (c, c), 1))
    heads = [slice(h * dk, (h + 1) * dk) for h in range(HGRN_HEADS)]

    def body(i, carry):
        st = [st_ref[h] for h in range(HGRN_HEADS)]
        for u in range(nu):
            rows = pl.ds(pl.multiple_of((i * nu + u) * c, c), c)
            decay = dec_ref[0, pl.ds(i * nu + u, 1), :]
            a = [jnp.where(causal, _dot_nt(qrel_ref[0, rows, hs], krel_ref[0, rows, hs]), 0.0).astype(BF16)
                 for hs in heads]
            o = [_dot(a[h], v_ref[0, rows, hs]) + _dot_nt(qdec_ref[0, rows, hs], st[h].astype(BF16))
                 for h, hs in enumerate(heads)]
            st = [decay[:, hs] * st[h] + _dot_tn(v_ref[0, rows, hs], kdec_ref[0, rows, hs])
                  for h, hs in enumerate(heads)]
            for h, hs in enumerate(heads):
                oh = o[h] * lax.rsqrt(jnp.mean(o[h] * o[h], axis=-1, keepdims=True) + EPS)
                o_ref[0, rows, hs] = (oh * gate_ref[0, rows, hs]).astype(o_ref.dtype)
        for h in range(HGRN_HEADS):
            st_ref[h] = st[h]
        return carry

    lax.fori_loop(0, HGRN_TC // (nu * c), body, 0)


def _hgrn(gate, p16, dec, wa, wb):
    b, s, _ = gate.shape
    nt = s // HGRN_TC
    blk = lambda name: pl.BlockSpec((1, HGRN_TC, HGRN_WIDTH),
                                    lambda bi, t, j=P16_BLOCKS.index(name): (bi, t, j))
    wa_spec, wa_shape = _cast_rider_specs(wa, b * nt, nt)
    wb_spec, wb_shape = _cast_rider_specs(wb, b * nt, nt)
    return pl.pallas_call(
        _hgrn_kernel,
        grid=(b, nt),
        in_specs=[blk("v"), blk("q_rel"), blk("k_rel"), blk("q_dec"), blk("k_dec"),
                  pl.BlockSpec((1, HGRN_TC, HGRN_WIDTH), lambda bi, t: (bi, t, 0)),
                  pl.BlockSpec((1, HGRN_TC // HGRN_CHUNK, HGRN_WIDTH), lambda bi, t: (bi, t, 0)),
                  wa_spec, wb_spec],
        out_specs=[pl.BlockSpec((1, HGRN_TC, HGRN_WIDTH), lambda bi, t: (bi, t, 0)), wa_spec, wb_spec],
        out_shape=[jax.ShapeDtypeStruct((b, s, HGRN_WIDTH), BF16), wa_shape, wb_shape],
        scratch_shapes=[pltpu.VMEM((HGRN_HEADS, HGRN_HEAD_DIM, HGRN_HEAD_DIM), F32)],
        compiler_params=_params("arbitrary", "arbitrary"),
        name="hgrn",
    )(p16, p16, p16, p16, p16, gate, dec, wa, wb)


def _t5_causal_buckets(dist):
    max_exact = REL_BUCKETS // 2
    d = np.maximum(dist, 0)
    log_b = max_exact + (np.log(np.maximum(d, 1) / max_exact)
                         / np.log(REL_MAX_DIST / max_exact)
                         * (REL_BUCKETS - max_exact)).astype(np.int32)
    log_b = np.minimum(log_b, REL_BUCKETS - 1)
    return np.where(d < max_exact, d, log_b).astype(np.int32)


def _bias_kernel(tab_ref, sink_ref, bucket_ref, valid_ref, w_ref, o_ref, w16_ref):
    w16_ref[...] = w_ref[...].astype(BF16)
    h = pl.program_id(0)
    bucket = bucket_ref[...]
    acc = jnp.zeros(bucket.shape, F32)
    for bk in range(REL_BUCKETS):
        acc = jnp.where(bucket == bk, tab_ref[bk, h], acc)
    sink_col = lax.broadcasted_iota(jnp.int32, bucket.shape, 1) == 0
    for variant in range(2):
        masked = jnp.where(valid_ref[variant] != 0, acc, NEG_INF)
        o_ref[variant, 0] = jnp.where(sink_col, sink_ref[h], masked)


def _bias_table(rel_table, sinks, w):
    l = ATTN_BLOCK
    qi = np.arange(l)[:, None]
    kj = np.arange(2 * l)[None, :]
    dist = qi + l - kj
    in_window = (dist >= 0) & (dist < WINDOW)
    valid = np.stack([in_window & (kj >= l), in_window]).astype(np.int32)
    assert not valid[:, :, 0].any()
    bucket = _t5_causal_buckets(dist)
    w_rows = w.shape[0] // ATTN_Q_HEADS
    assert w_rows * ATTN_Q_HEADS == w.shape[0] and w_rows % BF16_TILE_ROWS == 0
    w_spec = pl.BlockSpec((w_rows, w.shape[1]), lambda h: (h, 0))
    return pl.pallas_call(
        _bias_kernel,
        grid=(ATTN_Q_HEADS,),
        in_specs=[pl.BlockSpec(memory_space=pltpu.SMEM),
                  pl.BlockSpec(memory_space=pltpu.SMEM),
                  pl.BlockSpec((l, 2 * l), lambda h: (0, 0)),
                  pl.BlockSpec((2, l, 2 * l), lambda h: (0, 0, 0)),
                  w_spec],
        out_specs=[pl.BlockSpec((2, 1, l, 2 * l), lambda h: (0, h, 0, 0)), w_spec],
        out_shape=[jax.ShapeDtypeStruct((2, ATTN_Q_HEADS, l, 2 * l), F32),
                   jax.ShapeDtypeStruct(w.shape, BF16)],
        compiler_params=_params("arbitrary"),
        name="bias",
    )(rel_table, sinks, jnp.asarray(bucket), jnp.asarray(valid), w)


def _zero_first_row(a):
    top = a[:BF16_TILE_ROWS]
    row = lax.broadcasted_iota(jnp.int32, top.shape, 0)
    return jnp.concatenate([jnp.where(row == 0, 0.0, top).astype(a.dtype), a[BF16_TILE_ROWS:]], axis=0)


def _attn_kernel(q_ref, kp_ref, kc_ref, vp_ref, vc_ref, bias_ref, wa_ref, wb_ref,
                 o_ref, wa16_ref, wb16_ref):
    l = ATTN_BLOCK
    dh = ATTN_HEAD_DIM
    wa16_ref[...] = wa_ref[...].astype(BF16)
    wb16_ref[...] = wb_ref[...].astype(BF16)
    lane = lax.broadcasted_iota(jnp.int32, (1, 2 * dh), 1)
    low = lane < dh
    first = jnp.minimum(pl.program_id(1), 1)

    for pair in range(ATTN_KV_HEADS // 2):
        pc = slice(pair * 2 * dh, (pair + 1) * 2 * dh)
        kk = pltpu.bitcast(jnp.concatenate([kp_ref[0, :, pc], kc_ref[0, :, pc]], axis=0), jnp.int32)
        vv = pltpu.bitcast(jnp.concatenate([vp_ref[0, :, pc], vc_ref[0, :, pc]], axis=0), jnp.int32)
        kk_sw = pltpu.roll(kk, dh, axis=1)
        vv_sw = pltpu.roll(vv, dh, axis=1)
        for sub in range(2):
            kvh = pair * 2 + sub
            if sub == 0:
                k2_all = jnp.where(low, kk, kk_sw)
                v2_all = jnp.where(low, vv, vv_sw)
            else:
                k2_all = jnp.where(low, kk_sw, kk)
                v2_all = jnp.where(low, vv_sw, vv)
            k2_all = pltpu.bitcast(k2_all, BF16)
            v2_all = pltpu.bitcast(v2_all, BF16)
            for t in range(ATTN_QB):
                qrows = slice(t * l, (t + 1) * l)
                k2 = _zero_first_row(k2_all[t * l:(t + 2) * l])
                v2 = _zero_first_row(v2_all[t * l:(t + 2) * l])
                qs = []
                for j in range(ATTN_GROUP):
                    hq = kvh * ATTN_GROUP + j
                    qc = slice((hq // 2) * 2 * dh, (hq // 2 + 1) * 2 * dh)
                    q2 = q_ref[0, qrows, qc] * (dh ** -0.5)
                    keep = low if hq % 2 == 0 else jnp.logical_not(low)
                    qs.append(jnp.where(keep, q2, 0.0).astype(BF16))
                s_all = _dot_nt(jnp.concatenate(qs, axis=0), k2)
                ps = []
                for j in range(ATTN_GROUP):
                    hq = kvh * ATTN_GROUP + j
                    bias = bias_ref[first, hq] if t == 0 else bias_ref[1, hq]
                    s = s_all[j * l:(j + 1) * l] + bias
                    p = jnp.exp(s - jnp.max(s, axis=-1, keepdims=True))
                    ps.append(p.astype(BF16))
                v2x = jnp.concatenate([v2, jnp.ones_like(v2)], axis=1)
                o_all = _dot(jnp.concatenate(ps, axis=0), v2x)
                o_all = o_all[:, :2 * dh] / o_all[:, 2 * dh:]
                for jp in range(ATTN_GROUP // 2):
                    hq = kvh * ATTN_GROUP + 2 * jp
                    oe = o_all[(2 * jp) * l:(2 * jp + 1) * l]
                    oo = o_all[(2 * jp + 1) * l:(2 * jp + 2) * l]
                    oc = slice((hq // 2) * 2 * dh, (hq // 2 + 1) * 2 * dh)
                    o_ref[0, qrows, oc] = jnp.where(low, oe, oo).astype(o_ref.dtype)


def _cast_rider_specs(w, steps, steps_per_batch):
    rows = w.shape[0] // steps
    assert rows * steps == w.shape[0] and rows % 8 == 0
    spec = pl.BlockSpec((rows, w.shape[1]), lambda bi, n: (bi * steps_per_batch + n, 0))
    return spec, jax.ShapeDtypeStruct(w.shape, BF16)


def _attn(p16, bias, wa, wb):
    b, s, _ = p16.shape
    l = ATTN_BLOCK
    tq = ATTN_QB * l
    nq = s // tq
    q_blk = P16_BLOCKS.index("aq")
    k_blk = P16_KV_COL // KV_WIDTH
    v_blk = k_blk + 1
    prev_spec = lambda blk: pl.BlockSpec(
        (1, l, KV_WIDTH), lambda bi, n: (bi, jnp.maximum(n * ATTN_QB - 1, 0), blk))
    cur_spec = lambda blk: pl.BlockSpec((1, tq, KV_WIDTH), lambda bi, n: (bi, n, blk))
    wa_spec, wa_shape = _cast_rider_specs(wa, b * nq, nq)
    wb_spec, wb_shape = _cast_rider_specs(wb, b * nq, nq)
    return pl.pallas_call(
        _attn_kernel,
        grid=(b, nq),
        in_specs=[pl.BlockSpec((1, tq, ATTN_WIDTH), lambda bi, n: (bi, n, q_blk)),
                  prev_spec(k_blk), cur_spec(k_blk), prev_spec(v_blk), cur_spec(v_blk),
                  pl.BlockSpec(bias.shape, lambda bi, n: (0, 0, 0, 0), pipeline_mode=pl.Buffered(1)),
                  wa_spec, wb_spec],
        out_specs=[pl.BlockSpec((1, tq, ATTN_WIDTH), lambda bi, n: (bi, n, 0)), wa_spec, wb_spec],
        out_shape=[jax.ShapeDtypeStruct((b, s, ATTN_WIDTH), BF16), wa_shape, wb_shape],
        compiler_params=_params("arbitrary", "arbitrary"),
        name="attn",
    )(p16, p16, p16, p16, p16, bias, wa, wb)


def _outproj_kernel(orec_ref, oatt_ref, x_ref, mod_ref, nw_ref, w_ref, x1_ref, h2_ref, rs_ref):
    ssq = None
    for c0 in range(0, D_MODEL, OUTPROJ_TN):
        cs = slice(c0, c0 + OUTPROJ_TN)
        y = _dot(orec_ref[0], w_ref[0:HGRN_WIDTH, cs]) + _dot(oatt_ref[0], w_ref[HGRN_WIDTH:, cs])
        x1 = x_ref[0, :, cs] + mod_ref[0, 2:3, cs] * y
        x1_ref[0, :, cs] = x1
        sq = x1 * x1
        for k0 in range(0, OUTPROJ_TN, 128):
            ssq = sq[:, k0:k0 + 128] if ssq is None else ssq + sq[:, k0:k0 + 128]
    rs_ref[...] = lax.rsqrt(jnp.sum(ssq, axis=-1, keepdims=True) * (1.0 / D_MODEL) + EPS)

    shift = mod_ref[0, 3:4, :]
    wmod = nw_ref[...] * (1.0 + mod_ref[0, 4:5, :])

    def chunk(rows):
        h2_ref[0, rows, :] = (x1_ref[0, rows, :] * rs_ref[rows, :] * wmod + shift).astype(BF16)

    _for_row_chunks(OUTPROJ_TM, chunk)


def _outproj(o_rec, o_att, x, mod, norm_w, w_bf):
    b, s, d = x.shape
    tm = OUTPROJ_TM
    return pl.pallas_call(
        _outproj_kernel,
        grid=(b, s // tm),
        in_specs=[pl.BlockSpec((1, tm, HGRN_WIDTH), lambda bi, m: (bi, m, 0)),
                  pl.BlockSpec((1, tm, ATTN_WIDTH), lambda bi, m: (bi, m, 0)),
                  pl.BlockSpec((1, tm, d), lambda bi, m: (bi, m, 0)),
                  pl.BlockSpec((1, 6, d), lambda bi, m: (bi, 0, 0)),
                  pl.BlockSpec((1, d), lambda bi, m: (0, 0)),
                  pl.BlockSpec(w_bf.shape, lambda bi, m: (0, 0), pipeline_mode=pl.Buffered(1))],
        out_specs=[pl.BlockSpec((1, tm, d), lambda bi, m: (bi, m, 0)),
                   pl.BlockSpec((1, tm, d), lambda bi, m: (bi, m, 0))],
        out_shape=[jax.ShapeDtypeStruct((b, s, d), F32),
                   jax.ShapeDtypeStruct((b, s, d), BF16)],
        scratch_shapes=[pltpu.VMEM((tm, 1), F32)],
        compiler_params=_params("parallel", "parallel"),
        name="outproj",
    )(o_rec, o_att, x, mod, norm_w, w_bf)


def _ffn_kernel(h_ref, x1_hbm, mod_ref, fw_ref, wg_ref, wu_ref, wd_ref, o_ref, x1_buf, rs_ref, x1_sem,
                *, final_norm):
    bi, m, j = pl.program_id(0), pl.program_id(1), pl.program_id(2)

    def x1_copy():
        rows = pl.ds(pl.multiple_of(m * FFN_TM, FFN_TM), FFN_TM)
        return pltpu.make_async_copy(x1_hbm.at[bi, rows, :], x1_buf, x1_sem)

    last = pl.num_programs(2) - 1
    ksubs = [slice(k0, k0 + FFN_SUB) for k0 in range(0, FFN_TF, FFN_SUB)]
    pieces = [slice(c0, c0 + FFN_DN) for c0 in range(0, D_MODEL, FFN_DN)]

    def acts():
        return [(_silu(_dot(h_ref[0], wg_ref[:, ks])) * _dot(h_ref[0], wu_ref[:, ks])).astype(BF16)
                for ks in ksubs]

    def down(a, cs):
        return functools.reduce(lambda x, y: x + y, [_dot(ai, wd_ref[ks, cs]) for ai, ks in zip(a, ksubs)])

    @pl.when(j == 0)
    def _():
        x1_copy().start()
        a = acts()
        for cs in pieces:
            o_ref[0, :, cs] = down(a, cs)

    @pl.when(jnp.logical_and(j > 0, j < last))
    def _():
        a = acts()
        for cs in pieces:
            o_ref[0, :, cs] += down(a, cs)

    @pl.when(j == last)
    def _():
        x1_copy().wait()
        a = acts()
        ssq = None
        for cs in pieces:
            x2 = x1_buf[:, cs] + mod_ref[0, 5:6, cs] * (o_ref[0, :, cs] + down(a, cs))
            x1_buf[:, cs] = x2
            if final_norm:
                sq = x2 * x2
                for k0 in range(0, FFN_DN, 128):
                    ssq = sq[:, k0:k0 + 128] if ssq is None else ssq + sq[:, k0:k0 + 128]
        if final_norm:
            rs_ref[...] = lax.rsqrt(jnp.sum(ssq, axis=-1, keepdims=True) * (1.0 / D_MODEL) + EPS)
        fw = fw_ref[...]

        def chunk(rows):
            x2 = x1_buf[rows, :]
            o_ref[0, rows, :] = x2 * rs_ref[rows, :] * fw if final_norm else x2

        _for_row_chunks(FFN_TM, chunk)


def _ffn(h2, x1, mod, final_w, wg, wu, wd, final_norm):
    b, s, d = x1.shape
    tm, tf = FFN_TM, FFN_TF
    return pl.pallas_call(
        functools.partial(_ffn_kernel, final_norm=final_norm),
        grid=(b, s // tm, D_FF // tf),
        in_specs=[pl.BlockSpec((1, tm, d), lambda bi, m, j: (bi, m, 0)),
                  pl.BlockSpec(memory_space=pl.ANY),
                  pl.BlockSpec((1, 6, d), lambda bi, m, j: (bi, 0, 0)),
                  pl.BlockSpec((1, d), lambda bi, m, j: (0, 0)),
                  pl.BlockSpec((d, tf), lambda bi, m, j: (0, j)),
                  pl.BlockSpec((d, tf), lambda bi, m, j: (0, j)),
                  pl.BlockSpec((tf, d), lambda bi, m, j: (j, 0))],
        out_specs=pl.BlockSpec((1, tm, d), lambda bi, m, j: (bi, m, 0)),
        out_shape=jax.ShapeDtypeStruct((b, s, d), F32),
        scratch_shapes=[pltpu.VMEM((tm, d), F32), pltpu.VMEM((tm, 1), F32), pltpu.SemaphoreType.DMA(())],
        compiler_params=_params("arbitrary", "arbitrary", "arbitrary"),
        name="ffn",
    )(h2, x1, mod, final_w, wg, wu, wd)


def kernel(x, c, w_ada, b_ada, norm1_w, w_in, lower_bounds, hgrn_norm_w, attn_sinks,
           rel_bias_table, w_out, norm2_w, w_gate, w_up, w_down, final_norm_w):
    b, s, d = x.shape
    assert (d, w_in.shape[-1], w_gate.shape[-1]) == (D_MODEL, IN_WIDTH, D_FF)
    assert s % max(INPROJ_TM, HGRN_TC, OUTPROJ_TM, FFN_TM, ATTN_QB * ATTN_BLOCK) == 0
    depth = w_ada.shape[0]
    c8 = jnp.pad(c, ((0, 8 - b), (0, 0)))
    for layer in range(depth):
        bias, wi16 = _bias_table(rel_bias_table, attn_sinks[layer], w_in[layer])
        mod = _ada(c8, w_ada[layer], b_ada[layer][None, :])[:b].reshape(b, 6, d)
        gate, p16, dec = _inproj(x, mod, norm1_w[layer][None, :], wi16, lower_bounds,
                                 hgrn_norm_w[layer][None, :], layer)
        o_rec, wd16, wo16 = _hgrn(gate, p16, dec, w_down[layer], w_out[layer])
        o_att, wg16, wu16 = _attn(p16, bias, w_gate[layer], w_up[layer])
        x1, h2 = _outproj(o_rec, o_att, x, mod, norm2_w[layer][None, :], wo16)
        x = _ffn(h2, x1, mod, final_norm_w[None, :], wg16, wu16, wd16,
                 final_norm=(layer == depth - 1))
    return x
```

```python
import functools

import numpy as np
import jax
import jax.numpy as jnp
from jax import lax
from jax.experimental import pallas as pl
from jax.experimental.pallas import tpu as pltpu

D_MODEL = 2048
DEPTH = 1
HGRN_WIDTH = 1024
HGRN_HEAD_DIM = 128
HGRN_HEADS = HGRN_WIDTH // HGRN_HEAD_DIM
HGRN_CHUNK = 64
ATTN_WIDTH = 1024
ATTN_HEAD_DIM = 64
ATTN_Q_HEADS = ATTN_WIDTH // ATTN_HEAD_DIM
ATTN_KV_HEADS = 4
ATTN_GROUP = ATTN_Q_HEADS // ATTN_KV_HEADS
WINDOW = 128
ATTN_BLOCK = 128
ATTN_QB = 2
REL_BUCKETS = 32
REL_MAX_DIST = 128
D_FF = 5632
KV_WIDTH = ATTN_KV_HEADS * ATTN_HEAD_DIM
IN_WIDTH = 4 * HGRN_WIDTH + ATTN_WIDTH + 2 * KV_WIDTH
P16_BLOCKS = ("v", "aq", "q_rel", "k_rel", "q_dec", "k_dec")
P16_KV_COL = len(P16_BLOCKS) * HGRN_WIDTH
P16_WIDTH = P16_KV_COL + 2 * KV_WIDTH
EPS = 1e-6
NEG_INF = -1e30
LOG2E = 1.4426950408889634

F32 = jnp.float32
BF16 = jnp.bfloat16

VMEM_LIMIT_BYTES = 56 * 1024 * 1024
BF16_TILE_ROWS = 16

ADA_TN = 1024
INPROJ_TM = 512
INPROJ_TN = 512
HGRN_TC = 512
HGRN_UNROLL = 4
HGRN_ROWS = HGRN_UNROLL * HGRN_CHUNK
OUTPROJ_TM = 512
OUTPROJ_TN = 256
FFN_TM = 1024
FFN_TF = 512
FFN_SUB = 256
FFN_DN = 512
FFN_X1_START = 4
ROW_CHUNK = 16
ROW_UNROLL = 8


def _params(*semantics):
    return pltpu.CompilerParams(dimension_semantics=semantics,
                                vmem_limit_bytes=VMEM_LIMIT_BYTES)


def _sigmoid(v):
    return 1.0 / (1.0 + jnp.exp2(v * (-LOG2E)))


def _silu(v):
    return v * _sigmoid(v)


def _dot(a, b):
    return jnp.dot(a, b, preferred_element_type=F32)


def _dot_nt(a, b):
    return lax.dot_general(a, b, (((1,), (1,)), ((), ())), preferred_element_type=F32)


def _dot_tn(a, b):
    return lax.dot_general(a, b, (((0,), (0,)), ((), ())), preferred_element_type=F32)


def _ada_kernel(c_ref, w_ref, b_ref, o_ref):
    c_act = _silu(c_ref[...])
    o_ref[...] = _dot(c_act.astype(BF16), w_ref[...].astype(BF16)) + b_ref[...]


def _ada(c8, w, b):
    n = w.shape[1]
    return pl.pallas_call(
        _ada_kernel,
        grid=(n // ADA_TN,),
        in_specs=[pl.BlockSpec((8, D_MODEL), lambda j: (0, 0)),
                  pl.BlockSpec((D_MODEL, ADA_TN), lambda j: (0, j)),
                  pl.BlockSpec((1, ADA_TN), lambda j: (0, j))],
        out_specs=pl.BlockSpec((8, ADA_TN), lambda j: (0, j)),
        out_shape=jax.ShapeDtypeStruct((8, n), F32),
        compiler_params=_params("arbitrary"),
        name="ada",
    )(c8, w, b)


def _modulated_norm(x, wmod, shift):
    return x * lax.rsqrt(jnp.mean(x * x, axis=-1, keepdims=True) + EPS) * wmod + shift


def _for_row_chunks(n_rows, fn):
    def body(i, carry):
        fn(pl.ds(pl.multiple_of(i * ROW_CHUNK, ROW_CHUNK), ROW_CHUNK))
        return carry

    lax.fori_loop(0, n_rows // ROW_CHUNK, body, 0, unroll=ROW_UNROLL)


def _split3(v):
    hi = v.astype(BF16)
    r1 = v - hi.astype(F32)
    mid = r1.astype(BF16)
    lo = (r1 - mid.astype(F32)).astype(BF16)
    return hi, mid, lo


def _hgrn_tril3():
    tril = np.tril(np.ones((HGRN_CHUNK, HGRN_CHUNK), np.float32))
    return jnp.asarray(np.concatenate([tril, tril, tril], axis=1), dtype=BF16)


def _hgrn_prepare(q, f_logit, g, lb, nw, tril3):
    c, nu = HGRN_CHUNK, HGRN_UNROLL

    def per_chunk_rows(x, offset):
        rid = lax.broadcasted_iota(jnp.int32, x.shape, 0)
        out = jnp.broadcast_to(x[offset:offset + 1, :], x.shape)
        for u in range(1, nu):
            out = jnp.where(rid >= u * c, jnp.broadcast_to(x[u * c + offset:u * c + offset + 1, :], x.shape), out)
        return out

    f = lb + (1.0 - lb) * _sigmoid(f_logit)
    k = 1.0 - f
    parts = _split3(jnp.log2(f))
    b = jnp.concatenate(
        [_dot(tril3, jnp.concatenate([p[u * c:(u + 1) * c] for p in parts], axis=0)) for u in range(nu)],
        axis=0)
    b_mid = per_chunk_rows(b, c // 2 - 1)
    b_last = per_chunk_rows(b, c - 1)
    d_mid = b - b_mid
    qr = q * jnp.exp2(d_mid)
    kr = k * jnp.exp2(-d_mid)
    q_dec = (qr * jnp.exp2(b_mid)).astype(BF16)
    k_dec = (kr * jnp.exp2(b_last - b_mid)).astype(BF16)
    decays = [jnp.exp2(b[(u + 1) * c - 1:(u + 1) * c, :]) for u in range(nu)]
    return qr.astype(BF16), kr.astype(BF16), q_dec, k_dec, nw * _silu(g), decays


def _inproj_kernel(x_ref, mod_ref, nw_ref, w_ref, lb_ref, nwh_ref, tril_ref, gate_ref, o16_ref, dec_ref,
                   h_ref, *, layer):
    shift = mod_ref[0, 0:1, :]
    wmod = nw_ref[...] * (1.0 + mod_ref[0, 1:2, :])

    def chunk(rows):
        h_ref[rows, :] = _modulated_norm(x_ref[0, rows, :], wmod, shift).astype(BF16)

    _for_row_chunks(INPROJ_TM, chunk)

    lb_rows = [lb_ref[r:r + 1, :] for r in range(lb_ref.shape[0])]
    lb_max = functools.reduce(jnp.maximum, lb_rows)
    lb_exp = [jnp.exp(r - lb_max) for r in lb_rows]
    lb_all = sum(lb_exp[:layer + 1]) / sum(lb_exp)

    tn, hw = INPROJ_TN, HGRN_WIDTH
    p16_col = {name: i * hw for i, name in enumerate(P16_BLOCKS)}

    def project(w_col):
        return _dot(h_ref[...], w_ref[:, w_col:w_col + tn])

    def to_p16(name, c0, value):
        o16_ref[0, :, p16_col[name] + c0:p16_col[name] + c0 + tn] = value.astype(BF16)

    for c0 in range(0, hw, tn):
        cs = slice(c0, c0 + tn)
        hq, hf, hg = project(c0), project(hw + c0), project(3 * hw + c0)
        followers = [lambda: to_p16("v", c0, project(2 * hw + c0)),
                     lambda: to_p16("aq", c0, project(4 * hw + c0))]
        for half in range(INPROJ_TM // HGRN_ROWS):
            rows = slice(half * HGRN_ROWS, (half + 1) * HGRN_ROWS)
            q_rel, k_rel, q_dec, k_dec, gate, decays = _hgrn_prepare(
                hq[rows], hf[rows], hg[rows], lb_all[:, cs], nwh_ref[:, cs], tril_ref[...])
            for name, value in (("q_rel", q_rel), ("k_rel", k_rel), ("q_dec", q_dec), ("k_dec", k_dec)):
                o16_ref[0, rows, p16_col[name] + c0:p16_col[name] + c0 + tn] = value
            gate_ref[0, rows, cs] = gate
            for u, decay in enumerate(decays):
                dec_ref[0, half * HGRN_UNROLL + u:half * HGRN_UNROLL + u + 1, cs] = decay
            followers[half]()
    o16_ref[0, :, P16_KV_COL:] = project(4 * hw + ATTN_WIDTH).astype(BF16)


def _inproj(x, mod, norm_w, w_bf, lower_bounds, hgrn_norm_w, layer):
    b, s, d = x.shape
    tm = INPROJ_TM
    assert tm == 2 * HGRN_ROWS and 2 * KV_WIDTH == INPROJ_TN
    chunks = tm // HGRN_CHUNK
    tril3 = _hgrn_tril3()
    return pl.pallas_call(
        functools.partial(_inproj_kernel, layer=layer),
        grid=(b, s // tm),
        in_specs=[pl.BlockSpec((1, tm, d), lambda bi, m: (bi, m, 0)),
                  pl.BlockSpec((1, 6, d), lambda bi, m: (bi, 0, 0)),
                  pl.BlockSpec((1, d), lambda bi, m: (0, 0)),
                  pl.BlockSpec(w_bf.shape, lambda bi, m: (0, 0), pipeline_mode=pl.Buffered(1)),
                  pl.BlockSpec(lower_bounds.shape, lambda bi, m: (0, 0)),
                  pl.BlockSpec((1, HGRN_WIDTH), lambda bi, m: (0, 0)),
                  pl.BlockSpec(tril3.shape, lambda bi, m: (0, 0))],
        out_specs=[pl.BlockSpec((1, tm, HGRN_WIDTH), lambda bi, m: (bi, m, 0)),
                   pl.BlockSpec((1, tm, P16_WIDTH), lambda bi, m: (bi, m, 0)),
                   pl.BlockSpec((1, chunks, HGRN_WIDTH), lambda bi, m: (bi, m, 0))],
        out_shape=[jax.ShapeDtypeStruct((b, s, HGRN_WIDTH), F32),
                   jax.ShapeDtypeStruct((b, s, P16_WIDTH), BF16),
                   jax.ShapeDtypeStruct((b, s // HGRN_CHUNK, HGRN_WIDTH), F32)],
        scratch_shapes=[pltpu.VMEM((tm, d), BF16)],
        compiler_params=_params("parallel", "parallel"),
        name="inproj",
    )(x, mod, norm_w, w_bf, lower_bounds, hgrn_norm_w, tril3)


def _hgrn_kernel(v_ref, qrel_ref, krel_ref, qdec_ref, kdec_ref, gate_ref, dec_ref, wa_ref, wb_ref,
                 o_ref, wa16_ref, wb16_ref, st_ref):
    c, dk, nu = HGRN_CHUNK, HGRN_HEAD_DIM, HGRN_UNROLL
    wa16_ref[...] = wa_ref[...].astype(BF16)
    wb16_ref[...] = wb_ref[...].astype(BF16)

    @pl.when(pl.program_id(1) == 0)
    def _():
        st_ref[...] = jnp.zeros_like(st_ref)

    causal = (lax.broadcasted_iota(jnp.int32, (c, c), 0) >= lax.broadcasted_iota(jnp.int32, (c, c), 1))
    heads = [slice(h * dk, (h + 1) * dk) for h in range(HGRN_HEADS)]

    def body(i, carry):
        st = [st_ref[h] for h in range(HGRN_HEADS)]
        for u in range(nu):
            rows = pl.ds(pl.multiple_of((i * nu + u) * c, c), c)
            decay = dec_ref[0, pl.ds(i * nu + u, 1), :]
            a = [jnp.where(causal, _dot_nt(qrel_ref[0, rows, hs], krel_ref[0, rows, hs]), 0.0).astype(BF16)
                 for hs in heads]
            o = [_dot(a[h], v_ref[0, rows, hs]) + _dot_nt(qdec_ref[0, rows, hs], st[h].astype(BF16))
                 for h, hs in enumerate(heads)]
            st = [decay[:, hs] * st[h] + _dot_tn(v_ref[0, rows, hs], kdec_ref[0, rows, hs])
                  for h, hs in enumerate(heads)]
            for h, hs in enumerate(heads):
                oh = o[h] * lax.rsqrt(jnp.mean(o[h] * o[h], axis=-1, keepdims=True) + EPS)
                o_ref[0, rows, hs] = (oh * gate_ref[0, rows, hs]).astype(o_ref.dtype)
        for h in range(HGRN_HEADS):
            st_ref[h] = st[h]
        return carry

    lax.fori_loop(0, HGRN_TC // (nu * c), body, 0)


def _hgrn(gate, p16, dec, wa, wb):
    b, s, _ = gate.shape
    nt = s // HGRN_TC
    blk = lambda name: pl.BlockSpec((1, HGRN_TC, HGRN_WIDTH),
                                    lambda bi, t, j=P16_BLOCKS.index(name): (bi, t, j))
    wa_spec, wa_shape = _cast_rider_specs(wa, b * nt, nt)
    wb_spec, wb_shape = _cast_rider_specs(wb, b * nt, nt)
    return pl.pallas_call(
        _hgrn_kernel,
        grid=(b, nt),
        in_specs=[blk("v"), blk("q_rel"), blk("k_rel"), blk("q_dec"), blk("k_dec"),
                  pl.BlockSpec((1, HGRN_TC, HGRN_WIDTH), lambda bi, t: (bi, t, 0)),
                  pl.BlockSpec((1, HGRN_TC // HGRN_CHUNK, HGRN_WIDTH), lambda bi, t: (bi, t, 0)),
                  wa_spec, wb_spec],
        out_specs=[pl.BlockSpec((1, HGRN_TC, HGRN_WIDTH), lambda bi, t: (bi, t, 0)), wa_spec, wb_spec],
        out_shape=[jax.ShapeDtypeStruct((b, s, HGRN_WIDTH), BF16), wa_shape, wb_shape],
        scratch_shapes=[pltpu.VMEM((HGRN_HEADS, HGRN_HEAD_DIM, HGRN_HEAD_DIM), F32)],
        compiler_params=_params("arbitrary", "arbitrary"),
        name="hgrn",
    )(p16, p16, p16, p16, p16, gate, dec, wa, wb)


def _t5_causal_buckets(dist):
    max_exact = REL_BUCKETS // 2
    d = np.maximum(dist, 0)
    log_b = max_exact + (np.log(np.maximum(d, 1) / max_exact)
                         / np.log(REL_MAX_DIST / max_exact)
                         * (REL_BUCKETS - max_exact)).astype(np.int32)
    log_b = np.minimum(log_b, REL_BUCKETS - 1)
    return np.where(d < max_exact, d, log_b).astype(np.int32)


def _bias_kernel(tab_ref, sink_ref, bucket_ref, valid_ref, w_ref, o_ref, w16_ref):
    w16_ref[...] = w_ref[...].astype(BF16)
    h = pl.program_id(0)
    bucket = bucket_ref[...]
    acc = jnp.zeros(bucket.shape, F32)
    for bk in range(REL_BUCKETS):
        acc = jnp.where(bucket == bk, tab_ref[bk, h], acc)
    sink_col = lax.broadcasted_iota(jnp.int32, bucket.shape, 1) == 0
    for variant in range(2):
        masked = jnp.where(valid_ref[variant] != 0, acc, NEG_INF)
        o_ref[variant, 0] = jnp.where(sink_col, sink_ref[h], masked)


def _bias_table(rel_table, sinks, w):
    l = ATTN_BLOCK
    qi = np.arange(l)[:, None]
    kj = np.arange(2 * l)[None, :]
    dist = qi + l - kj
    in_window = (dist >= 0) & (dist < WINDOW)
    valid = np.stack([in_window & (kj >= l), in_window]).astype(np.int32)
    assert not valid[:, :, 0].any()
    bucket = _t5_causal_buckets(dist)
    w_rows = w.shape[0] // ATTN_Q_HEADS
    assert w_rows * ATTN_Q_HEADS == w.shape[0] and w_rows % BF16_TILE_ROWS == 0
    w_spec = pl.BlockSpec((w_rows, w.shape[1]), lambda h: (h, 0))
    return pl.pallas_call(
        _bias_kernel,
        grid=(ATTN_Q_HEADS,),
        in_specs=[pl.BlockSpec(memory_space=pltpu.SMEM),
                  pl.BlockSpec(memory_space=pltpu.SMEM),
                  pl.BlockSpec((l, 2 * l), lambda h: (0, 0)),
                  pl.BlockSpec((2, l, 2 * l), lambda h: (0, 0, 0)),
                  w_spec],
        out_specs=[pl.BlockSpec((2, 1, l, 2 * l), lambda h: (0, h, 0, 0)), w_spec],
        out_shape=[jax.ShapeDtypeStruct((2, ATTN_Q_HEADS, l, 2 * l), F32),
                   jax.ShapeDtypeStruct(w.shape, BF16)],
        compiler_params=_params("arbitrary"),
        name="bias",
    )(rel_table, sinks, jnp.asarray(bucket), jnp.asarray(valid), w)


def _zero_first_row(a):
    top = a[:BF16_TILE_ROWS]
    row = lax.broadcasted_iota(jnp.int32, top.shape, 0)
    return jnp.concatenate([jnp.where(row == 0, 0.0, top).astype(a.dtype), a[BF16_TILE_ROWS:]], axis=0)


def _attn_kernel(q_ref, kp_ref, kc_ref, vp_ref, vc_ref, bias_ref, wa_ref, wb_ref,
                 o_ref, wa16_ref, wb16_ref):
    l = ATTN_BLOCK
    dh = ATTN_HEAD_DIM
    wa16_ref[...] = wa_ref[...].astype(BF16)
    wb16_ref[...] = wb_ref[...].astype(BF16)
    lane = lax.broadcasted_iota(jnp.int32, (1, 2 * dh), 1)
    low = lane < dh
    first = jnp.minimum(pl.program_id(1), 1)

    for pair in range(ATTN_KV_HEADS // 2):
        pc = slice(pair * 2 * dh, (pair + 1) * 2 * dh)
        kk = pltpu.bitcast(jnp.concatenate([kp_ref[0, :, pc], kc_ref[0, :, pc]], axis=0), jnp.int32)
        vv = pltpu.bitcast(jnp.concatenate([vp_ref[0, :, pc], vc_ref[0, :, pc]], axis=0), jnp.int32)
        kk_sw = pltpu.roll(kk, dh, axis=1)
        vv_sw = pltpu.roll(vv, dh, axis=1)
        for sub in range(2):
            kvh = pair * 2 + sub
            if sub == 0:
                k2_all = jnp.where(low, kk, kk_sw)
                v2_all = jnp.where(low, vv, vv_sw)
            else:
                k2_all = jnp.where(low, kk_sw, kk)
                v2_all = jnp.where(low, vv_sw, vv)
            k2_all = pltpu.bitcast(k2_all, BF16)
            v2_all = pltpu.bitcast(v2_all, BF16)
            for t in range(ATTN_QB):
                qrows = slice(t * l, (t + 1) * l)
                k2 = _zero_first_row(k2_all[t * l:(t + 2) * l])
                v2 = _zero_first_row(v2_all[t * l:(t + 2) * l])
                qs = []
                for j in range(ATTN_GROUP):
                    hq = kvh * ATTN_GROUP + j
                    qc = slice((hq // 2) * 2 * dh, (hq // 2 + 1) * 2 * dh)
                    q2 = q_ref[0, qrows, qc] * (dh ** -0.5)
                    keep = low if hq % 2 == 0 else jnp.logical_not(low)
                    qs.append(jnp.where(keep, q2, 0.0).astype(BF16))
                s_all = _dot_nt(jnp.concatenate(qs, axis=0), k2)
                ps = []
                for j in range(ATTN_GROUP):
                    hq = kvh * ATTN_GROUP + j
                    bias = bias_ref[first, hq] if t == 0 else bias_ref[1, hq]
                    s = s_all[j * l:(j + 1) * l] + bias
                    p = jnp.exp(s - jnp.max(s, axis=-1, keepdims=True))
                    ps.append(p.astype(BF16))
                v2x = jnp.concatenate([v2, jnp.ones_like(v2)], axis=1)
                o_all = _dot(jnp.concatenate(ps, axis=0), v2x)
                o_all = o_all[:, :2 * dh] / o_all[:, 2 * dh:]
                for jp in range(ATTN_GROUP // 2):
                    hq = kvh * ATTN_GROUP + 2 * jp
                    oe = o_all[(2 * jp) * l:(2 * jp + 1) * l]
                    oo = o_all[(2 * jp + 1) * l:(2 * jp + 2) * l]
                    oc = slice((hq // 2) * 2 * dh, (hq // 2 + 1) * 2 * dh)
                    o_ref[0, qrows, oc] = jnp.where(low, oe, oo).astype(o_ref.dtype)


def _cast_rider_specs(w, steps, steps_per_batch):
    rows = w.shape[0] // steps
    assert rows * steps == w.shape[0] and rows % 8 == 0
    spec = pl.BlockSpec((rows, w.shape[1]), lambda bi, n: (bi * steps_per_batch + n, 0))
    return spec, jax.ShapeDtypeStruct(w.shape, BF16)


def _attn(p16, bias, wa, wb):
    b, s, _ = p16.shape
    l = ATTN_BLOCK
    tq = ATTN_QB * l
    nq = s // tq
    q_blk = P16_BLOCKS.index("aq")
    k_blk = P16_KV_COL // KV_WIDTH
    v_blk = k_blk + 1
    prev_spec = lambda blk: pl.BlockSpec(
        (1, l, KV_WIDTH), lambda bi, n: (bi, jnp.maximum(n * ATTN_QB - 1, 0), blk))
    cur_spec = lambda blk: pl.BlockSpec((1, tq, KV_WIDTH), lambda bi, n: (bi, n, blk))
    wa_spec, wa_shape = _cast_rider_specs(wa, b * nq, nq)
    wb_spec, wb_shape = _cast_rider_specs(wb, b * nq, nq)
    return pl.pallas_call(
        _attn_kernel,
        grid=(b, nq),
        in_specs=[pl.BlockSpec((1, tq, ATTN_WIDTH), lambda bi, n: (bi, n, q_blk)),
                  prev_spec(k_blk), cur_spec(k_blk), prev_spec(v_blk), cur_spec(v_blk),
                  pl.BlockSpec(bias.shape, lambda bi, n: (0, 0, 0, 0), pipeline_mode=pl.Buffered(1)),
                  wa_spec, wb_spec],
        out_specs=[pl.BlockSpec((1, tq, ATTN_WIDTH), lambda bi, n: (bi, n, 0)), wa_spec, wb_spec],
        out_shape=[jax.ShapeDtypeStruct((b, s, ATTN_WIDTH), BF16), wa_shape, wb_shape],
        compiler_params=_params("arbitrary", "arbitrary"),
        name="attn",
    )(p16, p16, p16, p16, p16, bias, wa, wb)


def _outproj_kernel(orec_ref, oatt_ref, x_ref, mod_ref, nw_ref, w_ref, x1_ref, h2_ref, rs_ref):
    ssq = None
    for c0 in range(0, D_MODEL, OUTPROJ_TN):
        cs = slice(c0, c0 + OUTPROJ_TN)
        y = _dot(orec_ref[0], w_ref[0:HGRN_WIDTH, cs]) + _dot(oatt_ref[0], w_ref[HGRN_WIDTH:, cs])
        x1 = x_ref[0, :, cs] + mod_ref[0, 2:3, cs] * y
        x1_ref[0, :, cs] = x1
        sq = x1 * x1
        for k0 in range(0, OUTPROJ_TN, 128):
            ssq = sq[:, k0:k0 + 128] if ssq is None else ssq + sq[:, k0:k0 + 128]
    rs_ref[...] = lax.rsqrt(jnp.sum(ssq, axis=-1, keepdims=True) * (1.0 / D_MODEL) + EPS)

    shift = mod_ref[0, 3:4, :]
    wmod = nw_ref[...] * (1.0 + mod_ref[0, 4:5, :])

    def chunk(rows):
        h2_ref[0, rows, :] = (x1_ref[0, rows, :] * rs_ref[rows, :] * wmod + shift).astype(BF16)

    _for_row_chunks(OUTPROJ_TM, chunk)


def _outproj(o_rec, o_att, x, mod, norm_w, w_bf):
    b, s, d = x.shape
    tm = OUTPROJ_TM
    return pl.pallas_call(
        _outproj_kernel,
        grid=(b, s // tm),
        in_specs=[pl.BlockSpec((1, tm, HGRN_WIDTH), lambda bi, m: (bi, m, 0)),
                  pl.BlockSpec((1, tm, ATTN_WIDTH), lambda bi, m: (bi, m, 0)),
                  pl.BlockSpec((1, tm, d), lambda bi, m: (bi, m, 0)),
                  pl.BlockSpec((1, 6, d), lambda bi, m: (bi, 0, 0)),
                  pl.BlockSpec((1, d), lambda bi, m: (0, 0)),
                  pl.BlockSpec(w_bf.shape, lambda bi, m: (0, 0), pipeline_mode=pl.Buffered(1))],
        out_specs=[pl.BlockSpec((1, tm, d), lambda bi, m: (bi, m, 0)),
                   pl.BlockSpec((1, tm, d), lambda bi, m: (bi, m, 0))],
        out_shape=[jax.ShapeDtypeStruct((b, s, d), F32),
                   jax.ShapeDtypeStruct((b, s, d), BF16)],
        scratch_shapes=[pltpu.VMEM((tm, 1), F32)],
        compiler_params=_params("parallel", "parallel"),
        name="outproj",
    )(o_rec, o_att, x, mod, norm_w, w_bf)


def _ffn_kernel(h_ref, x1_hbm, mod_ref, fw_ref, wg_ref, wu_ref, wd_ref, o_ref, x1_buf, rs_ref, x1_sem,
                *, final_norm):
    bi, m, j = pl.program_id(0), pl.program_id(1), pl.program_id(2)

    def x1_copy():
        rows = pl.ds(pl.multiple_of(m * FFN_TM, FFN_TM), FFN_TM)
        return pltpu.make_async_copy(x1_hbm.at[bi, rows, :], x1_buf, x1_sem)

    last = pl.num_programs(2) - 1
    ksubs = [slice(k0, k0 + FFN_SUB) for k0 in range(0, FFN_TF, FFN_SUB)]
    pieces = [slice(c0, c0 + FFN_DN) for c0 in range(0, D_MODEL, FFN_DN)]

    def acts():
        return [(_silu(_dot(h_ref[0], wg_ref[:, ks])) * _dot(h_ref[0], wu_ref[:, ks])).astype(BF16)
                for ks in ksubs]

    def down(a, cs):
        return functools.reduce(lambda x, y: x + y, [_dot(ai, wd_ref[ks, cs]) for ai, ks in zip(a, ksubs)])

    @pl.when(j == FFN_X1_START)
    def _():
        x1_copy().start()

    @pl.when(j == 0)
    def _():
        a = acts()
        for cs in pieces:
            o_ref[0, :, cs] = down(a, cs)

    @pl.when(jnp.logical_and(j > 0, j < last))
    def _():
        a = acts()
        for cs in pieces:
            o_ref[0, :, cs] += down(a, cs)

    @pl.when(j == last)
    def _():
        x1_copy().wait()
        a = acts()
        ssq = None
        for cs in pieces:
            x2 = x1_buf[:, cs] + mod_ref[0, 5:6, cs] * (o_ref[0, :, cs] + down(a, cs))
            x1_buf[:, cs] = x2
            if final_norm:
                sq = x2 * x2
                for k0 in range(0, FFN_DN, 128):
                    ssq = sq[:, k0:k0 + 128] if ssq is None else ssq + sq[:, k0:k0 + 128]
        if final_norm:
            rs_ref[...] = lax.rsqrt(jnp.sum(ssq, axis=-1, keepdims=True) * (1.0 / D_MODEL) + EPS)
        fw = fw_ref[...]

        def chunk(rows):
            x2 = x1_buf[rows, :]
            o_ref[0, rows, :] = x2 * rs_ref[rows, :] * fw if final_norm else x2

        _for_row_chunks(FFN_TM, chunk)


def _ffn(h2, x1, mod, final_w, wg, wu, wd, final_norm):
    b, s, d = x1.shape
    tm, tf = FFN_TM, FFN_TF
    assert 0 <= FFN_X1_START < D_FF // tf - 1
    return pl.pallas_call(
        functools.partial(_ffn_kernel, final_norm=final_norm),
        grid=(b, s // tm, D_FF // tf),
        in_specs=[pl.BlockSpec((1, tm, d), lambda bi, m, j: (bi, m, 0)),
                  pl.BlockSpec(memory_space=pl.ANY),
                  pl.BlockSpec((1, 6, d), lambda bi, m, j: (bi, 0, 0)),
                  pl.BlockSpec((1, d), lambda bi, m, j: (0, 0)),
                  pl.BlockSpec((d, tf), lambda bi, m, j: (0, j)),
                  pl.BlockSpec((d, tf), lambda bi, m, j: (0, j)),
                  pl.BlockSpec((tf, d), lambda bi, m, j: (j, 0))],
        out_specs=pl.BlockSpec((1, tm, d), lambda bi, m, j: (bi, m, 0)),
        out_shape=jax.ShapeDtypeStruct((b, s, d), F32),
        scratch_shapes=[pltpu.VMEM((tm, d), F32), pltpu.VMEM((tm, 1), F32), pltpu.SemaphoreType.DMA(())],
        compiler_params=_params("arbitrary", "arbitrary", "arbitrary"),
        name="ffn",
    )(h2, x1, mod, final_w, wg, wu, wd)


def kernel(x, c, w_ada, b_ada, norm1_w, w_in, lower_bounds, hgrn_norm_w, attn_sinks,
           rel_bias_table, w_out, norm2_w, w_gate, w_up, w_down, final_norm_w):
    b, s, d = x.shape
    assert (d, w_in.shape[-1], w_gate.shape[-1]) == (D_MODEL, IN_WIDTH, D_FF)
    assert s % max(INPROJ_TM, HGRN_TC, OUTPROJ_TM, FFN_TM, ATTN_QB * ATTN_BLOCK) == 0
    depth = w_ada.shape[0]
    c8 = jnp.pad(c, ((0, 8 - b), (0, 0)))
    for layer in range(depth):
        bias, wi16 = _bias_table(rel_bias_table, attn_sinks[layer], w_in[layer])
        mod = _ada(c8, w_ada[layer], b_ada[layer][None, :])[:b].reshape(b, 6, d)
        gate, p16, dec = _inproj(x, mod, norm1_w[layer][None, :], wi16, lower_bounds,
                                 hgrn_norm_w[layer][None, :], layer)
        o_rec, wd16, wo16 = _hgrn(gate, p16, dec, w_down[layer], w_out[layer])
        o_att, wg16, wu16 = _attn(p16, bias, w_gate[layer], w_up[layer])
        x1, h2 = _outproj(o_rec, o_att, x, mod, norm2_w[layer][None, :], wo16)
        x = _ffn(h2, x1, mod, final_norm_w[None, :], wg16, wu16, wd16,
                 final_norm=(layer == depth - 1))
    return x
```

```python
import functools

import numpy as np
import jax
import jax.numpy as jnp
from jax import lax
from jax.experimental import pallas as pl
from jax.experimental.pallas import tpu as pltpu

D_MODEL = 2048
DEPTH = 1
HGRN_WIDTH = 1024
HGRN_HEAD_DIM = 128
HGRN_HEADS = HGRN_WIDTH // HGRN_HEAD_DIM
HGRN_CHUNK = 64
ATTN_WIDTH = 1024
ATTN_HEAD_DIM = 64
ATTN_Q_HEADS = ATTN_WIDTH // ATTN_HEAD_DIM
ATTN_KV_HEADS = 4
ATTN_GROUP = ATTN_Q_HEADS // ATTN_KV_HEADS
WINDOW = 128
ATTN_BLOCK = 128
ATTN_QB = 2
REL_BUCKETS = 32
REL_MAX_DIST = 128
D_FF = 5632
KV_WIDTH = ATTN_KV_HEADS * ATTN_HEAD_DIM
IN_WIDTH = 4 * HGRN_WIDTH + ATTN_WIDTH + 2 * KV_WIDTH
P16_BLOCKS = ("v", "aq", "q_rel", "k_rel", "q_dec", "k_dec")
P16_KV_COL = len(P16_BLOCKS) * HGRN_WIDTH
P16_WIDTH = P16_KV_COL + 2 * KV_WIDTH
EPS = 1e-6
NEG_INF = -1e30
LOG2E = 1.4426950408889634

F32 = jnp.float32
BF16 = jnp.bfloat16

VMEM_LIMIT_BYTES = 56 * 1024 * 1024
BF16_TILE_ROWS = 16

ADA_TN = 1024
INPROJ_TM = 512
INPROJ_TN = 256
HGRN_TC = 512
HGRN_UNROLL = 4
HGRN_ROWS = HGRN_UNROLL * HGRN_CHUNK
HGRN_TRIP_CHUNKS = 8
OUTPROJ_TM = 512
OUTPROJ_TN = 256
FFN_TM = 1024
FFN_TF = 512
FFN_SUB = 256
FFN_DN = 512
ROW_CHUNK = 16
ROW_UNROLL = 8


def _params(*semantics):
    return pltpu.CompilerParams(dimension_semantics=semantics,
                                vmem_limit_bytes=VMEM_LIMIT_BYTES)


def _sigmoid(v):
    return 1.0 / (1.0 + jnp.exp2(v * (-LOG2E)))


def _silu(v):
    return v * _sigmoid(v)


def _dot(a, b):
    return jnp.dot(a, b, preferred_element_type=F32)


def _dot_nt(a, b):
    return lax.dot_general(a, b, (((1,), (1,)), ((), ())), preferred_element_type=F32)


def _dot_tn(a, b):
    return lax.dot_general(a, b, (((0,), (0,)), ((), ())), preferred_element_type=F32)


def _ada_kernel(c_ref, w_ref, b_ref, o_ref):
    c_act = _silu(c_ref[...])
    o_ref[...] = _dot(c_act.astype(BF16), w_ref[...].astype(BF16)) + b_ref[...]


def _ada(c8, w, b):
    n = w.shape[1]
    return pl.pallas_call(
        _ada_kernel,
        grid=(n // ADA_TN,),
        in_specs=[pl.BlockSpec((8, D_MODEL), lambda j: (0, 0)),
                  pl.BlockSpec((D_MODEL, ADA_TN), lambda j: (0, j)),
                  pl.BlockSpec((1, ADA_TN), lambda j: (0, j))],
        out_specs=pl.BlockSpec((8, ADA_TN), lambda j: (0, j)),
        out_shape=jax.ShapeDtypeStruct((8, n), F32),
        compiler_params=_params("arbitrary"),
        name="ada",
    )(c8, w, b)


def _modulated_norm(x, wmod, shift):
    return x * lax.rsqrt(jnp.mean(x * x, axis=-1, keepdims=True) + EPS) * wmod + shift


def _for_row_chunks(n_rows, fn):
    def body(i, carry):
        fn(pl.ds(pl.multiple_of(i * ROW_CHUNK, ROW_CHUNK), ROW_CHUNK))
        return carry

    lax.fori_loop(0, n_rows // ROW_CHUNK, body, 0, unroll=ROW_UNROLL)


def _split3(v):
    hi = v.astype(BF16)
    r1 = v - hi.astype(F32)
    mid = r1.astype(BF16)
    lo = (r1 - mid.astype(F32)).astype(BF16)
    return hi, mid, lo


def _hgrn_tril3():
    tril = np.tril(np.ones((HGRN_CHUNK, HGRN_CHUNK), np.float32))
    return jnp.asarray(np.concatenate([tril, tril, tril], axis=1), dtype=BF16)


def _hgrn_prepare(q, f_logit, g, lb, nw, tril3):
    c, nu = HGRN_CHUNK, HGRN_UNROLL

    def per_chunk_rows(x, offset):
        rid = lax.broadcasted_iota(jnp.int32, x.shape, 0)
        out = jnp.broadcast_to(x[offset:offset + 1, :], x.shape)
        for u in range(1, nu):
            out = jnp.where(rid >= u * c, jnp.broadcast_to(x[u * c + offset:u * c + offset + 1, :], x.shape), out)
        return out

    f = lb + (1.0 - lb) * _sigmoid(f_logit)
    k = 1.0 - f
    parts = _split3(jnp.log2(f))
    b = jnp.concatenate(
        [_dot(tril3, jnp.concatenate([p[u * c:(u + 1) * c] for p in parts], axis=0)) for u in range(nu)],
        axis=0)
    b_mid = per_chunk_rows(b, c // 2 - 1)
    b_last = per_chunk_rows(b, c - 1)
    d_mid = b - b_mid
    qr = q * jnp.exp2(d_mid)
    kr = k * jnp.exp2(-d_mid)
    q_dec = (qr * jnp.exp2(b_mid)).astype(BF16)
    k_dec = (kr * jnp.exp2(b_last - b_mid)).astype(BF16)
    decays = [jnp.exp2(b[(u + 1) * c - 1:(u + 1) * c, :]) for u in range(nu)]
    return qr.astype(BF16), kr.astype(BF16), q_dec, k_dec, nw * _silu(g), decays


def _inproj_kernel(x_ref, mod_ref, nw_ref, w_ref, lb_ref, nwh_ref, tril_ref, gate_ref, o16_ref, dec_ref,
                   h_ref, *, layer):
    shift = mod_ref[0, 0:1, :]
    wmod = nw_ref[...] * (1.0 + mod_ref[0, 1:2, :])

    def chunk(rows):
        h_ref[rows, :] = _modulated_norm(x_ref[0, rows, :], wmod, shift).astype(BF16)

    _for_row_chunks(INPROJ_TM, chunk)

    lb_rows = [lb_ref[r:r + 1, :] for r in range(lb_ref.shape[0])]
    lb_max = functools.reduce(jnp.maximum, lb_rows)
    lb_exp = [jnp.exp(r - lb_max) for r in lb_rows]
    lb_all = sum(lb_exp[:layer + 1]) / sum(lb_exp)

    tn, hw = INPROJ_TN, HGRN_WIDTH
    p16_col = {name: i * hw for i, name in enumerate(P16_BLOCKS)}

    def project(w_col):
        return _dot(h_ref[...], w_ref[:, w_col:w_col + tn])

    def to_p16(name, c0, value):
        o16_ref[0, :, p16_col[name] + c0:p16_col[name] + c0 + tn] = value.astype(BF16)

    for c0 in range(0, hw, tn):
        cs = slice(c0, c0 + tn)
        hq, hf, hg = project(c0), project(hw + c0), project(3 * hw + c0)
        followers = [lambda: to_p16("v", c0, project(2 * hw + c0)),
                     lambda: to_p16("aq", c0, project(4 * hw + c0))]
        for half in range(INPROJ_TM // HGRN_ROWS):
            rows = slice(half * HGRN_ROWS, (half + 1) * HGRN_ROWS)
            q_rel, k_rel, q_dec, k_dec, gate, decays = _hgrn_prepare(
                hq[rows], hf[rows], hg[rows], lb_all[:, cs], nwh_ref[:, cs], tril_ref[...])
            for name, value in (("q_rel", q_rel), ("k_rel", k_rel), ("q_dec", q_dec), ("k_dec", k_dec)):
                o16_ref[0, rows, p16_col[name] + c0:p16_col[name] + c0 + tn] = value
            gate_ref[0, rows, cs] = gate
            for u, decay in enumerate(decays):
                dec_ref[0, half * HGRN_UNROLL + u:half * HGRN_UNROLL + u + 1, cs] = decay
            followers[half]()
    for c0 in range(0, 2 * KV_WIDTH, tn):
        o16_ref[0, :, P16_KV_COL + c0:P16_KV_COL + c0 + tn] = project(4 * hw + ATTN_WIDTH + c0).astype(BF16)


def _inproj(x, mod, norm_w, w_bf, lower_bounds, hgrn_norm_w, layer):
    b, s, d = x.shape
    tm = INPROJ_TM
    assert tm == 2 * HGRN_ROWS and (2 * KV_WIDTH) % INPROJ_TN == 0
    chunks = tm // HGRN_CHUNK
    tril3 = _hgrn_tril3()
    return pl.pallas_call(
        functools.partial(_inproj_kernel, layer=layer),
        grid=(b, s // tm),
        in_specs=[pl.BlockSpec((1, tm, d), lambda bi, m: (bi, m, 0)),
                  pl.BlockSpec((1, 6, d), lambda bi, m: (bi, 0, 0)),
                  pl.BlockSpec((1, d), lambda bi, m: (0, 0)),
                  pl.BlockSpec(w_bf.shape, lambda bi, m: (0, 0), pipeline_mode=pl.Buffered(1)),
                  pl.BlockSpec(lower_bounds.shape, lambda bi, m: (0, 0)),
                  pl.BlockSpec((1, HGRN_WIDTH), lambda bi, m: (0, 0)),
                  pl.BlockSpec(tril3.shape, lambda bi, m: (0, 0))],
        out_specs=[pl.BlockSpec((1, tm, HGRN_WIDTH), lambda bi, m: (bi, m, 0)),
                   pl.BlockSpec((1, tm, P16_WIDTH), lambda bi, m: (bi, m, 0)),
                   pl.BlockSpec((1, chunks, HGRN_WIDTH), lambda bi, m: (bi, m, 0))],
        out_shape=[jax.ShapeDtypeStruct((b, s, HGRN_WIDTH), F32),
                   jax.ShapeDtypeStruct((b, s, P16_WIDTH), BF16),
                   jax.ShapeDtypeStruct((b, s // HGRN_CHUNK, HGRN_WIDTH), F32)],
        scratch_shapes=[pltpu.VMEM((tm, d), BF16)],
        compiler_params=_params("parallel", "parallel"),
        name="inproj",
    )(x, mod, norm_w, w_bf, lower_bounds, hgrn_norm_w, tril3)


def _hgrn_kernel(v_ref, qrel_ref, krel_ref, qdec_ref, kdec_ref, gate_ref, dec_ref, wa_ref, wb_ref,
                 o_ref, wa16_ref, wb16_ref, st_ref):
    c, dk, nu = HGRN_CHUNK, HGRN_HEAD_DIM, HGRN_TRIP_CHUNKS
    wa16_ref[...] = wa_ref[...].astype(BF16)
    wb16_ref[...] = wb_ref[...].astype(BF16)

    @pl.when(pl.program_id(1) == 0)
    def _():
        st_ref[...] = jnp.zeros_like(st_ref)

    causal = (lax.broadcasted_iota(jnp.int32, (c, c), 0) >= lax.broadcasted_iota(jnp.int32, (c, c), 1))
    heads = [slice(h * dk, (h + 1) * dk) for h in range(HGRN_HEADS)]

    def body(i, carry):
        st = [st_ref[h] for h in range(HGRN_HEADS)]
        for u in range(nu):
            rows = pl.ds(pl.multiple_of((i * nu + u) * c, c), c)
            decay = dec_ref[0, pl.ds(i * nu + u, 1), :]
            a = [jnp.where(causal, _dot_nt(qrel_ref[0, rows, hs], krel_ref[0, rows, hs]), 0.0).astype(BF16)
                 for hs in heads]
            o = [_dot(a[h], v_ref[0, rows, hs]) + _dot_nt(qdec_ref[0, rows, hs], st[h].astype(BF16))
                 for h, hs in enumerate(heads)]
            st = [decay[:, hs] * st[h] + _dot_tn(v_ref[0, rows, hs], kdec_ref[0, rows, hs])
                  for h, hs in enumerate(heads)]
            for h, hs in enumerate(heads):
                oh = o[h] * lax.rsqrt(jnp.mean(o[h] * o[h], axis=-1, keepdims=True) + EPS)
                o_ref[0, rows, hs] = (oh * gate_ref[0, rows, hs]).astype(o_ref.dtype)
        for h in range(HGRN_HEADS):
            st_ref[h] = st[h]
        return carry

    lax.fori_loop(0, HGRN_TC // (nu * c), body, 0)


def _hgrn(gate, p16, dec, wa, wb):
    b, s, _ = gate.shape
    nt = s // HGRN_TC
    blk = lambda name: pl.BlockSpec((1, HGRN_TC, HGRN_WIDTH),
                                    lambda bi, t, j=P16_BLOCKS.index(name): (bi, t, j))
    wa_spec, wa_shape = _cast_rider_specs(wa, b * nt, nt)
    wb_spec, wb_shape = _cast_rider_specs(wb, b * nt, nt)
    return pl.pallas_call(
        _hgrn_kernel,
        grid=(b, nt),
        in_specs=[blk("v"), blk("q_rel"), blk("k_rel"), blk("q_dec"), blk("k_dec"),
                  pl.BlockSpec((1, HGRN_TC, HGRN_WIDTH), lambda bi, t: (bi, t, 0)),
                  pl.BlockSpec((1, HGRN_TC // HGRN_CHUNK, HGRN_WIDTH), lambda bi, t: (bi, t, 0)),
                  wa_spec, wb_spec],
        out_specs=[pl.BlockSpec((1, HGRN_TC, HGRN_WIDTH), lambda bi, t: (bi, t, 0)), wa_spec, wb_spec],
        out_shape=[jax.ShapeDtypeStruct((b, s, HGRN_WIDTH), BF16), wa_shape, wb_shape],
        scratch_shapes=[pltpu.VMEM((HGRN_HEADS, HGRN_HEAD_DIM, HGRN_HEAD_DIM), F32)],
        compiler_params=_params("arbitrary", "arbitrary"),
        name="hgrn",
    )(p16, p16, p16, p16, p16, gate, dec, wa, wb)


def _t5_causal_buckets(dist):
    max_exact = REL_BUCKETS // 2
    d = np.maximum(dist, 0)
    log_b = max_exact + (np.log(np.maximum(d, 1) / max_exact)
                         / np.log(REL_MAX_DIST / max_exact)
                         * (REL_BUCKETS - max_exact)).astype(np.int32)
    log_b = np.minimum(log_b, REL_BUCKETS - 1)
    return np.where(d < max_exact, d, log_b).astype(np.int32)


def _bias_kernel(tab_ref, sink_ref, bucket_ref, valid_ref, w_ref, o_ref, w16_ref):
    w16_ref[...] = w_ref[...].astype(BF16)
    h = pl.program_id(0)
    bucket = bucket_ref[...]
    acc = jnp.zeros(bucket.shape, F32)
    for bk in range(REL_BUCKETS):
        acc = jnp.where(bucket == bk, tab_ref[bk, h], acc)
    sink_col = lax.broadcasted_iota(jnp.int32, bucket.shape, 1) == 0
    for variant in range(2):
        masked = jnp.where(valid_ref[variant] != 0, acc, NEG_INF)
        o_ref[variant, 0] = jnp.where(sink_col, sink_ref[h], masked)


def _bias_table(rel_table, sinks, w):
    l = ATTN_BLOCK
    qi = np.arange(l)[:, None]
    kj = np.arange(2 * l)[None, :]
    dist = qi + l - kj
    in_window = (dist >= 0) & (dist < WINDOW)
    valid = np.stack([in_window & (kj >= l), in_window]).astype(np.int32)
    assert not valid[:, :, 0].any()
    bucket = _t5_causal_buckets(dist)
    w_rows = w.shape[0] // ATTN_Q_HEADS
    assert w_rows * ATTN_Q_HEADS == w.shape[0] and w_rows % BF16_TILE_ROWS == 0
    w_spec = pl.BlockSpec((w_rows, w.shape[1]), lambda h: (h, 0))
    return pl.pallas_call(
        _bias_kernel,
        grid=(ATTN_Q_HEADS,),
        in_specs=[pl.BlockSpec(memory_space=pltpu.SMEM),
                  pl.BlockSpec(memory_space=pltpu.SMEM),
                  pl.BlockSpec((l, 2 * l), lambda h: (0, 0)),
                  pl.BlockSpec((2, l, 2 * l), lambda h: (0, 0, 0)),
                  w_spec],
        out_specs=[pl.BlockSpec((2, 1, l, 2 * l), lambda h: (0, h, 0, 0)), w_spec],
        out_shape=[jax.ShapeDtypeStruct((2, ATTN_Q_HEADS, l, 2 * l), F32),
                   jax.ShapeDtypeStruct(w.shape, BF16)],
        compiler_params=_params("arbitrary"),
        name="bias",
    )(rel_table, sinks, jnp.asarray(bucket), jnp.asarray(valid), w)


def _zero_first_row(a):
    top = a[:BF16_TILE_ROWS]
    row = lax.broadcasted_iota(jnp.int32, top.shape, 0)
    return jnp.concatenate([jnp.where(row == 0, 0.0, top).astype(a.dtype), a[BF16_TILE_ROWS:]], axis=0)


def _attn_kernel(q_ref, kp_ref, kc_ref, vp_ref, vc_ref, bias_ref, wa_ref, wb_ref,
                 o_ref, wa16_ref, wb16_ref):
    l = ATTN_BLOCK
    dh = ATTN_HEAD_DIM
    wa16_ref[...] = wa_ref[...].astype(BF16)
    wb16_ref[...] = wb_ref[...].astype(BF16)
    lane = lax.broadcasted_iota(jnp.int32, (1, 2 * dh), 1)
    low = lane < dh
    first = jnp.minimum(pl.program_id(1), 1)

    for pair in range(ATTN_KV_HEADS // 2):
        pc = slice(pair * 2 * dh, (pair + 1) * 2 * dh)
        kk = pltpu.bitcast(jnp.concatenate([kp_ref[0, :, pc], kc_ref[0, :, pc]], axis=0), jnp.int32)
        vv = pltpu.bitcast(jnp.concatenate([vp_ref[0, :, pc], vc_ref[0, :, pc]], axis=0), jnp.int32)
        kk_sw = pltpu.roll(kk, dh, axis=1)
        vv_sw = pltpu.roll(vv, dh, axis=1)
        for sub in range(2):
            kvh = pair * 2 + sub
            if sub == 0:
                k2_all = jnp.where(low, kk, kk_sw)
                v2_all = jnp.where(low, vv, vv_sw)
            else:
                k2_all = jnp.where(low, kk_sw, kk)
                v2_all = jnp.where(low, vv_sw, vv)
            k2_all = pltpu.bitcast(k2_all, BF16)
            v2_all = pltpu.bitcast(v2_all, BF16)
            for t in range(ATTN_QB):
                qrows = slice(t * l, (t + 1) * l)
                k2 = _zero_first_row(k2_all[t * l:(t + 2) * l])
                v2 = _zero_first_row(v2_all[t * l:(t + 2) * l])
                qs = []
                for j in range(ATTN_GROUP):
                    hq = kvh * ATTN_GROUP + j
                    qc = slice((hq // 2) * 2 * dh, (hq // 2 + 1) * 2 * dh)
                    q2 = q_ref[0, qrows, qc] * (dh ** -0.5)
                    keep = low if hq % 2 == 0 else jnp.logical_not(low)
                    qs.append(jnp.where(keep, q2, 0.0).astype(BF16))
                s_all = _dot_nt(jnp.concatenate(qs, axis=0), k2)
                ps = []
                for j in range(ATTN_GROUP):
                    hq = kvh * ATTN_GROUP + j
                    bias = bias_ref[first, hq] if t == 0 else bias_ref[1, hq]
                    s = s_all[j * l:(j + 1) * l] + bias
                    p = jnp.exp(s - jnp.max(s, axis=-1, keepdims=True))
                    ps.append(p.astype(BF16))
                v2x = jnp.concatenate([v2, jnp.ones_like(v2)], axis=1)
                o_all = _dot(jnp.concatenate(ps, axis=0), v2x)
                o_all = o_all[:, :2 * dh] / o_all[:, 2 * dh:]
                for jp in range(ATTN_GROUP // 2):
                    hq = kvh * ATTN_GROUP + 2 * jp
                    oe = o_all[(2 * jp) * l:(2 * jp + 1) * l]
                    oo = o_all[(2 * jp + 1) * l:(2 * jp + 2) * l]
                    oc = slice((hq // 2) * 2 * dh, (hq // 2 + 1) * 2 * dh)
                    o_ref[0, qrows, oc] = jnp.where(low, oe, oo).astype(o_ref.dtype)


def _cast_rider_specs(w, steps, steps_per_batch):
    rows = w.shape[0] // steps
    assert rows * steps == w.shape[0] and rows % 8 == 0
    spec = pl.BlockSpec((rows, w.shape[1]), lambda bi, n: (bi * steps_per_batch + n, 0))
    return spec, jax.ShapeDtypeStruct(w.shape, BF16)


def _attn(p16, bias, wa, wb):
    b, s, _ = p16.shape
    l = ATTN_BLOCK
    tq = ATTN_QB * l
    nq = s // tq
    q_blk = P16_BLOCKS.index("aq")
    k_blk = P16_KV_COL // KV_WIDTH
    v_blk = k_blk + 1
    prev_spec = lambda blk: pl.BlockSpec(
        (1, l, KV_WIDTH), lambda bi, n: (bi, jnp.maximum(n * ATTN_QB - 1, 0), blk))
    cur_spec = lambda blk: pl.BlockSpec((1, tq, KV_WIDTH), lambda bi, n: (bi, n, blk))
    wa_spec, wa_shape = _cast_rider_specs(wa, b * nq, nq)
    wb_spec, wb_shape = _cast_rider_specs(wb, b * nq, nq)
    return pl.pallas_call(
        _attn_kernel,
        grid=(b, nq),
        in_specs=[pl.BlockSpec((1, tq, ATTN_WIDTH), lambda bi, n: (bi, n, q_blk)),
                  prev_spec(k_blk), cur_spec(k_blk), prev_spec(v_blk), cur_spec(v_blk),
                  pl.BlockSpec(bias.shape, lambda bi, n: (0, 0, 0, 0), pipeline_mode=pl.Buffered(1)),
                  wa_spec, wb_spec],
        out_specs=[pl.BlockSpec((1, tq, ATTN_WIDTH), lambda bi, n: (bi, n, 0)), wa_spec, wb_spec],
        out_shape=[jax.ShapeDtypeStruct((b, s, ATTN_WIDTH), BF16), wa_shape, wb_shape],
        compiler_params=_params("arbitrary", "arbitrary"),
        name="attn",
    )(p16, p16, p16, p16, p16, bias, wa, wb)


def _outproj_kernel(orec_ref, oatt_ref, x_ref, mod_ref, nw_ref, w_ref, x1_ref, h2_ref, rs_ref):
    ssq = None
    for c0 in range(0, D_MODEL, OUTPROJ_TN):
        cs = slice(c0, c0 + OUTPROJ_TN)
        y = _dot(orec_ref[0], w_ref[0:HGRN_WIDTH, cs]) + _dot(oatt_ref[0], w_ref[HGRN_WIDTH:, cs])
        x1 = x_ref[0, :, cs] + mod_ref[0, 2:3, cs] * y
        x1_ref[0, :, cs] = x1
        sq = x1 * x1
        for k0 in range(0, OUTPROJ_TN, 128):
            ssq = sq[:, k0:k0 + 128] if ssq is None else ssq + sq[:, k0:k0 + 128]
    rs_ref[...] = lax.rsqrt(jnp.sum(ssq, axis=-1, keepdims=True) * (1.0 / D_MODEL) + EPS)

    shift = mod_ref[0, 3:4, :]
    wmod = nw_ref[...] * (1.0 + mod_ref[0, 4:5, :])

    def chunk(rows):
        h2_ref[0, rows, :] = (x1_ref[0, rows, :] * rs_ref[rows, :] * wmod + shift).astype(BF16)

    _for_row_chunks(OUTPROJ_TM, chunk)


def _outproj(o_rec, o_att, x, mod, norm_w, w_bf):
    b, s, d = x.shape
    tm = OUTPROJ_TM
    return pl.pallas_call(
        _outproj_kernel,
        grid=(b, s // tm),
        in_specs=[pl.BlockSpec((1, tm, HGRN_WIDTH), lambda bi, m: (bi, m, 0)),
                  pl.BlockSpec((1, tm, ATTN_WIDTH), lambda bi, m: (bi, m, 0)),
                  pl.BlockSpec((1, tm, d), lambda bi, m: (bi, m, 0)),
                  pl.BlockSpec((1, 6, d), lambda bi, m: (bi, 0, 0)),
                  pl.BlockSpec((1, d), lambda bi, m: (0, 0)),
                  pl.BlockSpec(w_bf.shape, lambda bi, m: (0, 0), pipeline_mode=pl.Buffered(1))],
        out_specs=[pl.BlockSpec((1, tm, d), lambda bi, m: (bi, m, 0)),
                   pl.BlockSpec((1, tm, d), lambda bi, m: (bi, m, 0))],
        out_shape=[jax.ShapeDtypeStruct((b, s, d), F32),
                   jax.ShapeDtypeStruct((b, s, d), BF16)],
        scratch_shapes=[pltpu.VMEM((tm, 1), F32)],
        compiler_params=_params("parallel", "parallel"),
        name="outproj",
    )(o_rec, o_att, x, mod, norm_w, w_bf)


def _ffn_kernel(h_ref, x1_hbm, mod_ref, fw_ref, wg_ref, wu_ref, wd_ref, o_ref, x1_buf, rs_ref, x1_sem,
                *, final_norm):
    bi, m, j = pl.program_id(0), pl.program_id(1), pl.program_id(2)

    def x1_copy():
        rows = pl.ds(pl.multiple_of(m * FFN_TM, FFN_TM), FFN_TM)
        return pltpu.make_async_copy(x1_hbm.at[bi, rows, :], x1_buf, x1_sem)

    last = pl.num_programs(2) - 1
    ksubs = [slice(k0, k0 + FFN_SUB) for k0 in range(0, FFN_TF, FFN_SUB)]
    pieces = [slice(c0, c0 + FFN_DN) for c0 in range(0, D_MODEL, FFN_DN)]

    def acts():
        return [(_silu(_dot(h_ref[0], wg_ref[:, ks])) * _dot(h_ref[0], wu_ref[:, ks])).astype(BF16)
                for ks in ksubs]

    def down(a, cs):
        return functools.reduce(lambda x, y: x + y, [_dot(ai, wd_ref[ks, cs]) for ai, ks in zip(a, ksubs)])

    @pl.when(j == 0)
    def _():
        x1_copy().start()
        a = acts()
        for cs in pieces:
            o_ref[0, :, cs] = down(a, cs)

    @pl.when(jnp.logical_and(j > 0, j < last))
    def _():
        a = acts()
        for cs in pieces:
            o_ref[0, :, cs] += down(a, cs)

    @pl.when(j == last)
    def _():
        x1_copy().wait()
        a = acts()
        ssq = None
        for cs in pieces:
            x2 = x1_buf[:, cs] + mod_ref[0, 5:6, cs] * (o_ref[0, :, cs] + down(a, cs))
            x1_buf[:, cs] = x2
            if final_norm:
                sq = x2 * x2
                for k0 in range(0, FFN_DN, 128):
                    ssq = sq[:, k0:k0 + 128] if ssq is None else ssq + sq[:, k0:k0 + 128]
        if final_norm:
            rs_ref[...] = lax.rsqrt(jnp.sum(ssq, axis=-1, keepdims=True) * (1.0 / D_MODEL) + EPS)
        fw = fw_ref[...]

        def chunk(rows):
            x2 = x1_buf[rows, :]
            o_ref[0, rows, :] = x2 * rs_ref[rows, :] * fw if final_norm else x2

        _for_row_chunks(FFN_TM, chunk)


def _ffn(h2, x1, mod, final_w, wg, wu, wd, final_norm):
    b, s, d = x1.shape
    tm, tf = FFN_TM, FFN_TF
    return pl.pallas_call(
        functools.partial(_ffn_kernel, final_norm=final_norm),
        grid=(b, s // tm, D_FF // tf),
        in_specs=[pl.BlockSpec((1, tm, d), lambda bi, m, j: (bi, m, 0)),
                  pl.BlockSpec(memory_space=pl.ANY),
                  pl.BlockSpec((1, 6, d), lambda bi, m, j: (bi, 0, 0)),
                  pl.BlockSpec((1, d), lambda bi, m, j: (0, 0)),
                  pl.BlockSpec((d, tf), lambda bi, m, j: (0, j)),
                  pl.BlockSpec((d, tf), lambda bi, m, j: (0, j)),
                  pl.BlockSpec((tf, d), lambda bi, m, j: (j, 0))],
        out_specs=pl.BlockSpec((1, tm, d), lambda bi, m, j: (bi, m, 0)),
        out_shape=jax.ShapeDtypeStruct((b, s, d), F32),
        scratch_shapes=[pltpu.VMEM((tm, d), F32), pltpu.VMEM((tm, 1), F32), pltpu.SemaphoreType.DMA(())],
        compiler_params=_params("arbitrary", "arbitrary", "arbitrary"),
        name="ffn",
    )(h2, x1, mod, final_w, wg, wu, wd)


def kernel(x, c, w_ada, b_ada, norm1_w, w_in, lower_bounds, hgrn_norm_w, attn_sinks,
           rel_bias_table, w_out, norm2_w, w_gate, w_up, w_down, final_norm_w):
    b, s, d = x.shape
    assert (d, w_in.shape[-1], w_gate.shape[-1]) == (D_MODEL, IN_WIDTH, D_FF)
    assert s % max(INPROJ_TM, HGRN_TC, OUTPROJ_TM, FFN_TM, ATTN_QB * ATTN_BLOCK) == 0
    depth = w_ada.shape[0]
    c8 = jnp.pad(c, ((0, 8 - b), (0, 0)))
    for layer in range(depth):
        bias, wi16 = _bias_table(rel_bias_table, attn_sinks[layer], w_in[layer])
        mod = _ada(c8, w_ada[layer], b_ada[layer][None, :])[:b].reshape(b, 6, d)
        gate, p16, dec = _inproj(x, mod, norm1_w[layer][None, :], wi16, lower_bounds,
                                 hgrn_norm_w[layer][None, :], layer)
        o_rec, wd16, wo16 = _hgrn(gate, p16, dec, w_down[layer], w_out[layer])
        o_att, wg16, wu16 = _attn(p16, bias, w_gate[layer], w_up[layer])
        x1, h2 = _outproj(o_rec, o_att, x, mod, norm2_w[layer][None, :], wo16)
        x = _ffn(h2, x1, mod, final_norm_w[None, :], wg16, wu16, wd16,
                 final_norm=(layer == depth - 1))
    return x
```

```python
import functools

import numpy as np
import jax
import jax.numpy as jnp
from jax import lax
from jax.experimental import pallas as pl
from jax.experimental.pallas import tpu as pltpu

D_MODEL = 2048
DEPTH = 1
HGRN_WIDTH = 1024
HGRN_HEAD_DIM = 128
HGRN_HEADS = HGRN_WIDTH // HGRN_HEAD_DIM
HGRN_CHUNK = 64
ATTN_WIDTH = 1024
ATTN_HEAD_DIM = 64
ATTN_Q_HEADS = ATTN_WIDTH // ATTN_HEAD_DIM
ATTN_KV_HEADS = 4
ATTN_GROUP = ATTN_Q_HEADS // ATTN_KV_HEADS
WINDOW = 128
ATTN_BLOCK = 128
ATTN_QB = 4
REL_BUCKETS = 32
REL_MAX_DIST = 128
D_FF = 5632
KV_WIDTH = ATTN_KV_HEADS * ATTN_HEAD_DIM
IN_WIDTH = 4 * HGRN_WIDTH + ATTN_WIDTH + 2 * KV_WIDTH
P16_BLOCKS = ("v", "aq", "q_rel", "k_rel", "q_dec", "k_dec")
P16_KV_COL = len(P16_BLOCKS) * HGRN_WIDTH
P16_WIDTH = P16_KV_COL + 2 * KV_WIDTH
EPS = 1e-6
NEG_INF = -1e30
LOG2E = 1.4426950408889634

F32 = jnp.float32
BF16 = jnp.bfloat16

VMEM_LIMIT_BYTES = 56 * 1024 * 1024
BF16_TILE_ROWS = 16

ADA_TN = 1024
INPROJ_TM = 512
INPROJ_TN = 256
HGRN_TC = 512
HGRN_UNROLL = 4
HGRN_ROWS = HGRN_UNROLL * HGRN_CHUNK
HGRN_TRIP_CHUNKS = 8
OUTPROJ_TM = 512
OUTPROJ_TN = 256
FFN_TM = 1024
FFN_TF = 512
FFN_SUB = 256
FFN_DN = 512
ROW_CHUNK = 16
ROW_UNROLL = 8


def _params(*semantics):
    return pltpu.CompilerParams(dimension_semantics=semantics,
                                vmem_limit_bytes=VMEM_LIMIT_BYTES)


def _sigmoid(v):
    return 1.0 / (1.0 + jnp.exp2(v * (-LOG2E)))


def _silu(v):
    return v * _sigmoid(v)


def _dot(a, b):
    return jnp.dot(a, b, preferred_element_type=F32)


def _dot_nt(a, b):
    return lax.dot_general(a, b, (((1,), (1,)), ((), ())), preferred_element_type=F32)


def _dot_tn(a, b):
    return lax.dot_general(a, b, (((0,), (0,)), ((), ())), preferred_element_type=F32)


def _ada_kernel(c_ref, w_ref, b_ref, o_ref):
    c_act = _silu(c_ref[...])
    o_ref[...] = _dot(c_act.astype(BF16), w_ref[...].astype(BF16)) + b_ref[...]


def _ada(c8, w, b):
    n = w.shape[1]
    return pl.pallas_call(
        _ada_kernel,
        grid=(n // ADA_TN,),
        in_specs=[pl.BlockSpec((8, D_MODEL), lambda j: (0, 0)),
                  pl.BlockSpec((D_MODEL, ADA_TN), lambda j: (0, j)),
                  pl.BlockSpec((1, ADA_TN), lambda j: (0, j))],
        out_specs=pl.BlockSpec((8, ADA_TN), lambda j: (0, j)),
        out_shape=jax.ShapeDtypeStruct((8, n), F32),
        compiler_params=_params("arbitrary"),
        name="ada",
    )(c8, w, b)


def _modulated_norm(x, wmod, shift):
    return x * lax.rsqrt(jnp.mean(x * x, axis=-1, keepdims=True) + EPS) * wmod + shift


def _for_row_chunks(n_rows, fn):
    def body(i, carry):
        fn(pl.ds(pl.multiple_of(i * ROW_CHUNK, ROW_CHUNK), ROW_CHUNK))
        return carry

    lax.fori_loop(0, n_rows // ROW_CHUNK, body, 0, unroll=ROW_UNROLL)


def _split3(v):
    hi = v.astype(BF16)
    r1 = v - hi.astype(F32)
    mid = r1.astype(BF16)
    lo = (r1 - mid.astype(F32)).astype(BF16)
    return hi, mid, lo


def _hgrn_tril3():
    tril = np.tril(np.ones((HGRN_CHUNK, HGRN_CHUNK), np.float32))
    return jnp.asarray(np.concatenate([tril, tril, tril], axis=1), dtype=BF16)


def _hgrn_prepare(q, f_logit, g, lb, nw, tril3):
    c, nu = HGRN_CHUNK, HGRN_UNROLL

    def per_chunk_rows(x, offset):
        rid = lax.broadcasted_iota(jnp.int32, x.shape, 0)
        out = jnp.broadcast_to(x[offset:offset + 1, :], x.shape)
        for u in range(1, nu):
            out = jnp.where(rid >= u * c, jnp.broadcast_to(x[u * c + offset:u * c + offset + 1, :], x.shape), out)
        return out

    f = lb + (1.0 - lb) * _sigmoid(f_logit)
    k = 1.0 - f
    parts = _split3(jnp.log2(f))
    b = jnp.concatenate(
        [_dot(tril3, jnp.concatenate([p[u * c:(u + 1) * c] for p in parts], axis=0)) for u in range(nu)],
        axis=0)
    b_mid = per_chunk_rows(b, c // 2 - 1)
    b_last = per_chunk_rows(b, c - 1)
    d_mid = b - b_mid
    qr = q * jnp.exp2(d_mid)
    kr = k * jnp.exp2(-d_mid)
    q_dec = (qr * jnp.exp2(b_mid)).astype(BF16)
    k_dec = (kr * jnp.exp2(b_last - b_mid)).astype(BF16)
    decays = [jnp.exp2(b[(u + 1) * c - 1:(u + 1) * c, :]) for u in range(nu)]
    return qr.astype(BF16), kr.astype(BF16), q_dec, k_dec, nw * _silu(g), decays


def _inproj_kernel(x_ref, mod_ref, nw_ref, w_ref, lb_ref, nwh_ref, tril_ref, gate_ref, o16_ref, dec_ref,
                   h_ref, *, layer):
    shift = mod_ref[0, 0:1, :]
    wmod = nw_ref[...] * (1.0 + mod_ref[0, 1:2, :])

    def chunk(rows):
        h_ref[rows, :] = _modulated_norm(x_ref[0, rows, :], wmod, shift).astype(BF16)

    _for_row_chunks(INPROJ_TM, chunk)

    lb_rows = [lb_ref[r:r + 1, :] for r in range(lb_ref.shape[0])]
    lb_max = functools.reduce(jnp.maximum, lb_rows)
    lb_exp = [jnp.exp(r - lb_max) for r in lb_rows]
    lb_all = sum(lb_exp[:layer + 1]) / sum(lb_exp)

    tn, hw = INPROJ_TN, HGRN_WIDTH
    p16_col = {name: i * hw for i, name in enumerate(P16_BLOCKS)}

    def project(w_col):
        return _dot(h_ref[...], w_ref[:, w_col:w_col + tn])

    def to_p16(name, c0, value):
        o16_ref[0, :, p16_col[name] + c0:p16_col[name] + c0 + tn] = value.astype(BF16)

    for c0 in range(0, hw, tn):
        cs = slice(c0, c0 + tn)
        hq, hf, hg = project(c0), project(hw + c0), project(3 * hw + c0)
        followers = [lambda: to_p16("v", c0, project(2 * hw + c0)),
                     lambda: to_p16("aq", c0, project(4 * hw + c0))]
        for half in range(INPROJ_TM // HGRN_ROWS):
            rows = slice(half * HGRN_ROWS, (half + 1) * HGRN_ROWS)
            q_rel, k_rel, q_dec, k_dec, gate, decays = _hgrn_prepare(
                hq[rows], hf[rows], hg[rows], lb_all[:, cs], nwh_ref[:, cs], tril_ref[...])
            for name, value in (("q_rel", q_rel), ("k_rel", k_rel), ("q_dec", q_dec), ("k_dec", k_dec)):
                o16_ref[0, rows, p16_col[name] + c0:p16_col[name] + c0 + tn] = value
            gate_ref[0, rows, cs] = gate
            for u, decay in enumerate(decays):
                dec_ref[0, half * HGRN_UNROLL + u:half * HGRN_UNROLL + u + 1, cs] = decay
            n_groups = INPROJ_TM // HGRN_ROWS
            if (half + 1) % (n_groups // 2) == 0:
                followers[(half + 1) // (n_groups // 2) - 1]()
    for c0 in range(0, 2 * KV_WIDTH, tn):
        o16_ref[0, :, P16_KV_COL + c0:P16_KV_COL + c0 + tn] = project(4 * hw + ATTN_WIDTH + c0).astype(BF16)


def _inproj(x, mod, norm_w, w_bf, lower_bounds, hgrn_norm_w, layer):
    b, s, d = x.shape
    tm = INPROJ_TM
    assert tm % (2 * HGRN_ROWS) == 0 and (2 * KV_WIDTH) % INPROJ_TN == 0
    chunks = tm // HGRN_CHUNK
    tril3 = _hgrn_tril3()
    return pl.pallas_call(
        functools.partial(_inproj_kernel, layer=layer),
        grid=(b, s // tm),
        in_specs=[pl.BlockSpec((1, tm, d), lambda bi, m: (bi, m, 0)),
                  pl.BlockSpec((1, 6, d), lambda bi, m: (bi, 0, 0)),
                  pl.BlockSpec((1, d), lambda bi, m: (0, 0)),
                  pl.BlockSpec(w_bf.shape, lambda bi, m: (0, 0), pipeline_mode=pl.Buffered(1)),
                  pl.BlockSpec(lower_bounds.shape, lambda bi, m: (0, 0)),
                  pl.BlockSpec((1, HGRN_WIDTH), lambda bi, m: (0, 0)),
                  pl.BlockSpec(tril3.shape, lambda bi, m: (0, 0))],
        out_specs=[pl.BlockSpec((1, tm, HGRN_WIDTH), lambda bi, m: (bi, m, 0)),
                   pl.BlockSpec((1, tm, P16_WIDTH), lambda bi, m: (bi, m, 0)),
                   pl.BlockSpec((1, chunks, HGRN_WIDTH), lambda bi, m: (bi, m, 0))],
        out_shape=[jax.ShapeDtypeStruct((b, s, HGRN_WIDTH), F32),
                   jax.ShapeDtypeStruct((b, s, P16_WIDTH), BF16),
                   jax.ShapeDtypeStruct((b, s // HGRN_CHUNK, HGRN_WIDTH), F32)],
        scratch_shapes=[pltpu.VMEM((tm, d), BF16)],
        compiler_params=_params("parallel", "parallel"),
        name="inproj",
    )(x, mod, norm_w, w_bf, lower_bounds, hgrn_norm_w, tril3)


def _hgrn_kernel(v_ref, qrel_ref, krel_ref, qdec_ref, kdec_ref, gate_ref, dec_ref, wa_ref, wb_ref,
                 o_ref, wa16_ref, wb16_ref, st_ref):
    c, dk, nu = HGRN_CHUNK, HGRN_HEAD_DIM, HGRN_TRIP_CHUNKS
    wa16_ref[...] = wa_ref[...].astype(BF16)
    wb16_ref[...] = wb_ref[...].astype(BF16)

    @pl.when(pl.program_id(1) == 0)
    def _():
        st_ref[...] = jnp.zeros_like(st_ref)

    causal = (lax.broadcasted_iota(jnp.int32, (c, c), 0) >= lax.broadcasted_iota(jnp.int32, (c, c), 1))
    heads = [slice(h * dk, (h + 1) * dk) for h in range(HGRN_HEADS)]

    def body(i, carry):
        st = [st_ref[h] for h in range(HGRN_HEADS)]
        for u in range(nu):
            rows = pl.ds(pl.multiple_of((i * nu + u) * c, c), c)
            decay = dec_ref[0, pl.ds(i * nu + u, 1), :]
            a = [jnp.where(causal, _dot_nt(qrel_ref[0, rows, hs], krel_ref[0, rows, hs]), 0.0).astype(BF16)
                 for hs in heads]
            o = [_dot(a[h], v_ref[0, rows, hs]) + _dot_nt(qdec_ref[0, rows, hs], st[h].astype(BF16))
                 for h, hs in enumerate(heads)]
            st = [decay[:, hs] * st[h] + _dot_tn(v_ref[0, rows, hs], kdec_ref[0, rows, hs])
                  for h, hs in enumerate(heads)]
            for h, hs in enumerate(heads):
                oh = o[h] * lax.rsqrt(jnp.mean(o[h] * o[h], axis=-1, keepdims=True) + EPS)
                o_ref[0, rows, hs] = (oh * gate_ref[0, rows, hs]).astype(o_ref.dtype)
        for h in range(HGRN_HEADS):
            st_ref[h] = st[h]
        return carry

    lax.fori_loop(0, HGRN_TC // (nu * c), body, 0)


def _hgrn(gate, p16, dec, wa, wb):
    b, s, _ = gate.shape
    nt = s // HGRN_TC
    blk = lambda name: pl.BlockSpec((1, HGRN_TC, HGRN_WIDTH),
                                    lambda bi, t, j=P16_BLOCKS.index(name): (bi, t, j))
    wa_spec, wa_shape = _cast_rider_specs(wa, b * nt, nt)
    wb_spec, wb_shape = _cast_rider_specs(wb, b * nt, nt)
    return pl.pallas_call(
        _hgrn_kernel,
        grid=(b, nt),
        in_specs=[blk("v"), blk("q_rel"), blk("k_rel"), blk("q_dec"), blk("k_dec"),
                  pl.BlockSpec((1, HGRN_TC, HGRN_WIDTH), lambda bi, t: (bi, t, 0)),
                  pl.BlockSpec((1, HGRN_TC // HGRN_CHUNK, HGRN_WIDTH), lambda bi, t: (bi, t, 0)),
                  wa_spec, wb_spec],
        out_specs=[pl.BlockSpec((1, HGRN_TC, HGRN_WIDTH), lambda bi, t: (bi, t, 0)), wa_spec, wb_spec],
        out_shape=[jax.ShapeDtypeStruct((b, s, HGRN_WIDTH), BF16), wa_shape, wb_shape],
        scratch_shapes=[pltpu.VMEM((HGRN_HEADS, HGRN_HEAD_DIM, HGRN_HEAD_DIM), F32)],
        compiler_params=_params("arbitrary", "arbitrary"),
        name="hgrn",
    )(p16, p16, p16, p16, p16, gate, dec, wa, wb)


def _t5_causal_buckets(dist):
    max_exact = REL_BUCKETS // 2
    d = np.maximum(dist, 0)
    log_b = max_exact + (np.log(np.maximum(d, 1) / max_exact)
                         / np.log(REL_MAX_DIST / max_exact)
                         * (REL_BUCKETS - max_exact)).astype(np.int32)
    log_b = np.minimum(log_b, REL_BUCKETS - 1)
    return np.where(d < max_exact, d, log_b).astype(np.int32)


def _bias_kernel(tab_ref, sink_ref, bucket_ref, valid_ref, w_ref, o_ref, w16_ref):
    w16_ref[...] = w_ref[...].astype(BF16)
    h = pl.program_id(0)
    bucket = bucket_ref[...]
    acc = jnp.zeros(bucket.shape, F32)
    for bk in range(REL_BUCKETS):
        acc = jnp.where(bucket == bk, tab_ref[bk, h], acc)
    sink_col = lax.broadcasted_iota(jnp.int32, bucket.shape, 1) == 0
    for variant in range(2):
        masked = jnp.where(valid_ref[variant] != 0, acc, NEG_INF)
        o_ref[variant, 0] = jnp.where(sink_col, sink_ref[h], masked)


def _bias_table(rel_table, sinks, w):
    l = ATTN_BLOCK
    qi = np.arange(l)[:, None]
    kj = np.arange(2 * l)[None, :]
    dist = qi + l - kj
    in_window = (dist >= 0) & (dist < WINDOW)
    valid = np.stack([in_window & (kj >= l), in_window]).astype(np.int32)
    assert not valid[:, :, 0].any()
    bucket = _t5_causal_buckets(dist)
    w_rows = w.shape[0] // ATTN_Q_HEADS
    assert w_rows * ATTN_Q_HEADS == w.shape[0] and w_rows % BF16_TILE_ROWS == 0
    w_spec = pl.BlockSpec((w_rows, w.shape[1]), lambda h: (h, 0))
    return pl.pallas_call(
        _bias_kernel,
        grid=(ATTN_Q_HEADS,),
        in_specs=[pl.BlockSpec(memory_space=pltpu.SMEM),
                  pl.BlockSpec(memory_space=pltpu.SMEM),
                  pl.BlockSpec((l, 2 * l), lambda h: (0, 0)),
                  pl.BlockSpec((2, l, 2 * l), lambda h: (0, 0, 0)),
                  w_spec],
        out_specs=[pl.BlockSpec((2, 1, l, 2 * l), lambda h: (0, h, 0, 0)), w_spec],
        out_shape=[jax.ShapeDtypeStruct((2, ATTN_Q_HEADS, l, 2 * l), F32),
                   jax.ShapeDtypeStruct(w.shape, BF16)],
        compiler_params=_params("arbitrary"),
        name="bias",
    )(rel_table, sinks, jnp.asarray(bucket), jnp.asarray(valid), w)


def _zero_first_row(a):
    top = a[:BF16_TILE_ROWS]
    row = lax.broadcasted_iota(jnp.int32, top.shape, 0)
    return jnp.concatenate([jnp.where(row == 0, 0.0, top).astype(a.dtype), a[BF16_TILE_ROWS:]], axis=0)


def _attn_kernel(q_ref, kp_ref, kc_ref, vp_ref, vc_ref, bias_ref, wa_ref, wb_ref,
                 o_ref, wa16_ref, wb16_ref):
    l = ATTN_BLOCK
    dh = ATTN_HEAD_DIM
    wa16_ref[...] = wa_ref[...].astype(BF16)
    wb16_ref[...] = wb_ref[...].astype(BF16)
    lane = lax.broadcasted_iota(jnp.int32, (1, 2 * dh), 1)
    low = lane < dh
    first = jnp.minimum(pl.program_id(1), 1)

    for pair in range(ATTN_KV_HEADS // 2):
        pc = slice(pair * 2 * dh, (pair + 1) * 2 * dh)
        kk = pltpu.bitcast(jnp.concatenate([kp_ref[0, :, pc], kc_ref[0, :, pc]], axis=0), jnp.int32)
        vv = pltpu.bitcast(jnp.concatenate([vp_ref[0, :, pc], vc_ref[0, :, pc]], axis=0), jnp.int32)
        kk_sw = pltpu.roll(kk, dh, axis=1)
        vv_sw = pltpu.roll(vv, dh, axis=1)
        for sub in range(2):
            kvh = pair * 2 + sub
            if sub == 0:
                k2_all = jnp.where(low, kk, kk_sw)
                v2_all = jnp.where(low, vv, vv_sw)
            else:
                k2_all = jnp.where(low, kk_sw, kk)
                v2_all = jnp.where(low, vv_sw, vv)
            k2_all = pltpu.bitcast(k2_all, BF16)
            v2_all = pltpu.bitcast(v2_all, BF16)
            for t in range(ATTN_QB):
                qrows = slice(t * l, (t + 1) * l)
                k2 = _zero_first_row(k2_all[t * l:(t + 2) * l])
                v2 = _zero_first_row(v2_all[t * l:(t + 2) * l])
                qs = []
                for j in range(ATTN_GROUP):
                    hq = kvh * ATTN_GROUP + j
                    qc = slice((hq // 2) * 2 * dh, (hq // 2 + 1) * 2 * dh)
                    q2 = q_ref[0, qrows, qc] * (dh ** -0.5)
                    keep = low if hq % 2 == 0 else jnp.logical_not(low)
                    qs.append(jnp.where(keep, q2, 0.0).astype(BF16))
                s_all = _dot_nt(jnp.concatenate(qs, axis=0), k2)
                ps = []
                for j in range(ATTN_GROUP):
                    hq = kvh * ATTN_GROUP + j
                    bias = bias_ref[first, hq] if t == 0 else bias_ref[1, hq]
                    s = s_all[j * l:(j + 1) * l] + bias
                    p = jnp.exp(s - jnp.max(s, axis=-1, keepdims=True))
                    ps.append(p.astype(BF16))
                v2x = jnp.concatenate([v2, jnp.ones_like(v2)], axis=1)
                o_all = _dot(jnp.concatenate(ps, axis=0), v2x)
                o_all = o_all[:, :2 * dh] / o_all[:, 2 * dh:]
                for jp in range(ATTN_GROUP // 2):
                    hq = kvh * ATTN_GROUP + 2 * jp
                    oe = o_all[(2 * jp) * l:(2 * jp + 1) * l]
                    oo = o_all[(2 * jp + 1) * l:(2 * jp + 2) * l]
                    oc = slice((hq // 2) * 2 * dh, (hq // 2 + 1) * 2 * dh)
                    o_ref[0, qrows, oc] = jnp.where(low, oe, oo).astype(o_ref.dtype)


def _cast_rider_specs(w, steps, steps_per_batch):
    rows = w.shape[0] // steps
    assert rows * steps == w.shape[0] and rows % 8 == 0
    spec = pl.BlockSpec((rows, w.shape[1]), lambda bi, n: (bi * steps_per_batch + n, 0))
    return spec, jax.ShapeDtypeStruct(w.shape, BF16)


def _attn(p16, bias, wa, wb):
    b, s, _ = p16.shape
    l = ATTN_BLOCK
    tq = ATTN_QB * l
    nq = s // tq
    q_blk = P16_BLOCKS.index("aq")
    k_blk = P16_KV_COL // KV_WIDTH
    v_blk = k_blk + 1
    prev_spec = lambda blk: pl.BlockSpec(
        (1, l, KV_WIDTH), lambda bi, n: (bi, jnp.maximum(n * ATTN_QB - 1, 0), blk))
    cur_spec = lambda blk: pl.BlockSpec((1, tq, KV_WIDTH), lambda bi, n: (bi, n, blk))
    wa_spec, wa_shape = _cast_rider_specs(wa, b * nq, nq)
    wb_spec, wb_shape = _cast_rider_specs(wb, b * nq, nq)
    return pl.pallas_call(
        _attn_kernel,
        grid=(b, nq),
        in_specs=[pl.BlockSpec((1, tq, ATTN_WIDTH), lambda bi, n: (bi, n, q_blk)),
                  prev_spec(k_blk), cur_spec(k_blk), prev_spec(v_blk), cur_spec(v_blk),
                  pl.BlockSpec(bias.shape, lambda bi, n: (0, 0, 0, 0), pipeline_mode=pl.Buffered(1)),
                  wa_spec, wb_spec],
        out_specs=[pl.BlockSpec((1, tq, ATTN_WIDTH), lambda bi, n: (bi, n, 0)), wa_spec, wb_spec],
        out_shape=[jax.ShapeDtypeStruct((b, s, ATTN_WIDTH), BF16), wa_shape, wb_shape],
        compiler_params=_params("arbitrary", "arbitrary"),
        name="attn",
    )(p16, p16, p16, p16, p16, bias, wa, wb)


def _outproj_kernel(orec_ref, oatt_ref, x_ref, mod_ref, nw_ref, w_ref, x1_ref, h2_ref, rs_ref):
    ssq = None
    for c0 in range(0, D_MODEL, OUTPROJ_TN):
        cs = slice(c0, c0 + OUTPROJ_TN)
        y = _dot(orec_ref[0], w_ref[0:HGRN_WIDTH, cs]) + _dot(oatt_ref[0], w_ref[HGRN_WIDTH:, cs])
        x1 = x_ref[0, :, cs] + mod_ref[0, 2:3, cs] * y
        x1_ref[0, :, cs] = x1
        sq = x1 * x1
        for k0 in range(0, OUTPROJ_TN, 128):
            ssq = sq[:, k0:k0 + 128] if ssq is None else ssq + sq[:, k0:k0 + 128]
    rs_ref[...] = lax.rsqrt(jnp.sum(ssq, axis=-1, keepdims=True) * (1.0 / D_MODEL) + EPS)

    shift = mod_ref[0, 3:4, :]
    wmod = nw_ref[...] * (1.0 + mod_ref[0, 4:5, :])

    def chunk(rows):
        h2_ref[0, rows, :] = (x1_ref[0, rows, :] * rs_ref[rows, :] * wmod + shift).astype(BF16)

    _for_row_chunks(OUTPROJ_TM, chunk)


def _outproj(o_rec, o_att, x, mod, norm_w, w_bf):
    b, s, d = x.shape
    tm = OUTPROJ_TM
    return pl.pallas_call(
        _outproj_kernel,
        grid=(b, s // tm),
        in_specs=[pl.BlockSpec((1, tm, HGRN_WIDTH), lambda bi, m: (bi, m, 0)),
                  pl.BlockSpec((1, tm, ATTN_WIDTH), lambda bi, m: (bi, m, 0)),
                  pl.BlockSpec((1, tm, d), lambda bi, m: (bi, m, 0)),
                  pl.BlockSpec((1, 6, d), lambda bi, m: (bi, 0, 0)),
                  pl.BlockSpec((1, d), lambda bi, m: (0, 0)),
                  pl.BlockSpec(w_bf.shape, lambda bi, m: (0, 0), pipeline_mode=pl.Buffered(1))],
        out_specs=[pl.BlockSpec((1, tm, d), lambda bi, m: (bi, m, 0)),
                   pl.BlockSpec((1, tm, d), lambda bi, m: (bi, m, 0))],
        out_shape=[jax.ShapeDtypeStruct((b, s, d), F32),
                   jax.ShapeDtypeStruct((b, s, d), BF16)],
        scratch_shapes=[pltpu.VMEM((tm, 1), F32)],
        compiler_params=_params("parallel", "parallel"),
        name="outproj",
    )(o_rec, o_att, x, mod, norm_w, w_bf)


def _ffn_kernel(h_ref, x1_hbm, mod_ref, fw_ref, wg_ref, wu_ref, wd_ref, o_ref, x1_buf, rs_ref, x1_sem,
                *, final_norm):
    bi, m, j = pl.program_id(0), pl.program_id(1), pl.program_id(2)

    def x1_copy():
        rows = pl.ds(pl.multiple_of(m * FFN_TM, FFN_TM), FFN_TM)
        return pltpu.make_async_copy(x1_hbm.at[bi, rows, :], x1_buf, x1_sem)

    last = pl.num_programs(2) - 1
    ksubs = [slice(k0, k0 + FFN_SUB) for k0 in range(0, FFN_TF, FFN_SUB)]
    pieces = [slice(c0, c0 + FFN_DN) for c0 in range(0, D_MODEL, FFN_DN)]

    def acts():
        return [(_silu(_dot(h_ref[0], wg_ref[:, ks])) * _dot(h_ref[0], wu_ref[:, ks])).astype(BF16)
                for ks in ksubs]

    def down(a, cs):
        return functools.reduce(lambda x, y: x + y, [_dot(ai, wd_ref[ks, cs]) for ai, ks in zip(a, ksubs)])

    @pl.when(j == 0)
    def _():
        x1_copy().start()
        a = acts()
        for cs in pieces:
            o_ref[0, :, cs] = down(a, cs)

    @pl.when(jnp.logical_and(j > 0, j < last))
    def _():
        a = acts()
        for cs in pieces:
            o_ref[0, :, cs] += down(a, cs)

    @pl.when(j == last)
    def _():
        x1_copy().wait()
        a = acts()
        ssq = None
        for cs in pieces:
            x2 = x1_buf[:, cs] + mod_ref[0, 5:6, cs] * (o_ref[0, :, cs] + down(a, cs))
            x1_buf[:, cs] = x2
            if final_norm:
                sq = x2 * x2
                for k0 in range(0, FFN_DN, 128):
                    ssq = sq[:, k0:k0 + 128] if ssq is None else ssq + sq[:, k0:k0 + 128]
        if final_norm:
            rs_ref[...] = lax.rsqrt(jnp.sum(ssq, axis=-1, keepdims=True) * (1.0 / D_MODEL) + EPS)
        fw = fw_ref[...]

        def chunk(rows):
            x2 = x1_buf[rows, :]
            o_ref[0, rows, :] = x2 * rs_ref[rows, :] * fw if final_norm else x2

        _for_row_chunks(FFN_TM, chunk)


def _ffn(h2, x1, mod, final_w, wg, wu, wd, final_norm):
    b, s, d = x1.shape
    tm, tf = FFN_TM, FFN_TF
    return pl.pallas_call(
        functools.partial(_ffn_kernel, final_norm=final_norm),
        grid=(b, s // tm, D_FF // tf),
        in_specs=[pl.BlockSpec((1, tm, d), lambda bi, m, j: (bi, m, 0)),
                  pl.BlockSpec(memory_space=pl.ANY),
                  pl.BlockSpec((1, 6, d), lambda bi, m, j: (bi, 0, 0)),
                  pl.BlockSpec((1, d), lambda bi, m, j: (0, 0)),
                  pl.BlockSpec((d, tf), lambda bi, m, j: (0, j)),
                  pl.BlockSpec((d, tf), lambda bi, m, j: (0, j)),
                  pl.BlockSpec((tf, d), lambda bi, m, j: (j, 0))],
        out_specs=pl.BlockSpec((1, tm, d), lambda bi, m, j: (bi, m, 0)),
        out_shape=jax.ShapeDtypeStruct((b, s, d), F32),
        scratch_shapes=[pltpu.VMEM((tm, d), F32), pltpu.VMEM((tm, 1), F32), pltpu.SemaphoreType.DMA(())],
        compiler_params=_params("arbitrary", "arbitrary", "arbitrary"),
        name="ffn",
    )(h2, x1, mod, final_w, wg, wu, wd)


def kernel(x, c, w_ada, b_ada, norm1_w, w_in, lower_bounds, hgrn_norm_w, attn_sinks,
           rel_bias_table, w_out, norm2_w, w_gate, w_up, w_down, final_norm_w):
    b, s, d = x.shape
    assert (d, w_in.shape[-1], w_gate.shape[-1]) == (D_MODEL, IN_WIDTH, D_FF)
    assert s % max(INPROJ_TM, HGRN_TC, OUTPROJ_TM, FFN_TM, ATTN_QB * ATTN_BLOCK) == 0
    depth = w_ada.shape[0]
    c8 = jnp.pad(c, ((0, 8 - b), (0, 0)))
    for layer in range(depth):
        bias, wi16 = _bias_table(rel_bias_table, attn_sinks[layer], w_in[layer])
        mod = _ada(c8, w_ada[layer], b_ada[layer][None, :])[:b].reshape(b, 6, d)
        gate, p16, dec = _inproj(x, mod, norm1_w[layer][None, :], wi16, lower_bounds,
                                 hgrn_norm_w[layer][None, :], layer)
        o_rec, wd16, wo16 = _hgrn(gate, p16, dec, w_down[layer], w_out[layer])
        o_att, wg16, wu16 = _attn(p16, bias, w_gate[layer], w_up[layer])
        x1, h2 = _outproj(o_rec, o_att, x, mod, norm2_w[layer][None, :], wo16)
        x = _ffn(h2, x1, mod, final_norm_w[None, :], wg16, wu16, wd16,
                 final_norm=(layer == depth - 1))
    return x
```

```python
import functools

import numpy as np
import jax
import jax.numpy as jnp
from jax import lax
from jax.experimental import pallas as pl
from jax.experimental.pallas import tpu as pltpu

D_MODEL = 2048
HGRN_WIDTH = 1024
HGRN_HEAD_DIM = 128
HGRN_HEADS = HGRN_WIDTH // HGRN_HEAD_DIM
HGRN_CHUNK = 64
ATTN_WIDTH = 1024
ATTN_HEAD_DIM = 64
ATTN_Q_HEADS = ATTN_WIDTH // ATTN_HEAD_DIM
ATTN_KV_HEADS = 4
ATTN_GROUP = ATTN_Q_HEADS // ATTN_KV_HEADS
WINDOW = 128
ATTN_BLOCK = 128
ATTN_QB = 4
REL_BUCKETS = 32
REL_MAX_DIST = 128
D_FF = 5632
KV_WIDTH = ATTN_KV_HEADS * ATTN_HEAD_DIM
IN_WIDTH = 4 * HGRN_WIDTH + ATTN_WIDTH + 2 * KV_WIDTH
P16_BLOCKS = ("v", "aq", "q_rel", "k_rel", "q_dec", "k_dec")
P16_KV_COL = len(P16_BLOCKS) * HGRN_WIDTH
P16_WIDTH = P16_KV_COL + 2 * KV_WIDTH
EPS = 1e-6
NEG_INF = -1e30
LOG2E = 1.4426950408889634

F32 = jnp.float32
BF16 = jnp.bfloat16

VMEM_LIMIT_BYTES = 56 * 1024 * 1024
LANES = 128
SUBLANES = 8
BF16_TILE_ROWS = 2 * SUBLANES

ADA_TN = 1024
INPROJ_TM = 512
INPROJ_TN = 256
HGRN_TC = 1024
HGRN_UNROLL = 4
HGRN_ROWS = HGRN_UNROLL * HGRN_CHUNK
HGRN_TRIP_CHUNKS = 8
OUTPROJ_TM = 512
OUTPROJ_TN = 256
FFN_TM = 1024
FFN_TF = 512
FFN_SUB = 256
FFN_DN = 512
ROW_CHUNK = 16
ROW_UNROLL = 8


def _params(*semantics):
    return pltpu.CompilerParams(dimension_semantics=semantics,
                                vmem_limit_bytes=VMEM_LIMIT_BYTES)


def _sigmoid(v):
    return 1.0 / (1.0 + jnp.exp2(v * (-LOG2E)))


def _silu(v):
    return v * _sigmoid(v)


def _dot(a, b):
    return jnp.dot(a, b, preferred_element_type=F32)


def _dot_nt(a, b):
    return lax.dot_general(a, b, (((1,), (1,)), ((), ())), preferred_element_type=F32)


def _dot_tn(a, b):
    return lax.dot_general(a, b, (((0,), (0,)), ((), ())), preferred_element_type=F32)


def _ada_kernel(c_ref, w_ref, b_ref, o_ref):
    c_act = _silu(c_ref[...])
    o_ref[...] = _dot(c_act.astype(BF16), w_ref[...].astype(BF16)) + b_ref[...]


def _ada(c8, w, b):
    n = w.shape[1]
    return pl.pallas_call(
        _ada_kernel,
        grid=(n // ADA_TN,),
        in_specs=[pl.BlockSpec((8, D_MODEL), lambda j: (0, 0)),
                  pl.BlockSpec((D_MODEL, ADA_TN), lambda j: (0, j)),
                  pl.BlockSpec((1, ADA_TN), lambda j: (0, j))],
        out_specs=pl.BlockSpec((8, ADA_TN), lambda j: (0, j)),
        out_shape=jax.ShapeDtypeStruct((8, n), F32),
        compiler_params=_params("arbitrary"),
        name="ada",
    )(c8, w, b)


def _modulated_norm(x, wmod, shift):
    return x * lax.rsqrt(jnp.mean(x * x, axis=-1, keepdims=True) + EPS) * wmod + shift


def _for_row_chunks(n_rows, fn):
    def body(i, carry):
        fn(pl.ds(pl.multiple_of(i * ROW_CHUNK, ROW_CHUNK), ROW_CHUNK))
        return carry

    lax.fori_loop(0, n_rows // ROW_CHUNK, body, 0, unroll=ROW_UNROLL)


def _lane_folded_sumsq(ssq, v):
    sq = v * v
    for k0 in range(0, v.shape[1], LANES):
        ssq = sq[:, k0:k0 + LANES] if ssq is None else ssq + sq[:, k0:k0 + LANES]
    return ssq


def _rsqrt_mean(ssq):
    return lax.rsqrt(jnp.sum(ssq, axis=-1, keepdims=True) * (1.0 / D_MODEL) + EPS)


def _split3(v):
    hi = v.astype(BF16)
    r1 = v - hi.astype(F32)
    mid = r1.astype(BF16)
    lo = (r1 - mid.astype(F32)).astype(BF16)
    return hi, mid, lo


def _hgrn_tril3():
    tril = np.tril(np.ones((HGRN_CHUNK, HGRN_CHUNK), np.float32))
    return jnp.asarray(np.concatenate([tril, tril, tril], axis=1), dtype=BF16)


def _hgrn_prepare(q, f_logit, g, lb, nw, tril3):
    c, nu = HGRN_CHUNK, HGRN_UNROLL

    def per_chunk_rows(x, offset):
        rid = lax.broadcasted_iota(jnp.int32, x.shape, 0)
        out = jnp.broadcast_to(x[offset:offset + 1, :], x.shape)
        for u in range(1, nu):
            out = jnp.where(rid >= u * c, jnp.broadcast_to(x[u * c + offset:u * c + offset + 1, :], x.shape), out)
        return out

    f = lb + (1.0 - lb) * _sigmoid(f_logit)
    k = 1.0 - f
    parts = _split3(jnp.log2(f))
    b = jnp.concatenate(
        [_dot(tril3, jnp.concatenate([p[u * c:(u + 1) * c] for p in parts], axis=0)) for u in range(nu)],
        axis=0)
    b_mid = per_chunk_rows(b, c // 2 - 1)
    b_last = per_chunk_rows(b, c - 1)
    d_mid = b - b_mid
    qr = q * jnp.exp2(d_mid)
    kr = k * jnp.exp2(-d_mid)
    q_dec = (qr * jnp.exp2(b_mid)).astype(BF16)
    k_dec = (kr * jnp.exp2(b_last - b_mid)).astype(BF16)
    decays = [jnp.exp2(b[(u + 1) * c - 1:(u + 1) * c, :]) for u in range(nu)]
    return qr.astype(BF16), kr.astype(BF16), q_dec, k_dec, nw * _silu(g), decays


def _inproj_kernel(x_ref, mod_ref, nw_ref, w_ref, lb_ref, nwh_ref, tril_ref, gate_ref, o16_ref, dec_ref,
                   h_ref, *, layer):
    shift = mod_ref[0, 0:1, :]
    wmod = nw_ref[...] * (1.0 + mod_ref[0, 1:2, :])

    def chunk(rows):
        h_ref[rows, :] = _modulated_norm(x_ref[0, rows, :], wmod, shift).astype(BF16)

    _for_row_chunks(INPROJ_TM, chunk)

    lb_rows = [lb_ref[r:r + 1, :] for r in range(lb_ref.shape[0])]
    lb_max = functools.reduce(jnp.maximum, lb_rows)
    lb_exp = [jnp.exp(r - lb_max) for r in lb_rows]
    lb_all = sum(lb_exp[:layer + 1]) / sum(lb_exp)

    tn, hw = INPROJ_TN, HGRN_WIDTH
    p16_col = {name: i * hw for i, name in enumerate(P16_BLOCKS)}

    def project(w_col):
        return _dot(h_ref[...], w_ref[:, w_col:w_col + tn])

    def to_p16(name, c0, value):
        o16_ref[0, :, p16_col[name] + c0:p16_col[name] + c0 + tn] = value.astype(BF16)

    for c0 in range(0, hw, tn):
        cs = slice(c0, c0 + tn)
        hq, hf, hg = project(c0), project(hw + c0), project(3 * hw + c0)
        followers = [lambda: to_p16("v", c0, project(2 * hw + c0)),
                     lambda: to_p16("aq", c0, project(4 * hw + c0))]
        for half in range(INPROJ_TM // HGRN_ROWS):
            rows = slice(half * HGRN_ROWS, (half + 1) * HGRN_ROWS)
            q_rel, k_rel, q_dec, k_dec, gate, decays = _hgrn_prepare(
                hq[rows], hf[rows], hg[rows], lb_all[:, cs], nwh_ref[:, cs], tril_ref[...])
            for name, value in (("q_rel", q_rel), ("k_rel", k_rel), ("q_dec", q_dec), ("k_dec", k_dec)):
                o16_ref[0, rows, p16_col[name] + c0:p16_col[name] + c0 + tn] = value
            gate_ref[0, rows, cs] = gate
            for u, decay in enumerate(decays):
                dec_ref[0, half * HGRN_UNROLL + u:half * HGRN_UNROLL + u + 1, cs] = decay
            n_groups = INPROJ_TM // HGRN_ROWS
            if (half + 1) % (n_groups // 2) == 0:
                followers[(half + 1) // (n_groups // 2) - 1]()
    for c0 in range(0, 2 * KV_WIDTH, tn):
        o16_ref[0, :, P16_KV_COL + c0:P16_KV_COL + c0 + tn] = project(4 * hw + ATTN_WIDTH + c0).astype(BF16)


def _inproj(x, mod, norm_w, w_bf, lower_bounds, hgrn_norm_w, layer):
    b, s, d = x.shape
    tm = INPROJ_TM
    assert tm % (2 * HGRN_ROWS) == 0 and (2 * KV_WIDTH) % INPROJ_TN == 0
    chunks = tm // HGRN_CHUNK
    tril3 = _hgrn_tril3()
    return pl.pallas_call(
        functools.partial(_inproj_kernel, layer=layer),
        grid=(b, s // tm),
        in_specs=[pl.BlockSpec((1, tm, d), lambda bi, m: (bi, m, 0)),
                  pl.BlockSpec((1, 6, d), lambda bi, m: (bi, 0, 0)),
                  pl.BlockSpec((1, d), lambda bi, m: (0, 0)),
                  pl.BlockSpec(w_bf.shape, lambda bi, m: (0, 0), pipeline_mode=pl.Buffered(1)),
                  pl.BlockSpec(lower_bounds.shape, lambda bi, m: (0, 0)),
                  pl.BlockSpec((1, HGRN_WIDTH), lambda bi, m: (0, 0)),
                  pl.BlockSpec(tril3.shape, lambda bi, m: (0, 0))],
        out_specs=[pl.BlockSpec((1, tm, HGRN_WIDTH), lambda bi, m: (bi, m, 0)),
                   pl.BlockSpec((1, tm, P16_WIDTH), lambda bi, m: (bi, m, 0)),
                   pl.BlockSpec((1, chunks, HGRN_WIDTH), lambda bi, m: (bi, m, 0))],
        out_shape=[jax.ShapeDtypeStruct((b, s, HGRN_WIDTH), F32),
                   jax.ShapeDtypeStruct((b, s, P16_WIDTH), BF16),
                   jax.ShapeDtypeStruct((b, s // HGRN_CHUNK, HGRN_WIDTH), F32)],
        scratch_shapes=[pltpu.VMEM((tm, d), BF16)],
        compiler_params=_params("parallel", "parallel"),
        name="inproj",
    )(x, mod, norm_w, w_bf, lower_bounds, hgrn_norm_w, tril3)


def _hgrn_kernel(v_ref, qrel_ref, krel_ref, qdec_ref, kdec_ref, gate_ref, dec_ref, wa_ref, wb_ref,
                 o_ref, wa16_ref, wb16_ref, st_ref):
    c, dk, nu = HGRN_CHUNK, HGRN_HEAD_DIM, HGRN_TRIP_CHUNKS
    wa16_ref[...] = wa_ref[...].astype(BF16)
    wb16_ref[...] = wb_ref[...].astype(BF16)

    @pl.when(pl.program_id(1) == 0)
    def _():
        st_ref[...] = jnp.zeros_like(st_ref)

    causal = (lax.broadcasted_iota(jnp.int32, (c, c), 0) >= lax.broadcasted_iota(jnp.int32, (c, c), 1))
    heads = [slice(h * dk, (h + 1) * dk) for h in range(HGRN_HEADS)]

    def body(i, carry):
        st = [st_ref[h] for h in range(HGRN_HEADS)]
        for u in range(nu):
            rows = pl.ds(pl.multiple_of((i * nu + u) * c, c), c)
            decay = dec_ref[0, pl.ds(i * nu + u, 1), :]
            a = [jnp.where(causal, _dot_nt(qrel_ref[0, rows, hs], krel_ref[0, rows, hs]), 0.0).astype(BF16)
                 for hs in heads]
            o = [_dot(a[h], v_ref[0, rows, hs]) + _dot_nt(qdec_ref[0, rows, hs], st[h].astype(BF16))
                 for h, hs in enumerate(heads)]
            st = [decay[:, hs] * st[h] + _dot_tn(v_ref[0, rows, hs], kdec_ref[0, rows, hs])
                  for h, hs in enumerate(heads)]
            for h, hs in enumerate(heads):
                oh = o[h] * lax.rsqrt(jnp.mean(o[h] * o[h], axis=-1, keepdims=True) + EPS)
                o_ref[0, rows, hs] = (oh * gate_ref[0, rows, hs]).astype(o_ref.dtype)
        for h in range(HGRN_HEADS):
            st_ref[h] = st[h]
        return carry

    lax.fori_loop(0, HGRN_TC // (nu * c), body, 0)


def _hgrn(gate, p16, dec, wa, wb):
    b, s, _ = gate.shape
    nt = s // HGRN_TC
    blk = lambda name: pl.BlockSpec((1, HGRN_TC, HGRN_WIDTH),
                                    lambda bi, t, j=P16_BLOCKS.index(name): (bi, t, j))
    wa_spec, wa_shape = _cast_rider_specs(wa, b * nt, nt)
    wb_spec, wb_shape = _cast_rider_specs(wb, b * nt, nt)
    return pl.pallas_call(
        _hgrn_kernel,
        grid=(b, nt),
        in_specs=[blk("v"), blk("q_rel"), blk("k_rel"), blk("q_dec"), blk("k_dec"),
                  pl.BlockSpec((1, HGRN_TC, HGRN_WIDTH), lambda bi, t: (bi, t, 0)),
                  pl.BlockSpec((1, HGRN_TC // HGRN_CHUNK, HGRN_WIDTH), lambda bi, t: (bi, t, 0)),
                  wa_spec, wb_spec],
        out_specs=[pl.BlockSpec((1, HGRN_TC, HGRN_WIDTH), lambda bi, t: (bi, t, 0)), wa_spec, wb_spec],
        out_shape=[jax.ShapeDtypeStruct((b, s, HGRN_WIDTH), BF16), wa_shape, wb_shape],
        scratch_shapes=[pltpu.VMEM((HGRN_HEADS, HGRN_HEAD_DIM, HGRN_HEAD_DIM), F32)],
        compiler_params=_params("arbitrary", "arbitrary"),
        name="hgrn",
    )(p16, p16, p16, p16, p16, gate, dec, wa, wb)


def _t5_causal_buckets(dist):
    max_exact = REL_BUCKETS // 2
    d = np.maximum(dist, 0)
    log_b = max_exact + (np.log(np.maximum(d, 1) / max_exact)
                         / np.log(REL_MAX_DIST / max_exact)
                         * (REL_BUCKETS - max_exact)).astype(np.int32)
    log_b = np.minimum(log_b, REL_BUCKETS - 1)
    return np.where(d < max_exact, d, log_b).astype(np.int32)


def _bias_kernel(tab_ref, sink_ref, bucket_ref, valid_ref, w_ref, o_ref, w16_ref):
    w16_ref[...] = w_ref[...].astype(BF16)
    h = pl.program_id(0)
    bucket = bucket_ref[...]
    acc = jnp.zeros(bucket.shape, F32)
    for bk in range(REL_BUCKETS):
        acc = jnp.where(bucket == bk, tab_ref[bk, h], acc)
    sink_col = lax.broadcasted_iota(jnp.int32, bucket.shape, 1) == 0
    for variant in range(2):
        masked = jnp.where(valid_ref[variant] != 0, acc, NEG_INF)
        o_ref[variant, 0] = jnp.where(sink_col, sink_ref[h], masked)


def _bias_table(rel_table, sinks, w):
    l = ATTN_BLOCK
    qi = np.arange(l)[:, None]
    kj = np.arange(2 * l)[None, :]
    dist = qi + l - kj
    in_window = (dist >= 0) & (dist < WINDOW)
    valid = np.stack([in_window & (kj >= l), in_window]).astype(np.int32)
    assert not valid[:, :, 0].any()
    bucket = _t5_causal_buckets(dist)
    w_rows = w.shape[0] // ATTN_Q_HEADS
    assert w_rows * ATTN_Q_HEADS == w.shape[0] and w_rows % BF16_TILE_ROWS == 0
    w_spec = pl.BlockSpec((w_rows, w.shape[1]), lambda h: (h, 0))
    return pl.pallas_call(
        _bias_kernel,
        grid=(ATTN_Q_HEADS,),
        in_specs=[pl.BlockSpec(memory_space=pltpu.SMEM),
                  pl.BlockSpec(memory_space=pltpu.SMEM),
                  pl.BlockSpec((l, 2 * l), lambda h: (0, 0)),
                  pl.BlockSpec((2, l, 2 * l), lambda h: (0, 0, 0)),
                  w_spec],
        out_specs=[pl.BlockSpec((2, 1, l, 2 * l), lambda h: (0, h, 0, 0)), w_spec],
        out_shape=[jax.ShapeDtypeStruct((2, ATTN_Q_HEADS, l, 2 * l), F32),
                   jax.ShapeDtypeStruct(w.shape, BF16)],
        compiler_params=_params("arbitrary"),
        name="bias",
    )(rel_table, sinks, jnp.asarray(bucket), jnp.asarray(valid), w)


def _zero_first_row(a):
    top = a[:BF16_TILE_ROWS]
    row = lax.broadcasted_iota(jnp.int32, top.shape, 0)
    return jnp.concatenate([jnp.where(row == 0, 0.0, top).astype(a.dtype), a[BF16_TILE_ROWS:]], axis=0)


def _attn_kernel(q_ref, kp_ref, kc_ref, vp_ref, vc_ref, bias_ref, wa_ref, wb_ref,
                 o_ref, wa16_ref, wb16_ref):
    l = ATTN_BLOCK
    dh = ATTN_HEAD_DIM
    wa16_ref[...] = wa_ref[...].astype(BF16)
    wb16_ref[...] = wb_ref[...].astype(BF16)
    lane = lax.broadcasted_iota(jnp.int32, (1, 2 * dh), 1)
    low = lane < dh
    first = jnp.minimum(pl.program_id(1), 1)

    for pair in range(ATTN_KV_HEADS // 2):
        pc = slice(pair * 2 * dh, (pair + 1) * 2 * dh)
        kk = pltpu.bitcast(jnp.concatenate([kp_ref[0, :, pc], kc_ref[0, :, pc]], axis=0), jnp.int32)
        vv = pltpu.bitcast(jnp.concatenate([vp_ref[0, :, pc], vc_ref[0, :, pc]], axis=0), jnp.int32)
        kk_sw = pltpu.roll(kk, dh, axis=1)
        vv_sw = pltpu.roll(vv, dh, axis=1)
        for sub in range(2):
            kvh = pair * 2 + sub
            if sub == 0:
                k2_all = jnp.where(low, kk, kk_sw)
                v2_all = jnp.where(low, vv, vv_sw)
            else:
                k2_all = jnp.where(low, kk_sw, kk)
                v2_all = jnp.where(low, vv_sw, vv)
            k2_all = pltpu.bitcast(k2_all, BF16)
            v2_all = pltpu.bitcast(v2_all, BF16)
            for t in range(ATTN_QB):
                qrows = slice(t * l, (t + 1) * l)
                k2 = _zero_first_row(k2_all[t * l:(t + 2) * l])
                v2 = _zero_first_row(v2_all[t * l:(t + 2) * l])
                qs = []
                for j in range(ATTN_GROUP):
                    hq = kvh * ATTN_GROUP + j
                    qc = slice((hq // 2) * 2 * dh, (hq // 2 + 1) * 2 * dh)
                    q2 = q_ref[0, qrows, qc] * (dh ** -0.5)
                    keep = low if hq % 2 == 0 else jnp.logical_not(low)
                    qs.append(jnp.where(keep, q2, 0.0).astype(BF16))
                s_all = _dot_nt(jnp.concatenate(qs, axis=0), k2)
                ps = []
                for j in range(ATTN_GROUP):
                    hq = kvh * ATTN_GROUP + j
                    bias = bias_ref[first, hq] if t == 0 else bias_ref[1, hq]
                    s = s_all[j * l:(j + 1) * l] + bias
                    p = jnp.exp(s - jnp.max(s, axis=-1, keepdims=True))
                    ps.append(p.astype(BF16))
                v2x = jnp.concatenate([v2, jnp.ones_like(v2)], axis=1)
                o_all = _dot(jnp.concatenate(ps, axis=0), v2x)
                o_all = o_all[:, :2 * dh] / o_all[:, 2 * dh:]
                for jp in range(ATTN_GROUP // 2):
                    hq = kvh * ATTN_GROUP + 2 * jp
                    oe = o_all[(2 * jp) * l:(2 * jp + 1) * l]
                    oo = o_all[(2 * jp + 1) * l:(2 * jp + 2) * l]
                    oc = slice((hq // 2) * 2 * dh, (hq // 2 + 1) * 2 * dh)
                    o_ref[0, qrows, oc] = jnp.where(low, oe, oo).astype(o_ref.dtype)


def _cast_rider_specs(w, steps, steps_per_batch):
    rows = w.shape[0] // steps
    assert rows * steps == w.shape[0] and rows % BF16_TILE_ROWS == 0
    spec = pl.BlockSpec((rows, w.shape[1]), lambda bi, n: (bi * steps_per_batch + n, 0))
    return spec, jax.ShapeDtypeStruct(w.shape, BF16)


def _attn(p16, bias, wa, wb):
    b, s, _ = p16.shape
    l = ATTN_BLOCK
    tq = ATTN_QB * l
    nq = s // tq
    q_blk = P16_BLOCKS.index("aq")
    k_blk = P16_KV_COL // KV_WIDTH
    v_blk = k_blk + 1
    prev_spec = lambda blk: pl.BlockSpec(
        (1, l, KV_WIDTH), lambda bi, n: (bi, jnp.maximum(n * ATTN_QB - 1, 0), blk))
    cur_spec = lambda blk: pl.BlockSpec((1, tq, KV_WIDTH), lambda bi, n: (bi, n, blk))
    wa_spec, wa_shape = _cast_rider_specs(wa, b * nq, nq)
    wb_spec, wb_shape = _cast_rider_specs(wb, b * nq, nq)
    return pl.pallas_call(
        _attn_kernel,
        grid=(b, nq),
        in_specs=[pl.BlockSpec((1, tq, ATTN_WIDTH), lambda bi, n: (bi, n, q_blk)),
                  prev_spec(k_blk), cur_spec(k_blk), prev_spec(v_blk), cur_spec(v_blk),
                  pl.BlockSpec(bias.shape, lambda bi, n: (0, 0, 0, 0), pipeline_mode=pl.Buffered(1)),
                  wa_spec, wb_spec],
        out_specs=[pl.BlockSpec((1, tq, ATTN_WIDTH), lambda bi, n: (bi, n, 0)), wa_spec, wb_spec],
        out_shape=[jax.ShapeDtypeStruct((b, s, ATTN_WIDTH), BF16), wa_shape, wb_shape],
        compiler_params=_params("arbitrary", "arbitrary"),
        name="attn",
    )(p16, p16, p16, p16, p16, bias, wa, wb)


def _outproj_kernel(orec_ref, oatt_ref, x_ref, mod_ref, nw_ref, w_ref, x1_ref, h2_ref, rs_ref):
    ssq = None
    for c0 in range(0, D_MODEL, OUTPROJ_TN):
        cs = slice(c0, c0 + OUTPROJ_TN)
        y = _dot(orec_ref[0], w_ref[0:HGRN_WIDTH, cs]) + _dot(oatt_ref[0], w_ref[HGRN_WIDTH:, cs])
        x1 = x_ref[0, :, cs] + mod_ref[0, 2:3, cs] * y
        x1_ref[0, :, cs] = x1
        ssq = _lane_folded_sumsq(ssq, x1)
    rs_ref[...] = _rsqrt_mean(ssq)

    shift = mod_ref[0, 3:4, :]
    wmod = nw_ref[...] * (1.0 + mod_ref[0, 4:5, :])

    def chunk(rows):
        h2_ref[0, rows, :] = (x1_ref[0, rows, :] * rs_ref[rows, :] * wmod + shift).astype(BF16)

    _for_row_chunks(OUTPROJ_TM, chunk)


def _outproj(o_rec, o_att, x, mod, norm_w, w_bf):
    b, s, d = x.shape
    tm = OUTPROJ_TM
    return pl.pallas_call(
        _outproj_kernel,
        grid=(b, s // tm),
        in_specs=[pl.BlockSpec((1, tm, HGRN_WIDTH), lambda bi, m: (bi, m, 0)),
                  pl.BlockSpec((1, tm, ATTN_WIDTH), lambda bi, m: (bi, m, 0)),
                  pl.BlockSpec((1, tm, d), lambda bi, m: (bi, m, 0)),
                  pl.BlockSpec((1, 6, d), lambda bi, m: (bi, 0, 0)),
                  pl.BlockSpec((1, d), lambda bi, m: (0, 0)),
                  pl.BlockSpec(w_bf.shape, lambda bi, m: (0, 0), pipeline_mode=pl.Buffered(1))],
        out_specs=[pl.BlockSpec((1, tm, d), lambda bi, m: (bi, m, 0)),
                   pl.BlockSpec((1, tm, d), lambda bi, m: (bi, m, 0))],
        out_shape=[jax.ShapeDtypeStruct((b, s, d), F32),
                   jax.ShapeDtypeStruct((b, s, d), BF16)],
        scratch_shapes=[pltpu.VMEM((tm, 1), F32)],
        compiler_params=_params("parallel", "parallel"),
        name="outproj",
    )(o_rec, o_att, x, mod, norm_w, w_bf)


def _ffn_kernel(h_ref, x1_hbm, mod_ref, fw_ref, wg_ref, wu_ref, wd_ref, o_ref, x1_buf, rs_ref, x1_sem,
                *, final_norm):
    bi, m, j = pl.program_id(0), pl.program_id(1), pl.program_id(2)

    def x1_copy():
        rows = pl.ds(pl.multiple_of(m * FFN_TM, FFN_TM), FFN_TM)
        return pltpu.make_async_copy(x1_hbm.at[bi, rows, :], x1_buf, x1_sem)

    last = pl.num_programs(2) - 1
    ksubs = [slice(k0, k0 + FFN_SUB) for k0 in range(0, FFN_TF, FFN_SUB)]
    pieces = [slice(c0, c0 + FFN_DN) for c0 in range(0, D_MODEL, FFN_DN)]

    def acts():
        return [(_silu(_dot(h_ref[0], wg_ref[:, ks])) * _dot(h_ref[0], wu_ref[:, ks])).astype(BF16)
                for ks in ksubs]

    def down(a, cs):
        return functools.reduce(lambda x, y: x + y, [_dot(ai, wd_ref[ks, cs]) for ai, ks in zip(a, ksubs)])

    @pl.when(j == 0)
    def _():
        x1_copy().start()
        a = acts()
        for cs in pieces:
            o_ref[0, :, cs] = down(a, cs)

    @pl.when(jnp.logical_and(j > 0, j < last))
    def _():
        a = acts()
        for cs in pieces:
            o_ref[0, :, cs] += down(a, cs)

    @pl.when(j == last)
    def _():
        x1_copy().wait()
        a = acts()
        ssq = None
        for cs in pieces:
            x2 = x1_buf[:, cs] + mod_ref[0, 5:6, cs] * (o_ref[0, :, cs] + down(a, cs))
            x1_buf[:, cs] = x2
            if final_norm:
                ssq = _lane_folded_sumsq(ssq, x2)
        if final_norm:
            rs_ref[...] = _rsqrt_mean(ssq)
        fw = fw_ref[...]

        def chunk(rows):
            x2 = x1_buf[rows, :]
            o_ref[0, rows, :] = x2 * rs_ref[rows, :] * fw if final_norm else x2

        _for_row_chunks(FFN_TM, chunk)


def _ffn(h2, x1, mod, final_w, wg, wu, wd, final_norm):
    b, s, d = x1.shape
    tm, tf = FFN_TM, FFN_TF
    return pl.pallas_call(
        functools.partial(_ffn_kernel, final_norm=final_norm),
        grid=(b, s // tm, D_FF // tf),
        in_specs=[pl.BlockSpec((1, tm, d), lambda bi, m, j: (bi, m, 0)),
                  pl.BlockSpec(memory_space=pl.ANY),
                  pl.BlockSpec((1, 6, d), lambda bi, m, j: (bi, 0, 0)),
                  pl.BlockSpec((1, d), lambda bi, m, j: (0, 0)),
                  pl.BlockSpec((d, tf), lambda bi, m, j: (0, j)),
                  pl.BlockSpec((d, tf), lambda bi, m, j: (0, j)),
                  pl.BlockSpec((tf, d), lambda bi, m, j: (j, 0))],
        out_specs=pl.BlockSpec((1, tm, d), lambda bi, m, j: (bi, m, 0)),
        out_shape=jax.ShapeDtypeStruct((b, s, d), F32),
        scratch_shapes=[pltpu.VMEM((tm, d), F32), pltpu.VMEM((tm, 1), F32), pltpu.SemaphoreType.DMA(())],
        compiler_params=_params("arbitrary", "arbitrary", "arbitrary"),
        name="ffn",
    )(h2, x1, mod, final_w, wg, wu, wd)


def kernel(x, c, w_ada, b_ada, norm1_w, w_in, lower_bounds, hgrn_norm_w, attn_sinks,
           rel_bias_table, w_out, norm2_w, w_gate, w_up, w_down, final_norm_w):
    b, s, d = x.shape
    assert (d, w_in.shape[-1], w_gate.shape[-1]) == (D_MODEL, IN_WIDTH, D_FF)
    assert s % max(INPROJ_TM, HGRN_TC, OUTPROJ_TM, FFN_TM, ATTN_QB * ATTN_BLOCK) == 0
    depth = w_ada.shape[0]
    c8 = jnp.pad(c, ((0, 8 - b), (0, 0)))
    for layer in range(depth):
        bias, wi16 = _bias_table(rel_bias_table, attn_sinks[layer], w_in[layer])
        mod = _ada(c8, w_ada[layer], b_ada[layer][None, :])[:b].reshape(b, 6, d)
        gate, p16, dec = _inproj(x, mod, norm1_w[layer][None, :], wi16, lower_bounds,
                                 hgrn_norm_w[layer][None, :], layer)
        o_rec, wd16, wo16 = _hgrn(gate, p16, dec, w_down[layer], w_out[layer])
        o_att, wg16, wu16 = _attn(p16, bias, w_gate[layer], w_up[layer])
        x1, h2 = _outproj(o_rec, o_att, x, mod, norm2_w[layer][None, :], wo16)
        x = _ffn(h2, x1, mod, final_norm_w[None, :], wg16, wu16, wd16,
                 final_norm=(layer == depth - 1))
    return x
```

```python
import functools

import numpy as np
import jax
import jax.numpy as jnp
from jax import lax
from jax.experimental import pallas as pl
from jax.experimental.pallas import tpu as pltpu

D_MODEL = 2048
HGRN_WIDTH = 1024
HGRN_HEAD_DIM = 128
HGRN_HEADS = HGRN_WIDTH // HGRN_HEAD_DIM
HGRN_CHUNK = 64
ATTN_WIDTH = 1024
ATTN_HEAD_DIM = 64
ATTN_Q_HEADS = ATTN_WIDTH // ATTN_HEAD_DIM
ATTN_KV_HEADS = 4
ATTN_GROUP = ATTN_Q_HEADS // ATTN_KV_HEADS
WINDOW = 128
ATTN_BLOCK = 128
ATTN_QB = 4
REL_BUCKETS = 32
REL_MAX_DIST = 128
D_FF = 5632
KV_WIDTH = ATTN_KV_HEADS * ATTN_HEAD_DIM
IN_WIDTH = 4 * HGRN_WIDTH + ATTN_WIDTH + 2 * KV_WIDTH
P16_BLOCKS = ("v", "aq", "q_rel", "k_rel", "q_dec", "k_dec")
P16_KV_COL = len(P16_BLOCKS) * HGRN_WIDTH
P16_WIDTH = P16_KV_COL + 2 * KV_WIDTH
EPS = 1e-6
NEG_INF = -1e30
LOG2E = 1.4426950408889634

F32 = jnp.float32
BF16 = jnp.bfloat16

VMEM_LIMIT_BYTES = 56 * 1024 * 1024
LANES = 128
SUBLANES = 8
BF16_TILE_ROWS = 2 * SUBLANES

ADA_TN = 1024
INPROJ_TM = 512
INPROJ_TN = 256
HGRN_TC = 1024
HGRN_UNROLL = 4
HGRN_ROWS = HGRN_UNROLL * HGRN_CHUNK
HGRN_TRIP_CHUNKS = 8
OUTPROJ_TM = 512
OUTPROJ_TN = 256
FFN_TM = 1024
FFN_TF = 512
FFN_SUB = 256
FFN_DN = 512
ROW_CHUNK = 16
ROW_UNROLL = 32


def _params(*semantics):
    return pltpu.CompilerParams(dimension_semantics=semantics,
                                vmem_limit_bytes=VMEM_LIMIT_BYTES)


def _sigmoid(v):
    return 1.0 / (1.0 + jnp.exp2(v * (-LOG2E)))


def _silu(v):
    return v * _sigmoid(v)


def _dot(a, b):
    return jnp.dot(a, b, preferred_element_type=F32)


def _dot_nt(a, b):
    return lax.dot_general(a, b, (((1,), (1,)), ((), ())), preferred_element_type=F32)


def _dot_tn(a, b):
    return lax.dot_general(a, b, (((0,), (0,)), ((), ())), preferred_element_type=F32)


def _ada_kernel(c_ref, w_ref, b_ref, o_ref):
    c_act = _silu(c_ref[...])
    o_ref[...] = _dot(c_act.astype(BF16), w_ref[...].astype(BF16)) + b_ref[...]


def _ada(c8, w, b):
    n = w.shape[1]
    return pl.pallas_call(
        _ada_kernel,
        grid=(n // ADA_TN,),
        in_specs=[pl.BlockSpec((8, D_MODEL), lambda j: (0, 0)),
                  pl.BlockSpec((D_MODEL, ADA_TN), lambda j: (0, j)),
                  pl.BlockSpec((1, ADA_TN), lambda j: (0, j))],
        out_specs=pl.BlockSpec((8, ADA_TN), lambda j: (0, j)),
        out_shape=jax.ShapeDtypeStruct((8, n), F32),
        compiler_params=_params("arbitrary"),
        name="ada",
    )(c8, w, b)


def _modulated_norm(x, wmod, shift):
    return x * lax.rsqrt(jnp.mean(x * x, axis=-1, keepdims=True) + EPS) * wmod + shift


def _for_row_chunks(n_rows, fn):
    def body(i, carry):
        fn(pl.ds(pl.multiple_of(i * ROW_CHUNK, ROW_CHUNK), ROW_CHUNK))
        return carry

    lax.fori_loop(0, n_rows // ROW_CHUNK, body, 0, unroll=ROW_UNROLL)


def _lane_folded_sumsq(ssq, v):
    sq = v * v
    for k0 in range(0, v.shape[1], LANES):
        ssq = sq[:, k0:k0 + LANES] if ssq is None else ssq + sq[:, k0:k0 + LANES]
    return ssq


def _rsqrt_mean(ssq):
    return lax.rsqrt(jnp.sum(ssq, axis=-1, keepdims=True) * (1.0 / D_MODEL) + EPS)


def _split3(v):
    hi = v.astype(BF16)
    r1 = v - hi.astype(F32)
    mid = r1.astype(BF16)
    lo = (r1 - mid.astype(F32)).astype(BF16)
    return hi, mid, lo


def _hgrn_tril3():
    tril = np.tril(np.ones((HGRN_CHUNK, HGRN_CHUNK), np.float32))
    return jnp.asarray(np.concatenate([tril, tril, tril], axis=1), dtype=BF16)


def _hgrn_prepare(q, f_logit, g, lb, nw, tril3):
    c, nu = HGRN_CHUNK, HGRN_UNROLL

    def per_chunk_rows(x, offset):
        rid = lax.broadcasted_iota(jnp.int32, x.shape, 0)
        out = jnp.broadcast_to(x[offset:offset + 1, :], x.shape)
        for u in range(1, nu):
            out = jnp.where(rid >= u * c, jnp.broadcast_to(x[u * c + offset:u * c + offset + 1, :], x.shape), out)
        return out

    f = lb + (1.0 - lb) * _sigmoid(f_logit)
    k = 1.0 - f
    parts = _split3(jnp.log2(f))
    b = jnp.concatenate(
        [_dot(tril3, jnp.concatenate([p[u * c:(u + 1) * c] for p in parts], axis=0)) for u in range(nu)],
        axis=0)
    b_mid = per_chunk_rows(b, c // 2 - 1)
    b_last = per_chunk_rows(b, c - 1)
    d_mid = b - b_mid
    qr = q * jnp.exp2(d_mid)
    kr = k * jnp.exp2(-d_mid)
    q_dec = (qr * jnp.exp2(b_mid)).astype(BF16)
    k_dec = (kr * jnp.exp2(b_last - b_mid)).astype(BF16)
    decays = [jnp.exp2(b[(u + 1) * c - 1:(u + 1) * c, :]) for u in range(nu)]
    return qr.astype(BF16), kr.astype(BF16), q_dec, k_dec, nw * _silu(g), decays


def _inproj_kernel(x_ref, mod_ref, nw_ref, w_ref, lb_ref, nwh_ref, tril_ref, gate_ref, o16_ref, dec_ref,
                   h_ref, *, layer):
    shift = mod_ref[0, 0:1, :]
    wmod = nw_ref[...] * (1.0 + mod_ref[0, 1:2, :])

    def chunk(rows):
        h_ref[rows, :] = _modulated_norm(x_ref[0, rows, :], wmod, shift).astype(BF16)

    _for_row_chunks(INPROJ_TM, chunk)

    lb_rows = [lb_ref[r:r + 1, :] for r in range(lb_ref.shape[0])]
    lb_max = functools.reduce(jnp.maximum, lb_rows)
    lb_exp = [jnp.exp(r - lb_max) for r in lb_rows]
    lb_all = sum(lb_exp[:layer + 1]) / sum(lb_exp)

    tn, hw = INPROJ_TN, HGRN_WIDTH
    p16_col = {name: i * hw for i, name in enumerate(P16_BLOCKS)}

    def project(w_col):
        return _dot(h_ref[...], w_ref[:, w_col:w_col + tn])

    def to_p16(name, c0, value):
        o16_ref[0, :, p16_col[name] + c0:p16_col[name] + c0 + tn] = value.astype(BF16)

    for c0 in range(0, hw, tn):
        cs = slice(c0, c0 + tn)
        hq, hf, hg = project(c0), project(hw + c0), project(3 * hw + c0)
        followers = [lambda: to_p16("v", c0, project(2 * hw + c0)),
                     lambda: to_p16("aq", c0, project(4 * hw + c0))]
        for half in range(INPROJ_TM // HGRN_ROWS):
            rows = slice(half * HGRN_ROWS, (half + 1) * HGRN_ROWS)
            q_rel, k_rel, q_dec, k_dec, gate, decays = _hgrn_prepare(
                hq[rows], hf[rows], hg[rows], lb_all[:, cs], nwh_ref[:, cs], tril_ref[...])
            for name, value in (("q_rel", q_rel), ("k_rel", k_rel), ("q_dec", q_dec), ("k_dec", k_dec)):
                o16_ref[0, rows, p16_col[name] + c0:p16_col[name] + c0 + tn] = value
            gate_ref[0, rows, cs] = gate
            for u, decay in enumerate(decays):
                dec_ref[0, half * HGRN_UNROLL + u:half * HGRN_UNROLL + u + 1, cs] = decay
            n_groups = INPROJ_TM // HGRN_ROWS
            if (half + 1) % (n_groups // 2) == 0:
                followers[(half + 1) // (n_groups // 2) - 1]()
    for c0 in range(0, 2 * KV_WIDTH, tn):
        o16_ref[0, :, P16_KV_COL + c0:P16_KV_COL + c0 + tn] = project(4 * hw + ATTN_WIDTH + c0).astype(BF16)


def _inproj(x, mod, norm_w, w_bf, lower_bounds, hgrn_norm_w, layer):
    b, s, d = x.shape
    tm = INPROJ_TM
    assert tm % (2 * HGRN_ROWS) == 0 and (2 * KV_WIDTH) % INPROJ_TN == 0
    chunks = tm // HGRN_CHUNK
    tril3 = _hgrn_tril3()
    return pl.pallas_call(
        functools.partial(_inproj_kernel, layer=layer),
        grid=(b, s // tm),
        in_specs=[pl.BlockSpec((1, tm, d), lambda bi, m: (bi, m, 0)),
                  pl.BlockSpec((1, 6, d), lambda bi, m: (bi, 0, 0)),
                  pl.BlockSpec((1, d), lambda bi, m: (0, 0)),
                  pl.BlockSpec(w_bf.shape, lambda bi, m: (0, 0), pipeline_mode=pl.Buffered(1)),
                  pl.BlockSpec(lower_bounds.shape, lambda bi, m: (0, 0)),
                  pl.BlockSpec((1, HGRN_WIDTH), lambda bi, m: (0, 0)),
                  pl.BlockSpec(tril3.shape, lambda bi, m: (0, 0))],
        out_specs=[pl.BlockSpec((1, tm, HGRN_WIDTH), lambda bi, m: (bi, m, 0)),
                   pl.BlockSpec((1, tm, P16_WIDTH), lambda bi, m: (bi, m, 0)),
                   pl.BlockSpec((1, chunks, HGRN_WIDTH), lambda bi, m: (bi, m, 0))],
        out_shape=[jax.ShapeDtypeStruct((b, s, HGRN_WIDTH), F32),
                   jax.ShapeDtypeStruct((b, s, P16_WIDTH), BF16),
                   jax.ShapeDtypeStruct((b, s // HGRN_CHUNK, HGRN_WIDTH), F32)],
        scratch_shapes=[pltpu.VMEM((tm, d), BF16)],
        compiler_params=_params("parallel", "parallel"),
        name="inproj",
    )(x, mod, norm_w, w_bf, lower_bounds, hgrn_norm_w, tril3)


def _hgrn_kernel(v_ref, qrel_ref, krel_ref, qdec_ref, kdec_ref, gate_ref, dec_ref, wa_ref, wb_ref,
                 o_ref, wa16_ref, wb16_ref, st_ref):
    c, dk, nu = HGRN_CHUNK, HGRN_HEAD_DIM, HGRN_TRIP_CHUNKS
    wa16_ref[...] = wa_ref[...].astype(BF16)
    wb16_ref[...] = wb_ref[...].astype(BF16)

    @pl.when(pl.program_id(1) == 0)
    def _():
        st_ref[...] = jnp.zeros_like(st_ref)

    causal = (lax.broadcasted_iota(jnp.int32, (c, c), 0) >= lax.broadcasted_iota(jnp.int32, (c, c), 1))
    heads = [slice(h * dk, (h + 1) * dk) for h in range(HGRN_HEADS)]

    def body(i, carry):
        st = [st_ref[h] for h in range(HGRN_HEADS)]
        for u in range(nu):
            rows = pl.ds(pl.multiple_of((i * nu + u) * c, c), c)
            decay = dec_ref[0, pl.ds(i * nu + u, 1), :]
            a = [jnp.where(causal, _dot_nt(qrel_ref[0, rows, hs], krel_ref[0, rows, hs]), 0.0).astype(BF16)
                 for hs in heads]
            o = [_dot(a[h], v_ref[0, rows, hs]) + _dot_nt(qdec_ref[0, rows, hs], st[h].astype(BF16))
                 for h, hs in enumerate(heads)]
            st = [decay[:, hs] * st[h] + _dot_tn(v_ref[0, rows, hs], kdec_ref[0, rows, hs])
                  for h, hs in enumerate(heads)]
            for h, hs in enumerate(heads):
                oh = o[h] * lax.rsqrt(jnp.mean(o[h] * o[h], axis=-1, keepdims=True) + EPS)
                o_ref[0, rows, hs] = (oh * gate_ref[0, rows, hs]).astype(o_ref.dtype)
        for h in range(HGRN_HEADS):
            st_ref[h] = st[h]
        return carry

    lax.fori_loop(0, HGRN_TC // (nu * c), body, 0)


def _hgrn(gate, p16, dec, wa, wb):
    b, s, _ = gate.shape
    nt = s // HGRN_TC
    blk = lambda name: pl.BlockSpec((1, HGRN_TC, HGRN_WIDTH),
                                    lambda bi, t, j=P16_BLOCKS.index(name): (bi, t, j))
    wa_spec, wa_shape = _cast_rider_specs(wa, b * nt, nt)
    wb_spec, wb_shape = _cast_rider_specs(wb, b * nt, nt)
    return pl.pallas_call(
        _hgrn_kernel,
        grid=(b, nt),
        in_specs=[blk("v"), blk("q_rel"), blk("k_rel"), blk("q_dec"), blk("k_dec"),
                  pl.BlockSpec((1, HGRN_TC, HGRN_WIDTH), lambda bi, t: (bi, t, 0)),
                  pl.BlockSpec((1, HGRN_TC // HGRN_CHUNK, HGRN_WIDTH), lambda bi, t: (bi, t, 0)),
                  wa_spec, wb_spec],
        out_specs=[pl.BlockSpec((1, HGRN_TC, HGRN_WIDTH), lambda bi, t: (bi, t, 0)), wa_spec, wb_spec],
        out_shape=[jax.ShapeDtypeStruct((b, s, HGRN_WIDTH), BF16), wa_shape, wb_shape],
        scratch_shapes=[pltpu.VMEM((HGRN_HEADS, HGRN_HEAD_DIM, HGRN_HEAD_DIM), F32)],
        compiler_params=_params("arbitrary", "arbitrary"),
        name="hgrn",
    )(p16, p16, p16, p16, p16, gate, dec, wa, wb)


def _t5_causal_buckets(dist):
    max_exact = REL_BUCKETS // 2
    d = np.maximum(dist, 0)
    log_b = max_exact + (np.log(np.maximum(d, 1) / max_exact)
                         / np.log(REL_MAX_DIST / max_exact)
                         * (REL_BUCKETS - max_exact)).astype(np.int32)
    log_b = np.minimum(log_b, REL_BUCKETS - 1)
    return np.where(d < max_exact, d, log_b).astype(np.int32)


def _bias_kernel(tab_ref, sink_ref, bucket_ref, valid_ref, w_ref, o_ref, w16_ref):
    w16_ref[...] = w_ref[...].astype(BF16)
    h = pl.program_id(0)
    bucket = bucket_ref[...]
    acc = jnp.zeros(bucket.shape, F32)
    for bk in range(REL_BUCKETS):
        acc = jnp.where(bucket == bk, tab_ref[bk, h], acc)
    sink_col = lax.broadcasted_iota(jnp.int32, bucket.shape, 1) == 0
    for variant in range(2):
        masked = jnp.where(valid_ref[variant] != 0, acc, NEG_INF)
        o_ref[variant, 0] = jnp.where(sink_col, sink_ref[h], masked)


def _bias_table(rel_table, sinks, w):
    l = ATTN_BLOCK
    qi = np.arange(l)[:, None]
    kj = np.arange(2 * l)[None, :]
    dist = qi + l - kj
    in_window = (dist >= 0) & (dist < WINDOW)
    valid = np.stack([in_window & (kj >= l), in_window]).astype(np.int32)
    assert not valid[:, :, 0].any()
    bucket = _t5_causal_buckets(dist)
    w_rows = w.shape[0] // ATTN_Q_HEADS
    assert w_rows * ATTN_Q_HEADS == w.shape[0] and w_rows % BF16_TILE_ROWS == 0
    w_spec = pl.BlockSpec((w_rows, w.shape[1]), lambda h: (h, 0))
    return pl.pallas_call(
        _bias_kernel,
        grid=(ATTN_Q_HEADS,),
        in_specs=[pl.BlockSpec(memory_space=pltpu.SMEM),
                  pl.BlockSpec(memory_space=pltpu.SMEM),
                  pl.BlockSpec((l, 2 * l), lambda h: (0, 0)),
                  pl.BlockSpec((2, l, 2 * l), lambda h: (0, 0, 0)),
                  w_spec],
        out_specs=[pl.BlockSpec((2, 1, l, 2 * l), lambda h: (0, h, 0, 0)), w_spec],
        out_shape=[jax.ShapeDtypeStruct((2, ATTN_Q_HEADS, l, 2 * l), F32),
                   jax.ShapeDtypeStruct(w.shape, BF16)],
        compiler_params=_params("arbitrary"),
        name="bias",
    )(rel_table, sinks, jnp.asarray(bucket), jnp.asarray(valid), w)


def _zero_first_row(a):
    top = a[:BF16_TILE_ROWS]
    row = lax.broadcasted_iota(jnp.int32, top.shape, 0)
    return jnp.concatenate([jnp.where(row == 0, 0.0, top).astype(a.dtype), a[BF16_TILE_ROWS:]], axis=0)


def _attn_kernel(q_ref, kp_ref, kc_ref, vp_ref, vc_ref, bias_ref, wa_ref, wb_ref,
                 o_ref, wa16_ref, wb16_ref):
    l = ATTN_BLOCK
    dh = ATTN_HEAD_DIM
    wa16_ref[...] = wa_ref[...].astype(BF16)
    wb16_ref[...] = wb_ref[...].astype(BF16)
    lane = lax.broadcasted_iota(jnp.int32, (1, 2 * dh), 1)
    low = lane < dh
    first = jnp.minimum(pl.program_id(1), 1)

    for pair in range(ATTN_KV_HEADS // 2):
        pc = slice(pair * 2 * dh, (pair + 1) * 2 * dh)
        kk = pltpu.bitcast(jnp.concatenate([kp_ref[0, :, pc], kc_ref[0, :, pc]], axis=0), jnp.int32)
        vv = pltpu.bitcast(jnp.concatenate([vp_ref[0, :, pc], vc_ref[0, :, pc]], axis=0), jnp.int32)
        kk_sw = pltpu.roll(kk, dh, axis=1)
        vv_sw = pltpu.roll(vv, dh, axis=1)
        for sub in range(2):
            kvh = pair * 2 + sub
            if sub == 0:
                k2_all = jnp.where(low, kk, kk_sw)
                v2_all = jnp.where(low, vv, vv_sw)
            else:
                k2_all = jnp.where(low, kk_sw, kk)
                v2_all = jnp.where(low, vv_sw, vv)
            k2_all = pltpu.bitcast(k2_all, BF16)
            v2_all = pltpu.bitcast(v2_all, BF16)
            for t in range(ATTN_QB):
                qrows = slice(t * l, (t + 1) * l)
                k2 = _zero_first_row(k2_all[t * l:(t + 2) * l])
                v2 = _zero_first_row(v2_all[t * l:(t + 2) * l])
                qs = []
                for j in range(ATTN_GROUP):
                    hq = kvh * ATTN_GROUP + j
                    qc = slice((hq // 2) * 2 * dh, (hq // 2 + 1) * 2 * dh)
                    q2 = q_ref[0, qrows, qc] * (dh ** -0.5)
                    keep = low if hq % 2 == 0 else jnp.logical_not(low)
                    qs.append(jnp.where(keep, q2, 0.0).astype(BF16))
                s_all = _dot_nt(jnp.concatenate(qs, axis=0), k2)
                ps = []
                for j in range(ATTN_GROUP):
                    hq = kvh * ATTN_GROUP + j
                    bias = bias_ref[first, hq] if t == 0 else bias_ref[1, hq]
                    s = s_all[j * l:(j + 1) * l] + bias
                    p = jnp.exp(s - jnp.max(s, axis=-1, keepdims=True))
                    ps.append(p.astype(BF16))
                v2x = jnp.concatenate([v2, jnp.ones_like(v2)], axis=1)
                o_all = _dot(jnp.concatenate(ps, axis=0), v2x)
                o_all = o_all[:, :2 * dh] / o_all[:, 2 * dh:]
                for jp in range(ATTN_GROUP // 2):
                    hq = kvh * ATTN_GROUP + 2 * jp
                    oe = o_all[(2 * jp) * l:(2 * jp + 1) * l]
                    oo = o_all[(2 * jp + 1) * l:(2 * jp + 2) * l]
                    oc = slice((hq // 2) * 2 * dh, (hq // 2 + 1) * 2 * dh)
                    o_ref[0, qrows, oc] = jnp.where(low, oe, oo).astype(o_ref.dtype)


def _cast_rider_specs(w, steps, steps_per_batch):
    rows = w.shape[0] // steps
    assert rows * steps == w.shape[0] and rows % BF16_TILE_ROWS == 0
    spec = pl.BlockSpec((rows, w.shape[1]), lambda bi, n: (bi * steps_per_batch + n, 0))
    return spec, jax.ShapeDtypeStruct(w.shape, BF16)


def _attn(p16, bias, wa, wb):
    b, s, _ = p16.shape
    l = ATTN_BLOCK
    tq = ATTN_QB * l
    nq = s // tq
    q_blk = P16_BLOCKS.index("aq")
    k_blk = P16_KV_COL // KV_WIDTH
    v_blk = k_blk + 1
    prev_spec = lambda blk: pl.BlockSpec(
        (1, l, KV_WIDTH), lambda bi, n: (bi, jnp.maximum(n * ATTN_QB - 1, 0), blk))
    cur_spec = lambda blk: pl.BlockSpec((1, tq, KV_WIDTH), lambda bi, n: (bi, n, blk))
    wa_spec, wa_shape = _cast_rider_specs(wa, b * nq, nq)
    wb_spec, wb_shape = _cast_rider_specs(wb, b * nq, nq)
    return pl.pallas_call(
        _attn_kernel,
        grid=(b, nq),
        in_specs=[pl.BlockSpec((1, tq, ATTN_WIDTH), lambda bi, n: (bi, n, q_blk)),
                  prev_spec(k_blk), cur_spec(k_blk), prev_spec(v_blk), cur_spec(v_blk),
                  pl.BlockSpec(bias.shape, lambda bi, n: (0, 0, 0, 0), pipeline_mode=pl.Buffered(1)),
                  wa_spec, wb_spec],
        out_specs=[pl.BlockSpec((1, tq, ATTN_WIDTH), lambda bi, n: (bi, n, 0)), wa_spec, wb_spec],
        out_shape=[jax.ShapeDtypeStruct((b, s, ATTN_WIDTH), BF16), wa_shape, wb_shape],
        compiler_params=_params("arbitrary", "arbitrary"),
        name="attn",
    )(p16, p16, p16, p16, p16, bias, wa, wb)


def _outproj_kernel(orec_ref, oatt_ref, x_ref, mod_ref, nw_ref, w_ref, x1_ref, h2_ref, rs_ref):
    ssq = None
    for c0 in range(0, D_MODEL, OUTPROJ_TN):
        cs = slice(c0, c0 + OUTPROJ_TN)
        y = _dot(orec_ref[0], w_ref[0:HGRN_WIDTH, cs]) + _dot(oatt_ref[0], w_ref[HGRN_WIDTH:, cs])
        x1 = x_ref[0, :, cs] + mod_ref[0, 2:3, cs] * y
        x1_ref[0, :, cs] = x1
        ssq = _lane_folded_sumsq(ssq, x1)
    rs_ref[...] = _rsqrt_mean(ssq)

    shift = mod_ref[0, 3:4, :]
    wmod = nw_ref[...] * (1.0 + mod_ref[0, 4:5, :])

    def chunk(rows):
        h2_ref[0, rows, :] = (x1_ref[0, rows, :] * rs_ref[rows, :] * wmod + shift).astype(BF16)

    _for_row_chunks(OUTPROJ_TM, chunk)


def _outproj(o_rec, o_att, x, mod, norm_w, w_bf):
    b, s, d = x.shape
    tm = OUTPROJ_TM
    return pl.pallas_call(
        _outproj_kernel,
        grid=(b, s // tm),
        in_specs=[pl.BlockSpec((1, tm, HGRN_WIDTH), lambda bi, m: (bi, m, 0)),
                  pl.BlockSpec((1, tm, ATTN_WIDTH), lambda bi, m: (bi, m, 0)),
                  pl.BlockSpec((1, tm, d), lambda bi, m: (bi, m, 0)),
                  pl.BlockSpec((1, 6, d), lambda bi, m: (bi, 0, 0)),
                  pl.BlockSpec((1, d), lambda bi, m: (0, 0)),
                  pl.BlockSpec(w_bf.shape, lambda bi, m: (0, 0), pipeline_mode=pl.Buffered(1))],
        out_specs=[pl.BlockSpec((1, tm, d), lambda bi, m: (bi, m, 0)),
                   pl.BlockSpec((1, tm, d), lambda bi, m: (bi, m, 0))],
        out_shape=[jax.ShapeDtypeStruct((b, s, d), F32),
                   jax.ShapeDtypeStruct((b, s, d), BF16)],
        scratch_shapes=[pltpu.VMEM((tm, 1), F32)],
        compiler_params=_params("parallel", "parallel"),
        name="outproj",
    )(o_rec, o_att, x, mod, norm_w, w_bf)


def _ffn_kernel(h_ref, x1_hbm, mod_ref, fw_ref, wg_ref, wu_ref, wd_ref, o_ref, x1_buf, rs_ref, x1_sem,
                *, final_norm):
    bi, m, j = pl.program_id(0), pl.program_id(1), pl.program_id(2)

    def x1_copy():
        rows = pl.ds(pl.multiple_of(m * FFN_TM, FFN_TM), FFN_TM)
        return pltpu.make_async_copy(x1_hbm.at[bi, rows, :], x1_buf, x1_sem)

    last = pl.num_programs(2) - 1
    ksubs = [slice(k0, k0 + FFN_SUB) for k0 in range(0, FFN_TF, FFN_SUB)]
    pieces = [slice(c0, c0 + FFN_DN) for c0 in range(0, D_MODEL, FFN_DN)]

    def acts():
        return [(_silu(_dot(h_ref[0], wg_ref[:, ks])) * _dot(h_ref[0], wu_ref[:, ks])).astype(BF16)
                for ks in ksubs]

    def down(a, cs):
        return functools.reduce(lambda x, y: x + y, [_dot(ai, wd_ref[ks, cs]) for ai, ks in zip(a, ksubs)])

    @pl.when(j == 0)
    def _():
        x1_copy().start()
        a = acts()
        for cs in pieces:
            o_ref[0, :, cs] = down(a, cs)

    @pl.when(jnp.logical_and(j > 0, j < last))
    def _():
        a = acts()
        for cs in pieces:
            o_ref[0, :, cs] += down(a, cs)

    @pl.when(j == last)
    def _():
        x1_copy().wait()
        a = acts()
        ssq = None
        for cs in pieces:
            x2 = x1_buf[:, cs] + mod_ref[0, 5:6, cs] * (o_ref[0, :, cs] + down(a, cs))
            x1_buf[:, cs] = x2
            if final_norm:
                ssq = _lane_folded_sumsq(ssq, x2)
        if final_norm:
            rs_ref[...] = _rsqrt_mean(ssq)
        fw = fw_ref[...]

        def chunk(rows):
            x2 = x1_buf[rows, :]
            o_ref[0, rows, :] = x2 * rs_ref[rows, :] * fw if final_norm else x2

        _for_row_chunks(FFN_TM, chunk)


def _ffn(h2, x1, mod, final_w, wg, wu, wd, final_norm):
    b, s, d = x1.shape
    tm, tf = FFN_TM, FFN_TF
    return pl.pallas_call(
        functools.partial(_ffn_kernel, final_norm=final_norm),
        grid=(b, s // tm, D_FF // tf),
        in_specs=[pl.BlockSpec((1, tm, d), lambda bi, m, j: (bi, m, 0)),
                  pl.BlockSpec(memory_space=pl.ANY),
                  pl.BlockSpec((1, 6, d), lambda bi, m, j: (bi, 0, 0)),
                  pl.BlockSpec((1, d), lambda bi, m, j: (0, 0)),
                  pl.BlockSpec((d, tf), lambda bi, m, j: (0, j)),
                  pl.BlockSpec((d, tf), lambda bi, m, j: (0, j)),
                  pl.BlockSpec((tf, d), lambda bi, m, j: (j, 0))],
        out_specs=pl.BlockSpec((1, tm, d), lambda bi, m, j: (bi, m, 0)),
        out_shape=jax.ShapeDtypeStruct((b, s, d), F32),
        scratch_shapes=[pltpu.VMEM((tm, d), F32), pltpu.VMEM((tm, 1), F32), pltpu.SemaphoreType.DMA(())],
        compiler_params=_params("arbitrary", "arbitrary", "arbitrary"),
        name="ffn",
    )(h2, x1, mod, final_w, wg, wu, wd)


def kernel(x, c, w_ada, b_ada, norm1_w, w_in, lower_bounds, hgrn_norm_w, attn_sinks,
           rel_bias_table, w_out, norm2_w, w_gate, w_up, w_down, final_norm_w):
    b, s, d = x.shape
    assert (d, w_in.shape[-1], w_gate.shape[-1]) == (D_MODEL, IN_WIDTH, D_FF)
    assert s % max(INPROJ_TM, HGRN_TC, OUTPROJ_TM, FFN_TM, ATTN_QB * ATTN_BLOCK) == 0
    depth = w_ada.shape[0]
    c8 = jnp.pad(c, ((0, 8 - b), (0, 0)))
    for layer in range(depth):
        bias, wi16 = _bias_table(rel_bias_table, attn_sinks[layer], w_in[layer])
        mod = _ada(c8, w_ada[layer], b_ada[layer][None, :])[:b].reshape(b, 6, d)
        gate, p16, dec = _inproj(x, mod, norm1_w[layer][None, :], wi16, lower_bounds,
                                 hgrn_norm_w[layer][None, :], layer)
        o_rec, wd16, wo16 = _hgrn(gate, p16, dec, w_down[layer], w_out[layer])
        o_att, wg16, wu16 = _attn(p16, bias, w_gate[layer], w_up[layer])
        x1, h2 = _outproj(o_rec, o_att, x, mod, norm2_w[layer][None, :], wo16)
        x = _ffn(h2, x1, mod, final_norm_w[None, :], wg16, wu16, wd16,
                 final_norm=(layer == depth - 1))
    return x
```

```python
import functools

import numpy as np
import jax
import jax.numpy as jnp
from jax import lax
from jax.experimental import pallas as pl
from jax.experimental.pallas import tpu as pltpu

D_MODEL = 2048
HGRN_WIDTH = 1024
HGRN_HEAD_DIM = 128
HGRN_HEADS = HGRN_WIDTH // HGRN_HEAD_DIM
HGRN_CHUNK = 64
ATTN_WIDTH = 1024
ATTN_HEAD_DIM = 64
ATTN_Q_HEADS = ATTN_WIDTH // ATTN_HEAD_DIM
ATTN_KV_HEADS = 4
ATTN_GROUP = ATTN_Q_HEADS // ATTN_KV_HEADS
WINDOW = 128
ATTN_BLOCK = 128
ATTN_QB = 4
REL_BUCKETS = 32
REL_MAX_DIST = 128
D_FF = 5632
KV_WIDTH = ATTN_KV_HEADS * ATTN_HEAD_DIM
IN_WIDTH = 4 * HGRN_WIDTH + ATTN_WIDTH + 2 * KV_WIDTH
P16_BLOCKS = ("v", "aq", "q_rel", "k_rel", "q_dec", "k_dec")
P16_KV_COL = len(P16_BLOCKS) * HGRN_WIDTH
P16_WIDTH = P16_KV_COL + 2 * KV_WIDTH
EPS = 1e-6
NEG_INF = -1e30
LOG2E = 1.4426950408889634

F32 = jnp.float32
BF16 = jnp.bfloat16

VMEM_LIMIT_BYTES = 56 * 1024 * 1024
LANES = 128
SUBLANES = 8
BF16_TILE_ROWS = 2 * SUBLANES

ADA_TN = 1024
INPROJ_TM = 512
INPROJ_TN = 256
HGRN_TC = 1024
HGRN_UNROLL = 4
HGRN_ROWS = HGRN_UNROLL * HGRN_CHUNK
HGRN_TRIP_CHUNKS = 16
OUTPROJ_TM = 512
OUTPROJ_TN = 256
FFN_TM = 1024
FFN_TF = 512
FFN_SUB = 256
FFN_DN = 512
ROW_CHUNK = 16


def _params(*semantics):
    return pltpu.CompilerParams(dimension_semantics=semantics,
                                vmem_limit_bytes=VMEM_LIMIT_BYTES)


def _sigmoid(v):
    return 1.0 / (1.0 + jnp.exp2(v * (-LOG2E)))


def _silu(v):
    return v * _sigmoid(v)


def _dot(a, b):
    return jnp.dot(a, b, preferred_element_type=F32)


def _dot_nt(a, b):
    return lax.dot_general(a, b, (((1,), (1,)), ((), ())), preferred_element_type=F32)


def _dot_tn(a, b):
    return lax.dot_general(a, b, (((0,), (0,)), ((), ())), preferred_element_type=F32)


def _ada_kernel(c_ref, w_ref, b_ref, o_ref):
    c_act = _silu(c_ref[...])
    o_ref[...] = _dot(c_act.astype(BF16), w_ref[...].astype(BF16)) + b_ref[...]


def _ada(c8, w, b):
    n = w.shape[1]
    return pl.pallas_call(
        _ada_kernel,
        grid=(n // ADA_TN,),
        in_specs=[pl.BlockSpec((8, D_MODEL), lambda j: (0, 0)),
                  pl.BlockSpec((D_MODEL, ADA_TN), lambda j: (0, j)),
                  pl.BlockSpec((1, ADA_TN), lambda j: (0, j))],
        out_specs=pl.BlockSpec((8, ADA_TN), lambda j: (0, j)),
        out_shape=jax.ShapeDtypeStruct((8, n), F32),
        compiler_params=_params("arbitrary"),
        name="ada",
    )(c8, w, b)


def _modulated_norm(x, wmod, shift):
    return x * lax.rsqrt(jnp.mean(x * x, axis=-1, keepdims=True) + EPS) * wmod + shift


def _for_row_chunks(n_rows, fn, loop_trips=1):
    per_trip = n_rows // loop_trips
    assert per_trip * loop_trips == n_rows and per_trip % ROW_CHUNK == 0

    def trip(i, carry):
        for r0 in range(0, per_trip, ROW_CHUNK):
            start = r0 if loop_trips == 1 else pl.multiple_of(i * per_trip + r0, ROW_CHUNK)
            fn(pl.ds(start, ROW_CHUNK))
        return carry

    if loop_trips == 1:
        trip(0, 0)
    else:
        lax.fori_loop(0, loop_trips, trip, 0)


def _lane_folded_sumsq(ssq, v):
    sq = v * v
    for k0 in range(0, v.shape[1], LANES):
        ssq = sq[:, k0:k0 + LANES] if ssq is None else ssq + sq[:, k0:k0 + LANES]
    return ssq


def _rsqrt_mean(ssq):
    return lax.rsqrt(jnp.sum(ssq, axis=-1, keepdims=True) * (1.0 / D_MODEL) + EPS)


def _split3(v):
    hi = v.astype(BF16)
    r1 = v - hi.astype(F32)
    mid = r1.astype(BF16)
    lo = (r1 - mid.astype(F32)).astype(BF16)
    return hi, mid, lo


def _hgrn_tril3():
    tril = np.tril(np.ones((HGRN_CHUNK, HGRN_CHUNK), np.float32))
    return jnp.asarray(np.concatenate([tril, tril, tril], axis=1), dtype=BF16)


def _hgrn_prepare(q, f_logit, g, lb, nw, tril3):
    c, nu = HGRN_CHUNK, HGRN_UNROLL

    def per_chunk_rows(x, offset):
        rid = lax.broadcasted_iota(jnp.int32, x.shape, 0)
        out = jnp.broadcast_to(x[offset:offset + 1, :], x.shape)
        for u in range(1, nu):
            out = jnp.where(rid >= u * c, jnp.broadcast_to(x[u * c + offset:u * c + offset + 1, :], x.shape), out)
        return out

    f = lb + (1.0 - lb) * _sigmoid(f_logit)
    k = 1.0 - f
    parts = _split3(jnp.log2(f))
    b = jnp.concatenate(
        [_dot(tril3, jnp.concatenate([p[u * c:(u + 1) * c] for p in parts], axis=0)) for u in range(nu)],
        axis=0)
    b_mid = per_chunk_rows(b, c // 2 - 1)
    b_last = per_chunk_rows(b, c - 1)
    d_mid = b - b_mid
    qr = q * jnp.exp2(d_mid)
    kr = k * jnp.exp2(-d_mid)
    q_dec = (qr * jnp.exp2(b_mid)).astype(BF16)
    k_dec = (kr * jnp.exp2(b_last - b_mid)).astype(BF16)
    decays = [jnp.exp2(b[(u + 1) * c - 1:(u + 1) * c, :]) for u in range(nu)]
    return qr.astype(BF16), kr.astype(BF16), q_dec, k_dec, nw * _silu(g), decays


def _inproj_kernel(x_ref, mod_ref, nw_ref, w_ref, lb_ref, nwh_ref, tril_ref, gate_ref, o16_ref, dec_ref,
                   h_ref, *, layer):
    shift = mod_ref[0, 0:1, :]
    wmod = nw_ref[...] * (1.0 + mod_ref[0, 1:2, :])

    def chunk(rows):
        h_ref[rows, :] = _modulated_norm(x_ref[0, rows, :], wmod, shift).astype(BF16)

    _for_row_chunks(INPROJ_TM, chunk)

    lb_rows = [lb_ref[r:r + 1, :] for r in range(lb_ref.shape[0])]
    lb_max = functools.reduce(jnp.maximum, lb_rows)
    lb_exp = [jnp.exp(r - lb_max) for r in lb_rows]
    lb_all = sum(lb_exp[:layer + 1]) / sum(lb_exp)

    tn, hw = INPROJ_TN, HGRN_WIDTH
    p16_col = {name: i * hw for i, name in enumerate(P16_BLOCKS)}

    def project(w_col):
        return _dot(h_ref[...], w_ref[:, w_col:w_col + tn])

    def to_p16(name, c0, value):
        o16_ref[0, :, p16_col[name] + c0:p16_col[name] + c0 + tn] = value.astype(BF16)

    for c0 in range(0, hw, tn):
        cs = slice(c0, c0 + tn)
        hq, hf, hg = project(c0), project(hw + c0), project(3 * hw + c0)
        followers = [lambda: to_p16("v", c0, project(2 * hw + c0)),
                     lambda: to_p16("aq", c0, project(4 * hw + c0))]
        for half in range(INPROJ_TM // HGRN_ROWS):
            rows = slice(half * HGRN_ROWS, (half + 1) * HGRN_ROWS)
            q_rel, k_rel, q_dec, k_dec, gate, decays = _hgrn_prepare(
                hq[rows], hf[rows], hg[rows], lb_all[:, cs], nwh_ref[:, cs], tril_ref[...])
            for name, value in (("q_rel", q_rel), ("k_rel", k_rel), ("q_dec", q_dec), ("k_dec", k_dec)):
                o16_ref[0, rows, p16_col[name] + c0:p16_col[name] + c0 + tn] = value
            gate_ref[0, rows, cs] = gate
            for u, decay in enumerate(decays):
                dec_ref[0, half * HGRN_UNROLL + u:half * HGRN_UNROLL + u + 1, cs] = decay
            n_groups = INPROJ_TM // HGRN_ROWS
            if (half + 1) % (n_groups // 2) == 0:
                followers[(half + 1) // (n_groups // 2) - 1]()
    for c0 in range(0, 2 * KV_WIDTH, tn):
        o16_ref[0, :, P16_KV_COL + c0:P16_KV_COL + c0 + tn] = project(4 * hw + ATTN_WIDTH + c0).astype(BF16)


def _inproj(x, mod, norm_w, w_bf, lower_bounds, hgrn_norm_w, layer):
    b, s, d = x.shape
    tm = INPROJ_TM
    assert tm % (2 * HGRN_ROWS) == 0 and (2 * KV_WIDTH) % INPROJ_TN == 0
    chunks = tm // HGRN_CHUNK
    tril3 = _hgrn_tril3()
    return pl.pallas_call(
        functools.partial(_inproj_kernel, layer=layer),
        grid=(b, s // tm),
        in_specs=[pl.BlockSpec((1, tm, d), lambda bi, m: (bi, m, 0)),
                  pl.BlockSpec((1, 6, d), lambda bi, m: (bi, 0, 0)),
                  pl.BlockSpec((1, d), lambda bi, m: (0, 0)),
                  pl.BlockSpec(w_bf.shape, lambda bi, m: (0, 0), pipeline_mode=pl.Buffered(1)),
                  pl.BlockSpec(lower_bounds.shape, lambda bi, m: (0, 0)),
                  pl.BlockSpec((1, HGRN_WIDTH), lambda bi, m: (0, 0)),
                  pl.BlockSpec(tril3.shape, lambda bi, m: (0, 0))],
        out_specs=[pl.BlockSpec((1, tm, HGRN_WIDTH), lambda bi, m: (bi, m, 0)),
                   pl.BlockSpec((1, tm, P16_WIDTH), lambda bi, m: (bi, m, 0)),
                   pl.BlockSpec((1, chunks, HGRN_WIDTH), lambda bi, m: (bi, m, 0))],
        out_shape=[jax.ShapeDtypeStruct((b, s, HGRN_WIDTH), F32),
                   jax.ShapeDtypeStruct((b, s, P16_WIDTH), BF16),
                   jax.ShapeDtypeStruct((b, s // HGRN_CHUNK, HGRN_WIDTH), F32)],
        scratch_shapes=[pltpu.VMEM((tm, d), BF16)],
        compiler_params=_params("parallel", "parallel"),
        name="inproj",
    )(x, mod, norm_w, w_bf, lower_bounds, hgrn_norm_w, tril3)


def _hgrn_kernel(v_ref, qrel_ref, krel_ref, qdec_ref, kdec_ref, gate_ref, dec_ref, wa_ref, wb_ref,
                 o_ref, wa16_ref, wb16_ref, st_ref):
    c, dk, nu = HGRN_CHUNK, HGRN_HEAD_DIM, HGRN_TRIP_CHUNKS
    wa16_ref[...] = wa_ref[...].astype(BF16)
    wb16_ref[...] = wb_ref[...].astype(BF16)

    @pl.when(pl.program_id(1) == 0)
    def _():
        st_ref[...] = jnp.zeros_like(st_ref)

    causal = (lax.broadcasted_iota(jnp.int32, (c, c), 0) >= lax.broadcasted_iota(jnp.int32, (c, c), 1))
    heads = [slice(h * dk, (h + 1) * dk) for h in range(HGRN_HEADS)]

    def body(i, carry):
        st = [st_ref[h] for h in range(HGRN_HEADS)]
        for u in range(nu):
            rows = pl.ds(pl.multiple_of((i * nu + u) * c, c), c)
            decay = dec_ref[0, pl.ds(i * nu + u, 1), :]
            a = [jnp.where(causal, _dot_nt(qrel_ref[0, rows, hs], krel_ref[0, rows, hs]), 0.0).astype(BF16)
                 for hs in heads]
            o = [_dot(a[h], v_ref[0, rows, hs]) + _dot_nt(qdec_ref[0, rows, hs], st[h].astype(BF16))
                 for h, hs in enumerate(heads)]
            st = [decay[:, hs] * st[h] + _dot_tn(v_ref[0, rows, hs], kdec_ref[0, rows, hs])
                  for h, hs in enumerate(heads)]
            for h, hs in enumerate(heads):
                oh = o[h] * lax.rsqrt(jnp.mean(o[h] * o[h], axis=-1, keepdims=True) + EPS)
                o_ref[0, rows, hs] = (oh * gate_ref[0, rows, hs]).astype(o_ref.dtype)
        for h in range(HGRN_HEADS):
            st_ref[h] = st[h]
        return carry

    lax.fori_loop(0, HGRN_TC // (nu * c), body, 0)


def _hgrn(gate, p16, dec, wa, wb):
    b, s, _ = gate.shape
    nt = s // HGRN_TC
    blk = lambda name: pl.BlockSpec((1, HGRN_TC, HGRN_WIDTH),
                                    lambda bi, t, j=P16_BLOCKS.index(name): (bi, t, j))
    wa_spec, wa_shape = _cast_rider_specs(wa, b * nt, nt)
    wb_spec, wb_shape = _cast_rider_specs(wb, b * nt, nt)
    return pl.pallas_call(
        _hgrn_kernel,
        grid=(b, nt),
        in_specs=[blk("v"), blk("q_rel"), blk("k_rel"), blk("q_dec"), blk("k_dec"),
                  pl.BlockSpec((1, HGRN_TC, HGRN_WIDTH), lambda bi, t: (bi, t, 0)),
                  pl.BlockSpec((1, HGRN_TC // HGRN_CHUNK, HGRN_WIDTH), lambda bi, t: (bi, t, 0)),
                  wa_spec, wb_spec],
        out_specs=[pl.BlockSpec((1, HGRN_TC, HGRN_WIDTH), lambda bi, t: (bi, t, 0)), wa_spec, wb_spec],
        out_shape=[jax.ShapeDtypeStruct((b, s, HGRN_WIDTH), BF16), wa_shape, wb_shape],
        scratch_shapes=[pltpu.VMEM((HGRN_HEADS, HGRN_HEAD_DIM, HGRN_HEAD_DIM), F32)],
        compiler_params=_params("arbitrary", "arbitrary"),
        name="hgrn",
    )(p16, p16, p16, p16, p16, gate, dec, wa, wb)


def _t5_causal_buckets(dist):
    max_exact = REL_BUCKETS // 2
    d = np.maximum(dist, 0)
    log_b = max_exact + (np.log(np.maximum(d, 1) / max_exact)
                         / np.log(REL_MAX_DIST / max_exact)
                         * (REL_BUCKETS - max_exact)).astype(np.int32)
    log_b = np.minimum(log_b, REL_BUCKETS - 1)
    return np.where(d < max_exact, d, log_b).astype(np.int32)


def _bias_kernel(tab_ref, sink_ref, bucket_ref, valid_ref, w_ref, o_ref, w16_ref):
    w16_ref[...] = w_ref[...].astype(BF16)
    h = pl.program_id(0)
    bucket = bucket_ref[...]
    acc = jnp.zeros(bucket.shape, F32)
    for bk in range(REL_BUCKETS):
        acc = jnp.where(bucket == bk, tab_ref[bk, h], acc)
    sink_col = lax.broadcasted_iota(jnp.int32, bucket.shape, 1) == 0
    for variant in range(2):
        masked = jnp.where(valid_ref[variant] != 0, acc, NEG_INF)
        o_ref[variant, 0] = jnp.where(sink_col, sink_ref[h], masked)


def _bias_table(rel_table, sinks, w):
    l = ATTN_BLOCK
    qi = np.arange(l)[:, None]
    kj = np.arange(2 * l)[None, :]
    dist = qi + l - kj
    in_window = (dist >= 0) & (dist < WINDOW)
    valid = np.stack([in_window & (kj >= l), in_window]).astype(np.int32)
    assert not valid[:, :, 0].any()
    bucket = _t5_causal_buckets(dist)
    w_rows = w.shape[0] // ATTN_Q_HEADS
    assert w_rows * ATTN_Q_HEADS == w.shape[0] and w_rows % BF16_TILE_ROWS == 0
    w_spec = pl.BlockSpec((w_rows, w.shape[1]), lambda h: (h, 0))
    return pl.pallas_call(
        _bias_kernel,
        grid=(ATTN_Q_HEADS,),
        in_specs=[pl.BlockSpec(memory_space=pltpu.SMEM),
                  pl.BlockSpec(memory_space=pltpu.SMEM),
                  pl.BlockSpec((l, 2 * l), lambda h: (0, 0)),
                  pl.BlockSpec((2, l, 2 * l), lambda h: (0, 0, 0)),
                  w_spec],
        out_specs=[pl.BlockSpec((2, 1, l, 2 * l), lambda h: (0, h, 0, 0)), w_spec],
        out_shape=[jax.ShapeDtypeStruct((2, ATTN_Q_HEADS, l, 2 * l), F32),
                   jax.ShapeDtypeStruct(w.shape, BF16)],
        compiler_params=_params("arbitrary"),
        name="bias",
    )(rel_table, sinks, jnp.asarray(bucket), jnp.asarray(valid), w)


def _zero_first_row(a):
    top = a[:BF16_TILE_ROWS]
    row = lax.broadcasted_iota(jnp.int32, top.shape, 0)
    return jnp.concatenate([jnp.where(row == 0, 0.0, top).astype(a.dtype), a[BF16_TILE_ROWS:]], axis=0)


def _attn_kernel(q_ref, kp_ref, kc_ref, vp_ref, vc_ref, bias_ref, wa_ref, wb_ref,
                 o_ref, wa16_ref, wb16_ref):
    l = ATTN_BLOCK
    dh = ATTN_HEAD_DIM
    wa16_ref[...] = wa_ref[...].astype(BF16)
    wb16_ref[...] = wb_ref[...].astype(BF16)
    lane = lax.broadcasted_iota(jnp.int32, (1, 2 * dh), 1)
    low = lane < dh
    first = jnp.minimum(pl.program_id(1), 1)

    for pair in range(ATTN_KV_HEADS // 2):
        pc = slice(pair * 2 * dh, (pair + 1) * 2 * dh)
        kk = pltpu.bitcast(jnp.concatenate([kp_ref[0, :, pc], kc_ref[0, :, pc]], axis=0), jnp.int32)
        vv = pltpu.bitcast(jnp.concatenate([vp_ref[0, :, pc], vc_ref[0, :, pc]], axis=0), jnp.int32)
        kk_sw = pltpu.roll(kk, dh, axis=1)
        vv_sw = pltpu.roll(vv, dh, axis=1)
        for sub in range(2):
            kvh = pair * 2 + sub
            if sub == 0:
                k2_all = jnp.where(low, kk, kk_sw)
                v2_all = jnp.where(low, vv, vv_sw)
            else:
                k2_all = jnp.where(low, kk_sw, kk)
                v2_all = jnp.where(low, vv_sw, vv)
            k2_all = pltpu.bitcast(k2_all, BF16)
            v2_all = pltpu.bitcast(v2_all, BF16)
            for t in range(ATTN_QB):
                qrows = slice(t * l, (t + 1) * l)
                k2 = _zero_first_row(k2_all[t * l:(t + 2) * l])
                v2 = _zero_first_row(v2_all[t * l:(t + 2) * l])
                qs = []
                for j in range(ATTN_GROUP):
                    hq = kvh * ATTN_GROUP + j
                    qc = slice((hq // 2) * 2 * dh, (hq // 2 + 1) * 2 * dh)
                    q2 = q_ref[0, qrows, qc] * (dh ** -0.5)
                    keep = low if hq % 2 == 0 else jnp.logical_not(low)
                    qs.append(jnp.where(keep, q2, 0.0).astype(BF16))
                s_all = _dot_nt(jnp.concatenate(qs, axis=0), k2)
                ps = []
                for j in range(ATTN_GROUP):
                    hq = kvh * ATTN_GROUP + j
                    bias = bias_ref[first, hq] if t == 0 else bias_ref[1, hq]
                    s = s_all[j * l:(j + 1) * l] + bias
                    p = jnp.exp(s - jnp.max(s, axis=-1, keepdims=True))
                    ps.append(p.astype(BF16))
                v2x = jnp.concatenate([v2, jnp.ones_like(v2)], axis=1)
                o_all = _dot(jnp.concatenate(ps, axis=0), v2x)
                o_all = o_all[:, :2 * dh] / o_all[:, 2 * dh:]
                for jp in range(ATTN_GROUP // 2):
                    hq = kvh * ATTN_GROUP + 2 * jp
                    oe = o_all[(2 * jp) * l:(2 * jp + 1) * l]
                    oo = o_all[(2 * jp + 1) * l:(2 * jp + 2) * l]
                    oc = slice((hq // 2) * 2 * dh, (hq // 2 + 1) * 2 * dh)
                    o_ref[0, qrows, oc] = jnp.where(low, oe, oo).astype(o_ref.dtype)


def _cast_rider_specs(w, steps, steps_per_batch):
    rows = w.shape[0] // steps
    assert rows * steps == w.shape[0] and rows % BF16_TILE_ROWS == 0
    spec = pl.BlockSpec((rows, w.shape[1]), lambda bi, n: (bi * steps_per_batch + n, 0))
    return spec, jax.ShapeDtypeStruct(w.shape, BF16)


def _attn(p16, bias, wa, wb):
    b, s, _ = p16.shape
    l = ATTN_BLOCK
    tq = ATTN_QB * l
    nq = s // tq
    q_blk = P16_BLOCKS.index("aq")
    k_blk = P16_KV_COL // KV_WIDTH
    v_blk = k_blk + 1
    prev_spec = lambda blk: pl.BlockSpec(
        (1, l, KV_WIDTH), lambda bi, n: (bi, jnp.maximum(n * ATTN_QB - 1, 0), blk))
    cur_spec = lambda blk: pl.BlockSpec((1, tq, KV_WIDTH), lambda bi, n: (bi, n, blk))
    wa_spec, wa_shape = _cast_rider_specs(wa, b * nq, nq)
    wb_spec, wb_shape = _cast_rider_specs(wb, b * nq, nq)
    return pl.pallas_call(
        _attn_kernel,
        grid=(b, nq),
        in_specs=[pl.BlockSpec((1, tq, ATTN_WIDTH), lambda bi, n: (bi, n, q_blk)),
                  prev_spec(k_blk), cur_spec(k_blk), prev_spec(v_blk), cur_spec(v_blk),
                  pl.BlockSpec(bias.shape, lambda bi, n: (0, 0, 0, 0), pipeline_mode=pl.Buffered(1)),
                  wa_spec, wb_spec],
        out_specs=[pl.BlockSpec((1, tq, ATTN_WIDTH), lambda bi, n: (bi, n, 0)), wa_spec, wb_spec],
        out_shape=[jax.ShapeDtypeStruct((b, s, ATTN_WIDTH), BF16), wa_shape, wb_shape],
        compiler_params=_params("arbitrary", "arbitrary"),
        name="attn",
    )(p16, p16, p16, p16, p16, bias, wa, wb)


def _outproj_kernel(orec_ref, oatt_ref, x_ref, mod_ref, nw_ref, w_ref, x1_ref, h2_ref, rs_ref):
    ssq = None
    for c0 in range(0, D_MODEL, OUTPROJ_TN):
        cs = slice(c0, c0 + OUTPROJ_TN)
        y = _dot(orec_ref[0], w_ref[0:HGRN_WIDTH, cs]) + _dot(oatt_ref[0], w_ref[HGRN_WIDTH:, cs])
        x1 = x_ref[0, :, cs] + mod_ref[0, 2:3, cs] * y
        x1_ref[0, :, cs] = x1
        ssq = _lane_folded_sumsq(ssq, x1)
    rs_ref[...] = _rsqrt_mean(ssq)

    shift = mod_ref[0, 3:4, :]
    wmod = nw_ref[...] * (1.0 + mod_ref[0, 4:5, :])

    def chunk(rows):
        h2_ref[0, rows, :] = (x1_ref[0, rows, :] * rs_ref[rows, :] * wmod + shift).astype(BF16)

    _for_row_chunks(OUTPROJ_TM, chunk)


def _outproj(o_rec, o_att, x, mod, norm_w, w_bf):
    b, s, d = x.shape
    tm = OUTPROJ_TM
    return pl.pallas_call(
        _outproj_kernel,
        grid=(b, s // tm),
        in_specs=[pl.BlockSpec((1, tm, HGRN_WIDTH), lambda bi, m: (bi, m, 0)),
                  pl.BlockSpec((1, tm, ATTN_WIDTH), lambda bi, m: (bi, m, 0)),
                  pl.BlockSpec((1, tm, d), lambda bi, m: (bi, m, 0)),
                  pl.BlockSpec((1, 6, d), lambda bi, m: (bi, 0, 0)),
                  pl.BlockSpec((1, d), lambda bi, m: (0, 0)),
                  pl.BlockSpec(w_bf.shape, lambda bi, m: (0, 0), pipeline_mode=pl.Buffered(1))],
        out_specs=[pl.BlockSpec((1, tm, d), lambda bi, m: (bi, m, 0)),
                   pl.BlockSpec((1, tm, d), lambda bi, m: (bi, m, 0))],
        out_shape=[jax.ShapeDtypeStruct((b, s, d), F32),
                   jax.ShapeDtypeStruct((b, s, d), BF16)],
        scratch_shapes=[pltpu.VMEM((tm, 1), F32)],
        compiler_params=_params("parallel", "parallel"),
        name="outproj",
    )(o_rec, o_att, x, mod, norm_w, w_bf)


def _ffn_kernel(h_ref, x1_hbm, mod_ref, fw_ref, wg_ref, wu_ref, wd_ref, o_ref, x1_buf, rs_ref, x1_sem,
                *, final_norm):
    bi, m, j = pl.program_id(0), pl.program_id(1), pl.program_id(2)

    def x1_copy():
        rows = pl.ds(pl.multiple_of(m * FFN_TM, FFN_TM), FFN_TM)
        return pltpu.make_async_copy(x1_hbm.at[bi, rows, :], x1_buf, x1_sem)

    last = pl.num_programs(2) - 1
    ksubs = [slice(k0, k0 + FFN_SUB) for k0 in range(0, FFN_TF, FFN_SUB)]
    pieces = [slice(c0, c0 + FFN_DN) for c0 in range(0, D_MODEL, FFN_DN)]

    def acts():
        return [(_silu(_dot(h_ref[0], wg_ref[:, ks])) * _dot(h_ref[0], wu_ref[:, ks])).astype(BF16)
                for ks in ksubs]

    def down(a, cs):
        return functools.reduce(lambda x, y: x + y, [_dot(ai, wd_ref[ks, cs]) for ai, ks in zip(a, ksubs)])

    @pl.when(j == 0)
    def _():
        x1_copy().start()
        a = acts()
        for cs in pieces:
            o_ref[0, :, cs] = down(a, cs)

    @pl.when(jnp.logical_and(j > 0, j < last))
    def _():
        a = acts()
        for cs in pieces:
            o_ref[0, :, cs] += down(a, cs)

    @pl.when(j == last)
    def _():
        x1_copy().wait()
        a = acts()
        ssq = None
        for cs in pieces:
            x2 = x1_buf[:, cs] + mod_ref[0, 5:6, cs] * (o_ref[0, :, cs] + down(a, cs))
            x1_buf[:, cs] = x2
            if final_norm:
                ssq = _lane_folded_sumsq(ssq, x2)
        if final_norm:
            rs_ref[...] = _rsqrt_mean(ssq)
        fw = fw_ref[...]

        def chunk(rows):
            x2 = x1_buf[rows, :]
            o_ref[0, rows, :] = x2 * rs_ref[rows, :] * fw if final_norm else x2

        _for_row_chunks(FFN_TM, chunk, loop_trips=2)


def _ffn(h2, x1, mod, final_w, wg, wu, wd, final_norm):
    b, s, d = x1.shape
    tm, tf = FFN_TM, FFN_TF
    return pl.pallas_call(
        functools.partial(_ffn_kernel, final_norm=final_norm),
        grid=(b, s // tm, D_FF // tf),
        in_specs=[pl.BlockSpec((1, tm, d), lambda bi, m, j: (bi, m, 0)),
                  pl.BlockSpec(memory_space=pl.ANY),
                  pl.BlockSpec((1, 6, d), lambda bi, m, j: (bi, 0, 0)),
                  pl.BlockSpec((1, d), lambda bi, m, j: (0, 0)),
                  pl.BlockSpec((d, tf), lambda bi, m, j: (0, j)),
                  pl.BlockSpec((d, tf), lambda bi, m, j: (0, j)),
                  pl.BlockSpec((tf, d), lambda bi, m, j: (j, 0))],
        out_specs=pl.BlockSpec((1, tm, d), lambda bi, m, j: (bi, m, 0)),
        out_shape=jax.ShapeDtypeStruct((b, s, d), F32),
        scratch_shapes=[pltpu.VMEM((tm, d), F32), pltpu.VMEM((tm, 1), F32), pltpu.SemaphoreType.DMA(())],
        compiler_params=_params("arbitrary", "arbitrary", "arbitrary"),
        name="ffn",
    )(h2, x1, mod, final_w, wg, wu, wd)


def kernel(x, c, w_ada, b_ada, norm1_w, w_in, lower_bounds, hgrn_norm_w, attn_sinks,
           rel_bias_table, w_out, norm2_w, w_gate, w_up, w_down, final_norm_w):
    b, s, d = x.shape
    assert (d, w_in.shape[-1], w_gate.shape[-1]) == (D_MODEL, IN_WIDTH, D_FF)
    assert s % max(INPROJ_TM, HGRN_TC, OUTPROJ_TM, FFN_TM, ATTN_QB * ATTN_BLOCK) == 0
    depth = w_ada.shape[0]
    c8 = jnp.pad(c, ((0, 8 - b), (0, 0)))
    for layer in range(depth):
        bias, wi16 = _bias_table(rel_bias_table, attn_sinks[layer], w_in[layer])
        mod = _ada(c8, w_ada[layer], b_ada[layer][None, :])[:b].reshape(b, 6, d)
        gate, p16, dec = _inproj(x, mod, norm1_w[layer][None, :], wi16, lower_bounds,
                                 hgrn_norm_w[layer][None, :], layer)
        o_rec, wd16, wo16 = _hgrn(gate, p16, dec, w_down[layer], w_out[layer])
        o_att, wg16, wu16 = _attn(p16, bias, w_gate[layer], w_up[layer])
        x1, h2 = _outproj(o_rec, o_att, x, mod, norm2_w[layer][None, :], wo16)
        x = _ffn(h2, x1, mod, final_norm_w[None, :], wg16, wu16, wd16,
                 final_norm=(layer == depth - 1))
    return x
```

```python
import functools

import numpy as np
import jax
import jax.numpy as jnp
from jax import lax
from jax.experimental import pallas as pl
from jax.experimental.pallas import tpu as pltpu

D_MODEL = 2048
HGRN_WIDTH = 1024
HGRN_HEAD_DIM = 128
HGRN_HEADS = HGRN_WIDTH // HGRN_HEAD_DIM
HGRN_CHUNK = 64
ATTN_WIDTH = 1024
ATTN_HEAD_DIM = 64
ATTN_Q_HEADS = ATTN_WIDTH // ATTN_HEAD_DIM
ATTN_KV_HEADS = 4
ATTN_GROUP = ATTN_Q_HEADS // ATTN_KV_HEADS
WINDOW = 128
ATTN_BLOCK = 128
ATTN_QB = 4
REL_BUCKETS = 32
REL_MAX_DIST = 128
D_FF = 5632
KV_WIDTH = ATTN_KV_HEADS * ATTN_HEAD_DIM
IN_WIDTH = 4 * HGRN_WIDTH + ATTN_WIDTH + 2 * KV_WIDTH
P16_BLOCKS = ("v", "aq", "q_rel", "k_rel", "q_dec", "k_dec")
P16_KV_COL = len(P16_BLOCKS) * HGRN_WIDTH
P16_WIDTH = P16_KV_COL + 2 * KV_WIDTH
EPS = 1e-6
NEG_INF = -1e30
LOG2E = 1.4426950408889634

F32 = jnp.float32
BF16 = jnp.bfloat16

VMEM_LIMIT_BYTES = 56 * 1024 * 1024
INPROJ_VMEM_LIMIT_BYTES = 61 * 1024 * 1024
LANES = 128
SUBLANES = 8
BF16_TILE_ROWS = 2 * SUBLANES

ADA_TN = 1024
INPROJ_TM = 512
INPROJ_TN = 256
HGRN_TC = 1024
HGRN_UNROLL = 4
HGRN_ROWS = HGRN_UNROLL * HGRN_CHUNK
HGRN_TRIP_CHUNKS = 16
OUTPROJ_TM = 512
OUTPROJ_TN = 256
FFN_TM = 1024
FFN_TF = 512
FFN_SUB = 256
FFN_DN = 512
ROW_CHUNK = 16


def _params(*semantics, vmem_limit_bytes=VMEM_LIMIT_BYTES):
    return pltpu.CompilerParams(dimension_semantics=semantics, vmem_limit_bytes=vmem_limit_bytes)


def _sigmoid(v):
    return 1.0 / (1.0 + jnp.exp2(v * (-LOG2E)))


def _silu(v):
    return v * _sigmoid(v)


def _dot(a, b):
    return jnp.dot(a, b, preferred_element_type=F32)


def _dot_nt(a, b):
    return lax.dot_general(a, b, (((1,), (1,)), ((), ())), preferred_element_type=F32)


def _dot_tn(a, b):
    return lax.dot_general(a, b, (((0,), (0,)), ((), ())), preferred_element_type=F32)


def _ada_kernel(c_ref, w_ref, b_ref, o_ref):
    c_act = _silu(c_ref[...])
    o_ref[...] = _dot(c_act.astype(BF16), w_ref[...].astype(BF16)) + b_ref[...]


def _ada(c8, w, b):
    n = w.shape[1]
    return pl.pallas_call(
        _ada_kernel,
        grid=(n // ADA_TN,),
        in_specs=[pl.BlockSpec((8, D_MODEL), lambda j: (0, 0)),
                  pl.BlockSpec((D_MODEL, ADA_TN), lambda j: (0, j)),
                  pl.BlockSpec((1, ADA_TN), lambda j: (0, j))],
        out_specs=pl.BlockSpec((8, ADA_TN), lambda j: (0, j)),
        out_shape=jax.ShapeDtypeStruct((8, n), F32),
        compiler_params=_params("arbitrary"),
        name="ada",
    )(c8, w, b)


def _modulated_norm(x, wmod, shift):
    return x * lax.rsqrt(jnp.mean(x * x, axis=-1, keepdims=True) + EPS) * wmod + shift


def _for_row_chunks(n_rows, fn, loop_trips=1):
    per_trip = n_rows // loop_trips
    assert per_trip * loop_trips == n_rows and per_trip % ROW_CHUNK == 0

    def trip(i, carry):
        for r0 in range(0, per_trip, ROW_CHUNK):
            start = r0 if loop_trips == 1 else pl.multiple_of(i * per_trip + r0, ROW_CHUNK)
            fn(pl.ds(start, ROW_CHUNK))
        return carry

    if loop_trips == 1:
        trip(0, 0)
    else:
        lax.fori_loop(0, loop_trips, trip, 0)


def _lane_folded_sumsq(ssq, v):
    sq = v * v
    for k0 in range(0, v.shape[1], LANES):
        ssq = sq[:, k0:k0 + LANES] if ssq is None else ssq + sq[:, k0:k0 + LANES]
    return ssq


def _rsqrt_mean(ssq):
    return lax.rsqrt(jnp.sum(ssq, axis=-1, keepdims=True) * (1.0 / D_MODEL) + EPS)


def _split3(v):
    hi = v.astype(BF16)
    r1 = v - hi.astype(F32)
    mid = r1.astype(BF16)
    lo = (r1 - mid.astype(F32)).astype(BF16)
    return hi, mid, lo


def _hgrn_tril3():
    tril = np.tril(np.ones((HGRN_CHUNK, HGRN_CHUNK), np.float32))
    return jnp.asarray(np.concatenate([tril, tril, tril], axis=1), dtype=BF16)


def _hgrn_prepare(q, f_logit, g, lb, nw, tril3):
    c, nu = HGRN_CHUNK, HGRN_UNROLL

    def per_chunk_rows(x, offset):
        rid = lax.broadcasted_iota(jnp.int32, x.shape, 0)
        out = jnp.broadcast_to(x[offset:offset + 1, :], x.shape)
        for u in range(1, nu):
            out = jnp.where(rid >= u * c, jnp.broadcast_to(x[u * c + offset:u * c + offset + 1, :], x.shape), out)
        return out

    f = lb + (1.0 - lb) * _sigmoid(f_logit)
    k = 1.0 - f
    parts = _split3(jnp.log2(f))
    b = jnp.concatenate(
        [_dot(tril3, jnp.concatenate([p[u * c:(u + 1) * c] for p in parts], axis=0)) for u in range(nu)],
        axis=0)
    b_mid = per_chunk_rows(b, c // 2 - 1)
    b_last = per_chunk_rows(b, c - 1)
    d_mid = b - b_mid
    qr = q * jnp.exp2(d_mid)
    kr = k * jnp.exp2(-d_mid)
    q_dec = (qr * jnp.exp2(b_mid)).astype(BF16)
    k_dec = (kr * jnp.exp2(b_last - b_mid)).astype(BF16)
    decays = [jnp.exp2(b[(u + 1) * c - 1:(u + 1) * c, :]) for u in range(nu)]
    return qr.astype(BF16), kr.astype(BF16), q_dec, k_dec, nw * _silu(g), decays


def _inproj_kernel(x_ref, mod_ref, nw_ref, w_ref, lb_ref, nwh_ref, tril_ref, wa_ref, wb_ref,
                   gate_ref, o16_ref, dec_ref, wa16_ref, wb16_ref, h_ref, *, layer):
    wa16_ref[...] = wa_ref[...].astype(BF16)
    wb16_ref[...] = wb_ref[...].astype(BF16)
    shift = mod_ref[0, 0:1, :]
    wmod = nw_ref[...] * (1.0 + mod_ref[0, 1:2, :])

    def chunk(rows):
        h_ref[rows, :] = _modulated_norm(x_ref[0, rows, :], wmod, shift).astype(BF16)

    _for_row_chunks(INPROJ_TM, chunk)

    lb_rows = [lb_ref[r:r + 1, :] for r in range(lb_ref.shape[0])]
    lb_max = functools.reduce(jnp.maximum, lb_rows)
    lb_exp = [jnp.exp(r - lb_max) for r in lb_rows]
    lb_all = sum(lb_exp[:layer + 1]) / sum(lb_exp)

    tn, hw = INPROJ_TN, HGRN_WIDTH
    p16_col = {name: i * hw for i, name in enumerate(P16_BLOCKS)}

    def project(w_col):
        return _dot(h_ref[...], w_ref[:, w_col:w_col + tn])

    def to_p16(name, c0, value):
        o16_ref[0, :, p16_col[name] + c0:p16_col[name] + c0 + tn] = value.astype(BF16)

    for c0 in range(0, hw, tn):
        cs = slice(c0, c0 + tn)
        hq, hf, hg = project(c0), project(hw + c0), project(3 * hw + c0)
        followers = [lambda: to_p16("v", c0, project(2 * hw + c0)),
                     lambda: to_p16("aq", c0, project(4 * hw + c0))]
        for half in range(INPROJ_TM // HGRN_ROWS):
            rows = slice(half * HGRN_ROWS, (half + 1) * HGRN_ROWS)
            q_rel, k_rel, q_dec, k_dec, gate, decays = _hgrn_prepare(
                hq[rows], hf[rows], hg[rows], lb_all[:, cs], nwh_ref[:, cs], tril_ref[...])
            for name, value in (("q_rel", q_rel), ("k_rel", k_rel), ("q_dec", q_dec), ("k_dec", k_dec)):
                o16_ref[0, rows, p16_col[name] + c0:p16_col[name] + c0 + tn] = value
            gate_ref[0, rows, cs] = gate
            for u, decay in enumerate(decays):
                dec_ref[0, half * HGRN_UNROLL + u:half * HGRN_UNROLL + u + 1, cs] = decay
            n_groups = INPROJ_TM // HGRN_ROWS
            if (half + 1) % (n_groups // 2) == 0:
                followers[(half + 1) // (n_groups // 2) - 1]()
    for c0 in range(0, 2 * KV_WIDTH, tn):
        o16_ref[0, :, P16_KV_COL + c0:P16_KV_COL + c0 + tn] = project(4 * hw + ATTN_WIDTH + c0).astype(BF16)


def _inproj(x, mod, norm_w, w_bf, lower_bounds, hgrn_norm_w, layer, wa, wb):
    b, s, d = x.shape
    tm = INPROJ_TM
    nt = s // tm
    assert tm % (2 * HGRN_ROWS) == 0 and (2 * KV_WIDTH) % INPROJ_TN == 0
    chunks = tm // HGRN_CHUNK
    tril3 = _hgrn_tril3()
    wa_spec, wa_shape = _cast_rider_specs(wa, b * nt, nt)
    wb_spec, wb_shape = _cast_rider_specs(wb, b * nt, nt)
    return pl.pallas_call(
        functools.partial(_inproj_kernel, layer=layer),
        grid=(b, s // tm),
        in_specs=[pl.BlockSpec((1, tm, d), lambda bi, m: (bi, m, 0)),
                  pl.BlockSpec((1, 6, d), lambda bi, m: (bi, 0, 0)),
                  pl.BlockSpec((1, d), lambda bi, m: (0, 0)),
                  pl.BlockSpec(w_bf.shape, lambda bi, m: (0, 0), pipeline_mode=pl.Buffered(1)),
                  pl.BlockSpec(lower_bounds.shape, lambda bi, m: (0, 0)),
                  pl.BlockSpec((1, HGRN_WIDTH), lambda bi, m: (0, 0)),
                  pl.BlockSpec(tril3.shape, lambda bi, m: (0, 0)),
                  wa_spec, wb_spec],
        out_specs=[pl.BlockSpec((1, tm, HGRN_WIDTH), lambda bi, m: (bi, m, 0)),
                   pl.BlockSpec((1, tm, P16_WIDTH), lambda bi, m: (bi, m, 0)),
                   pl.BlockSpec((1, chunks, HGRN_WIDTH), lambda bi, m: (bi, m, 0)),
                   wa_spec, wb_spec],
        out_shape=[jax.ShapeDtypeStruct((b, s, HGRN_WIDTH), F32),
                   jax.ShapeDtypeStruct((b, s, P16_WIDTH), BF16),
                   jax.ShapeDtypeStruct((b, s // HGRN_CHUNK, HGRN_WIDTH), F32),
                   wa_shape, wb_shape],
        scratch_shapes=[pltpu.VMEM((tm, d), BF16)],
        compiler_params=_params("arbitrary", "arbitrary", vmem_limit_bytes=INPROJ_VMEM_LIMIT_BYTES),
        name="inproj",
    )(x, mod, norm_w, w_bf, lower_bounds, hgrn_norm_w, tril3, wa, wb)


def _hgrn_kernel(v_ref, qrel_ref, krel_ref, qdec_ref, kdec_ref, gate_ref, dec_ref, o_ref, st_ref):
    c, dk, nu = HGRN_CHUNK, HGRN_HEAD_DIM, HGRN_TRIP_CHUNKS

    @pl.when(pl.program_id(1) == 0)
    def _():
        st_ref[...] = jnp.zeros_like(st_ref)

    causal = (lax.broadcasted_iota(jnp.int32, (c, c), 0) >= lax.broadcasted_iota(jnp.int32, (c, c), 1))
    heads = [slice(h * dk, (h + 1) * dk) for h in range(HGRN_HEADS)]

    def body(i, carry):
        st = [st_ref[h] for h in range(HGRN_HEADS)]
        for u in range(nu):
            rows = pl.ds(pl.multiple_of((i * nu + u) * c, c), c)
            decay = dec_ref[0, pl.ds(i * nu + u, 1), :]
            a = [jnp.where(causal, _dot_nt(qrel_ref[0, rows, hs], krel_ref[0, rows, hs]), 0.0).astype(BF16)
                 for hs in heads]
            o = [_dot(a[h], v_ref[0, rows, hs]) + _dot_nt(qdec_ref[0, rows, hs], st[h].astype(BF16))
                 for h, hs in enumerate(heads)]
            st = [decay[:, hs] * st[h] + _dot_tn(v_ref[0, rows, hs], kdec_ref[0, rows, hs])
                  for h, hs in enumerate(heads)]
            for h, hs in enumerate(heads):
                oh = o[h] * lax.rsqrt(jnp.mean(o[h] * o[h], axis=-1, keepdims=True) + EPS)
                o_ref[0, rows, hs] = (oh * gate_ref[0, rows, hs]).astype(o_ref.dtype)
        for h in range(HGRN_HEADS):
            st_ref[h] = st[h]
        return carry

    lax.fori_loop(0, HGRN_TC // (nu * c), body, 0)


def _hgrn(gate, p16, dec):
    b, s, _ = gate.shape
    nt = s // HGRN_TC
    blk = lambda name: pl.BlockSpec((1, HGRN_TC, HGRN_WIDTH),
                                    lambda bi, t, j=P16_BLOCKS.index(name): (bi, t, j))
    return pl.pallas_call(
        _hgrn_kernel,
        grid=(b, nt),
        in_specs=[blk("v"), blk("q_rel"), blk("k_rel"), blk("q_dec"), blk("k_dec"),
                  pl.BlockSpec((1, HGRN_TC, HGRN_WIDTH), lambda bi, t: (bi, t, 0)),
                  pl.BlockSpec((1, HGRN_TC // HGRN_CHUNK, HGRN_WIDTH), lambda bi, t: (bi, t, 0))],
        out_specs=pl.BlockSpec((1, HGRN_TC, HGRN_WIDTH), lambda bi, t: (bi, t, 0)),
        out_shape=jax.ShapeDtypeStruct((b, s, HGRN_WIDTH), BF16),
        scratch_shapes=[pltpu.VMEM((HGRN_HEADS, HGRN_HEAD_DIM, HGRN_HEAD_DIM), F32)],
        compiler_params=_params("arbitrary", "arbitrary"),
        name="hgrn",
    )(p16, p16, p16, p16, p16, gate, dec)


def _t5_causal_buckets(dist):
    max_exact = REL_BUCKETS // 2
    d = np.maximum(dist, 0)
    log_b = max_exact + (np.log(np.maximum(d, 1) / max_exact)
                         / np.log(REL_MAX_DIST / max_exact)
                         * (REL_BUCKETS - max_exact)).astype(np.int32)
    log_b = np.minimum(log_b, REL_BUCKETS - 1)
    return np.where(d < max_exact, d, log_b).astype(np.int32)


def _bias_kernel(tab_ref, sink_ref, bucket_ref, valid_ref, w_ref, o_ref, w16_ref):
    w16_ref[...] = w_ref[...].astype(BF16)
    h = pl.program_id(0)
    bucket = bucket_ref[...]
    acc = jnp.zeros(bucket.shape, F32)
    for bk in range(REL_BUCKETS):
        acc = jnp.where(bucket == bk, tab_ref[bk, h], acc)
    sink_col = lax.broadcasted_iota(jnp.int32, bucket.shape, 1) == 0
    for variant in range(2):
        masked = jnp.where(valid_ref[variant] != 0, acc, NEG_INF)
        o_ref[variant, 0] = jnp.where(sink_col, sink_ref[h], masked)


def _bias_table(rel_table, sinks, w):
    l = ATTN_BLOCK
    qi = np.arange(l)[:, None]
    kj = np.arange(2 * l)[None, :]
    dist = qi + l - kj
    in_window = (dist >= 0) & (dist < WINDOW)
    valid = np.stack([in_window & (kj >= l), in_window]).astype(np.int32)
    assert not valid[:, :, 0].any()
    bucket = _t5_causal_buckets(dist)
    w_rows = w.shape[0] // ATTN_Q_HEADS
    assert w_rows * ATTN_Q_HEADS == w.shape[0] and w_rows % BF16_TILE_ROWS == 0
    w_spec = pl.BlockSpec((w_rows, w.shape[1]), lambda h: (h, 0))
    return pl.pallas_call(
        _bias_kernel,
        grid=(ATTN_Q_HEADS,),
        in_specs=[pl.BlockSpec(memory_space=pltpu.SMEM),
                  pl.BlockSpec(memory_space=pltpu.SMEM),
                  pl.BlockSpec((l, 2 * l), lambda h: (0, 0)),
                  pl.BlockSpec((2, l, 2 * l), lambda h: (0, 0, 0)),
                  w_spec],
        out_specs=[pl.BlockSpec((2, 1, l, 2 * l), lambda h: (0, h, 0, 0)), w_spec],
        out_shape=[jax.ShapeDtypeStruct((2, ATTN_Q_HEADS, l, 2 * l), F32),
                   jax.ShapeDtypeStruct(w.shape, BF16)],
        compiler_params=_params("arbitrary"),
        name="bias",
    )(rel_table, sinks, jnp.asarray(bucket), jnp.asarray(valid), w)


def _zero_first_row(a):
    top = a[:BF16_TILE_ROWS]
    row = lax.broadcasted_iota(jnp.int32, top.shape, 0)
    return jnp.concatenate([jnp.where(row == 0, 0.0, top).astype(a.dtype), a[BF16_TILE_ROWS:]], axis=0)


def _attn_kernel(q_ref, kp_ref, kc_ref, vp_ref, vc_ref, bias_ref, wa_ref, wb_ref,
                 o_ref, wa16_ref, wb16_ref):
    l = ATTN_BLOCK
    dh = ATTN_HEAD_DIM
    wa16_ref[...] = wa_ref[...].astype(BF16)
    wb16_ref[...] = wb_ref[...].astype(BF16)
    lane = lax.broadcasted_iota(jnp.int32, (1, 2 * dh), 1)
    low = lane < dh
    first = jnp.minimum(pl.program_id(1), 1)

    for pair in range(ATTN_KV_HEADS // 2):
        pc = slice(pair * 2 * dh, (pair + 1) * 2 * dh)
        kk = pltpu.bitcast(jnp.concatenate([kp_ref[0, :, pc], kc_ref[0, :, pc]], axis=0), jnp.int32)
        vv = pltpu.bitcast(jnp.concatenate([vp_ref[0, :, pc], vc_ref[0, :, pc]], axis=0), jnp.int32)
        kk_sw = pltpu.roll(kk, dh, axis=1)
        vv_sw = pltpu.roll(vv, dh, axis=1)
        for sub in range(2):
            kvh = pair * 2 + sub
            if sub == 0:
                k2_all = jnp.where(low, kk, kk_sw)
                v2_all = jnp.where(low, vv, vv_sw)
            else:
                k2_all = jnp.where(low, kk_sw, kk)
                v2_all = jnp.where(low, vv_sw, vv)
            k2_all = pltpu.bitcast(k2_all, BF16)
            v2_all = pltpu.bitcast(v2_all, BF16)
            for t in range(ATTN_QB):
                qrows = slice(t * l, (t + 1) * l)
                k2 = _zero_first_row(k2_all[t * l:(t + 2) * l])
                v2 = _zero_first_row(v2_all[t * l:(t + 2) * l])
                qs = []
                for j in range(ATTN_GROUP):
                    hq = kvh * ATTN_GROUP + j
                    qc = slice((hq // 2) * 2 * dh, (hq // 2 + 1) * 2 * dh)
                    q2 = q_ref[0, qrows, qc] * (dh ** -0.5)
                    keep = low if hq % 2 == 0 else jnp.logical_not(low)
                    qs.append(jnp.where(keep, q2, 0.0).astype(BF16))
                s_all = _dot_nt(jnp.concatenate(qs, axis=0), k2)
                ps = []
                for j in range(ATTN_GROUP):
                    hq = kvh * ATTN_GROUP + j
                    bias = bias_ref[first, hq] if t == 0 else bias_ref[1, hq]
                    s = s_all[j * l:(j + 1) * l] + bias
                    p = jnp.exp(s - jnp.max(s, axis=-1, keepdims=True))
                    ps.append(p.astype(BF16))
                v2x = jnp.concatenate([v2, jnp.ones_like(v2)], axis=1)
                o_all = _dot(jnp.concatenate(ps, axis=0), v2x)
                o_all = o_all[:, :2 * dh] / o_all[:, 2 * dh:]
                for jp in range(ATTN_GROUP // 2):
                    hq = kvh * ATTN_GROUP + 2 * jp
                    oe = o_all[(2 * jp) * l:(2 * jp + 1) * l]
                    oo = o_all[(2 * jp + 1) * l:(2 * jp + 2) * l]
                    oc = slice((hq // 2) * 2 * dh, (hq // 2 + 1) * 2 * dh)
                    o_ref[0, qrows, oc] = jnp.where(low, oe, oo).astype(o_ref.dtype)


def _cast_rider_specs(w, steps, steps_per_batch):
    rows = w.shape[0] // steps
    assert rows * steps == w.shape[0] and rows % BF16_TILE_ROWS == 0
    spec = pl.BlockSpec((rows, w.shape[1]), lambda bi, n: (bi * steps_per_batch + n, 0))
    return spec, jax.ShapeDtypeStruct(w.shape, BF16)


def _attn(p16, bias, wa, wb):
    b, s, _ = p16.shape
    l = ATTN_BLOCK
    tq = ATTN_QB * l
    nq = s // tq
    q_blk = P16_BLOCKS.index("aq")
    k_blk = P16_KV_COL // KV_WIDTH
    v_blk = k_blk + 1
    prev_spec = lambda blk: pl.BlockSpec(
        (1, l, KV_WIDTH), lambda bi, n: (bi, jnp.maximum(n * ATTN_QB - 1, 0), blk))
    cur_spec = lambda blk: pl.BlockSpec((1, tq, KV_WIDTH), lambda bi, n: (bi, n, blk))
    wa_spec, wa_shape = _cast_rider_specs(wa, b * nq, nq)
    wb_spec, wb_shape = _cast_rider_specs(wb, b * nq, nq)
    return pl.pallas_call(
        _attn_kernel,
        grid=(b, nq),
        in_specs=[pl.BlockSpec((1, tq, ATTN_WIDTH), lambda bi, n: (bi, n, q_blk)),
                  prev_spec(k_blk), cur_spec(k_blk), prev_spec(v_blk), cur_spec(v_blk),
                  pl.BlockSpec(bias.shape, lambda bi, n: (0, 0, 0, 0), pipeline_mode=pl.Buffered(1)),
                  wa_spec, wb_spec],
        out_specs=[pl.BlockSpec((1, tq, ATTN_WIDTH), lambda bi, n: (bi, n, 0)), wa_spec, wb_spec],
        out_shape=[jax.ShapeDtypeStruct((b, s, ATTN_WIDTH), BF16), wa_shape, wb_shape],
        compiler_params=_params("arbitrary", "arbitrary"),
        name="attn",
    )(p16, p16, p16, p16, p16, bias, wa, wb)


def _outproj_kernel(orec_ref, oatt_ref, x_ref, mod_ref, nw_ref, w_ref, x1_ref, h2_ref, rs_ref):
    ssq = None
    for c0 in range(0, D_MODEL, OUTPROJ_TN):
        cs = slice(c0, c0 + OUTPROJ_TN)
        y = _dot(orec_ref[0], w_ref[0:HGRN_WIDTH, cs]) + _dot(oatt_ref[0], w_ref[HGRN_WIDTH:, cs])
        x1 = x_ref[0, :, cs] + mod_ref[0, 2:3, cs] * y
        x1_ref[0, :, cs] = x1
        ssq = _lane_folded_sumsq(ssq, x1)
    rs_ref[...] = _rsqrt_mean(ssq)

    shift = mod_ref[0, 3:4, :]
    wmod = nw_ref[...] * (1.0 + mod_ref[0, 4:5, :])

    def chunk(rows):
        h2_ref[0, rows, :] = (x1_ref[0, rows, :] * rs_ref[rows, :] * wmod + shift).astype(BF16)

    _for_row_chunks(OUTPROJ_TM, chunk)


def _outproj(o_rec, o_att, x, mod, norm_w, w_bf):
    b, s, d = x.shape
    tm = OUTPROJ_TM
    return pl.pallas_call(
        _outproj_kernel,
        grid=(b, s // tm),
        in_specs=[pl.BlockSpec((1, tm, HGRN_WIDTH), lambda bi, m: (bi, m, 0)),
                  pl.BlockSpec((1, tm, ATTN_WIDTH), lambda bi, m: (bi, m, 0)),
                  pl.BlockSpec((1, tm, d), lambda bi, m: (bi, m, 0)),
                  pl.BlockSpec((1, 6, d), lambda bi, m: (bi, 0, 0)),
                  pl.BlockSpec((1, d), lambda bi, m: (0, 0)),
                  pl.BlockSpec(w_bf.shape, lambda bi, m: (0, 0), pipeline_mode=pl.Buffered(1))],
        out_specs=[pl.BlockSpec((1, tm, d), lambda bi, m: (bi, m, 0)),
                   pl.BlockSpec((1, tm, d), lambda bi, m: (bi, m, 0))],
        out_shape=[jax.ShapeDtypeStruct((b, s, d), F32),
                   jax.ShapeDtypeStruct((b, s, d), BF16)],
        scratch_shapes=[pltpu.VMEM((tm, 1), F32)],
        compiler_params=_params("parallel", "parallel"),
        name="outproj",
    )(o_rec, o_att, x, mod, norm_w, w_bf)


def _ffn_kernel(h_ref, x1_hbm, mod_ref, fw_ref, wg_ref, wu_ref, wd_ref, o_ref, x1_buf, rs_ref, x1_sem,
                *, final_norm):
    bi, m, j = pl.program_id(0), pl.program_id(1), pl.program_id(2)

    def x1_copy():
        rows = pl.ds(pl.multiple_of(m * FFN_TM, FFN_TM), FFN_TM)
        return pltpu.make_async_copy(x1_hbm.at[bi, rows, :], x1_buf, x1_sem)

    last = pl.num_programs(2) - 1
    ksubs = [slice(k0, k0 + FFN_SUB) for k0 in range(0, FFN_TF, FFN_SUB)]
    pieces = [slice(c0, c0 + FFN_DN) for c0 in range(0, D_MODEL, FFN_DN)]

    def acts():
        return [(_silu(_dot(h_ref[0], wg_ref[:, ks])) * _dot(h_ref[0], wu_ref[:, ks])).astype(BF16)
                for ks in ksubs]

    def down(a, cs):
        return functools.reduce(lambda x, y: x + y, [_dot(ai, wd_ref[ks, cs]) for ai, ks in zip(a, ksubs)])

    @pl.when(j == 0)
    def _():
        x1_copy().start()
        a = acts()
        for cs in pieces:
            o_ref[0, :, cs] = down(a, cs)

    @pl.when(jnp.logical_and(j > 0, j < last))
    def _():
        a = acts()
        for cs in pieces:
            o_ref[0, :, cs] += down(a, cs)

    @pl.when(j == last)
    def _():
        x1_copy().wait()
        a = acts()
        ssq = None
        for cs in pieces:
            x2 = x1_buf[:, cs] + mod_ref[0, 5:6, cs] * (o_ref[0, :, cs] + down(a, cs))
            x1_buf[:, cs] = x2
            if final_norm:
                ssq = _lane_folded_sumsq(ssq, x2)
        if final_norm:
            rs_ref[...] = _rsqrt_mean(ssq)
        fw = fw_ref[...]

        def chunk(rows):
            x2 = x1_buf[rows, :]
            o_ref[0, rows, :] = x2 * rs_ref[rows, :] * fw if final_norm else x2

        _for_row_chunks(FFN_TM, chunk, loop_trips=2)


def _ffn(h2, x1, mod, final_w, wg, wu, wd, final_norm):
    b, s, d = x1.shape
    tm, tf = FFN_TM, FFN_TF
    return pl.pallas_call(
        functools.partial(_ffn_kernel, final_norm=final_norm),
        grid=(b, s // tm, D_FF // tf),
        in_specs=[pl.BlockSpec((1, tm, d), lambda bi, m, j: (bi, m, 0)),
                  pl.BlockSpec(memory_space=pl.ANY),
                  pl.BlockSpec((1, 6, d), lambda bi, m, j: (bi, 0, 0)),
                  pl.BlockSpec((1, d), lambda bi, m, j: (0, 0)),
                  pl.BlockSpec((d, tf), lambda bi, m, j: (0, j)),
                  pl.BlockSpec((d, tf), lambda bi, m, j: (0, j)),
                  pl.BlockSpec((tf, d), lambda bi, m, j: (j, 0))],
        out_specs=pl.BlockSpec((1, tm, d), lambda bi, m, j: (bi, m, 0)),
        out_shape=jax.ShapeDtypeStruct((b, s, d), F32),
        scratch_shapes=[pltpu.VMEM((tm, d), F32), pltpu.VMEM((tm, 1), F32), pltpu.SemaphoreType.DMA(())],
        compiler_params=_params("arbitrary", "arbitrary", "arbitrary"),
        name="ffn",
    )(h2, x1, mod, final_w, wg, wu, wd)


def kernel(x, c, w_ada, b_ada, norm1_w, w_in, lower_bounds, hgrn_norm_w, attn_sinks,
           rel_bias_table, w_out, norm2_w, w_gate, w_up, w_down, final_norm_w):
    b, s, d = x.shape
    assert (d, w_in.shape[-1], w_gate.shape[-1]) == (D_MODEL, IN_WIDTH, D_FF)
    assert s % max(INPROJ_TM, HGRN_TC, OUTPROJ_TM, FFN_TM, ATTN_QB * ATTN_BLOCK) == 0
    depth = w_ada.shape[0]
    c8 = jnp.pad(c, ((0, 8 - b), (0, 0)))
    for layer in range(depth):
        bias, wi16 = _bias_table(rel_bias_table, attn_sinks[layer], w_in[layer])
        mod = _ada(c8, w_ada[layer], b_ada[layer][None, :])[:b].reshape(b, 6, d)
        gate, p16, dec, wd16, wo16 = _inproj(x, mod, norm1_w[layer][None, :], wi16, lower_bounds,
                                             hgrn_norm_w[layer][None, :], layer,
                                             w_down[layer], w_out[layer])
        o_rec = _hgrn(gate, p16, dec)
        o_att, wg16, wu16 = _attn(p16, bias, w_gate[layer], w_up[layer])
        x1, h2 = _outproj(o_rec, o_att, x, mod, norm2_w[layer][None, :], wo16)
        x = _ffn(h2, x1, mod, final_norm_w[None, :], wg16, wu16, wd16,
                 final_norm=(layer == depth - 1))
    return x
```

```python
import functools

import numpy as np
import jax
import jax.numpy as jnp
from jax import lax
from jax.experimental import pallas as pl
from jax.experimental.pallas import tpu as pltpu

D_MODEL = 2048
HGRN_WIDTH = 1024
HGRN_HEAD_DIM = 128
HGRN_HEADS = HGRN_WIDTH // HGRN_HEAD_DIM
HGRN_CHUNK = 64
ATTN_WIDTH = 1024
ATTN_HEAD_DIM = 64
ATTN_Q_HEADS = ATTN_WIDTH // ATTN_HEAD_DIM
ATTN_KV_HEADS = 4
ATTN_GROUP = ATTN_Q_HEADS // ATTN_KV_HEADS
WINDOW = 128
ATTN_BLOCK = 128
ATTN_QB = 4
REL_BUCKETS = 32
REL_MAX_DIST = 128
D_FF = 5632
KV_WIDTH = ATTN_KV_HEADS * ATTN_HEAD_DIM
IN_WIDTH = 4 * HGRN_WIDTH + ATTN_WIDTH + 2 * KV_WIDTH
P16_BLOCKS = ("v", "aq", "q_rel", "k_rel", "q_dec", "k_dec")
P16_KV_COL = len(P16_BLOCKS) * HGRN_WIDTH
P16_WIDTH = P16_KV_COL + 2 * KV_WIDTH
EPS = 1e-6
NEG_INF = -1e30
LOG2E = 1.4426950408889634

F32 = jnp.float32
BF16 = jnp.bfloat16

VMEM_LIMIT_BYTES = 56 * 1024 * 1024
INPROJ_VMEM_LIMIT_BYTES = 61 * 1024 * 1024
LANES = 128
SUBLANES = 8
BF16_TILE_ROWS = 2 * SUBLANES

ADA_TN = 2048
BIAS_HEADS_PER_STEP = 2
INPROJ_TM = 512
INPROJ_TN = 256
HGRN_TC = 1024
HGRN_UNROLL = 4
HGRN_ROWS = HGRN_UNROLL * HGRN_CHUNK
HGRN_TRIP_CHUNKS = 16
OUTPROJ_TM = 512
OUTPROJ_TN = 256
FFN_TM = 1024
FFN_TF = 512
FFN_SUB = 256
FFN_DN = 512
ROW_CHUNK = 16


def _params(*semantics, vmem_limit_bytes=VMEM_LIMIT_BYTES):
    return pltpu.CompilerParams(dimension_semantics=semantics, vmem_limit_bytes=vmem_limit_bytes)


def _sigmoid(v):
    return 1.0 / (1.0 + jnp.exp2(v * (-LOG2E)))


def _silu(v):
    return v * _sigmoid(v)


def _dot(a, b):
    return jnp.dot(a, b, preferred_element_type=F32)


def _dot_nt(a, b):
    return lax.dot_general(a, b, (((1,), (1,)), ((), ())), preferred_element_type=F32)


def _dot_tn(a, b):
    return lax.dot_general(a, b, (((0,), (0,)), ((), ())), preferred_element_type=F32)


def _ada_kernel(c_ref, w_ref, b_ref, o_ref):
    c_act = _silu(c_ref[...])
    o_ref[...] = _dot(c_act.astype(BF16), w_ref[...].astype(BF16)) + b_ref[...]


def _ada(c8, w, b):
    n = w.shape[1]
    return pl.pallas_call(
        _ada_kernel,
        grid=(n // ADA_TN,),
        in_specs=[pl.BlockSpec((8, D_MODEL), lambda j: (0, 0)),
                  pl.BlockSpec((D_MODEL, ADA_TN), lambda j: (0, j)),
                  pl.BlockSpec((1, ADA_TN), lambda j: (0, j))],
        out_specs=pl.BlockSpec((8, ADA_TN), lambda j: (0, j)),
        out_shape=jax.ShapeDtypeStruct((8, n), F32),
        compiler_params=_params("arbitrary"),
        name="ada",
    )(c8, w, b)


def _modulated_norm(x, wmod, shift):
    return x * lax.rsqrt(jnp.mean(x * x, axis=-1, keepdims=True) + EPS) * wmod + shift


def _for_row_chunks(n_rows, fn, loop_trips=1):
    per_trip = n_rows // loop_trips
    assert per_trip * loop_trips == n_rows and per_trip % ROW_CHUNK == 0

    def trip(i, carry):
        for r0 in range(0, per_trip, ROW_CHUNK):
            start = r0 if loop_trips == 1 else pl.multiple_of(i * per_trip + r0, ROW_CHUNK)
            fn(pl.ds(start, ROW_CHUNK))
        return carry

    if loop_trips == 1:
        trip(0, 0)
    else:
        lax.fori_loop(0, loop_trips, trip, 0)


def _lane_folded_sumsq(ssq, v):
    sq = v * v
    for k0 in range(0, v.shape[1], LANES):
        ssq = sq[:, k0:k0 + LANES] if ssq is None else ssq + sq[:, k0:k0 + LANES]
    return ssq


def _rsqrt_mean(ssq):
    return lax.rsqrt(jnp.sum(ssq, axis=-1, keepdims=True) * (1.0 / D_MODEL) + EPS)


def _split3(v):
    hi = v.astype(BF16)
    r1 = v - hi.astype(F32)
    mid = r1.astype(BF16)
    lo = (r1 - mid.astype(F32)).astype(BF16)
    return hi, mid, lo


def _hgrn_tril3():
    tril = np.tril(np.ones((HGRN_CHUNK, HGRN_CHUNK), np.float32))
    return jnp.asarray(np.concatenate([tril, tril, tril], axis=1), dtype=BF16)


def _hgrn_prepare(q, f_logit, g, lb, nw, tril3):
    c, nu = HGRN_CHUNK, HGRN_UNROLL

    def per_chunk_rows(x, offset):
        rid = lax.broadcasted_iota(jnp.int32, x.shape, 0)
        out = jnp.broadcast_to(x[offset:offset + 1, :], x.shape)
        for u in range(1, nu):
            out = jnp.where(rid >= u * c, jnp.broadcast_to(x[u * c + offset:u * c + offset + 1, :], x.shape), out)
        return out

    f = lb + (1.0 - lb) * _sigmoid(f_logit)
    k = 1.0 - f
    parts = _split3(jnp.log2(f))
    b = jnp.concatenate(
        [_dot(tril3, jnp.concatenate([p[u * c:(u + 1) * c] for p in parts], axis=0)) for u in range(nu)],
        axis=0)
    b_mid = per_chunk_rows(b, c // 2 - 1)
    b_last = per_chunk_rows(b, c - 1)
    d_mid = b - b_mid
    qr = q * jnp.exp2(d_mid)
    kr = k * jnp.exp2(-d_mid)
    q_dec = (qr * jnp.exp2(b_mid)).astype(BF16)
    k_dec = (kr * jnp.exp2(b_last - b_mid)).astype(BF16)
    decays = [jnp.exp2(b[(u + 1) * c - 1:(u + 1) * c, :]) for u in range(nu)]
    return qr.astype(BF16), kr.astype(BF16), q_dec, k_dec, nw * _silu(g), decays


def _inproj_kernel(x_ref, mod_ref, nw_ref, w_ref, lb_ref, nwh_ref, tril_ref, wa_ref, wb_ref,
                   gate_ref, o16_ref, dec_ref, wa16_ref, wb16_ref, h_ref, *, layer):
    wa16_ref[...] = wa_ref[...].astype(BF16)
    wb16_ref[...] = wb_ref[...].astype(BF16)
    shift = mod_ref[0, 0:1, :]
    wmod = nw_ref[...] * (1.0 + mod_ref[0, 1:2, :])

    def chunk(rows):
        h_ref[rows, :] = _modulated_norm(x_ref[0, rows, :], wmod, shift).astype(BF16)

    _for_row_chunks(INPROJ_TM, chunk)

    lb_rows = [lb_ref[r:r + 1, :] for r in range(lb_ref.shape[0])]
    lb_max = functools.reduce(jnp.maximum, lb_rows)
    lb_exp = [jnp.exp(r - lb_max) for r in lb_rows]
    lb_all = sum(lb_exp[:layer + 1]) / sum(lb_exp)

    tn, hw = INPROJ_TN, HGRN_WIDTH
    p16_col = {name: i * hw for i, name in enumerate(P16_BLOCKS)}

    def project(w_col):
        return _dot(h_ref[...], w_ref[:, w_col:w_col + tn])

    def to_p16(name, c0, value):
        o16_ref[0, :, p16_col[name] + c0:p16_col[name] + c0 + tn] = value.astype(BF16)

    for c0 in range(0, hw, tn):
        cs = slice(c0, c0 + tn)
        hq, hf, hg = project(c0), project(hw + c0), project(3 * hw + c0)
        followers = [lambda: to_p16("v", c0, project(2 * hw + c0)),
                     lambda: to_p16("aq", c0, project(4 * hw + c0))]
        for half in range(INPROJ_TM // HGRN_ROWS):
            rows = slice(half * HGRN_ROWS, (half + 1) * HGRN_ROWS)
            q_rel, k_rel, q_dec, k_dec, gate, decays = _hgrn_prepare(
                hq[rows], hf[rows], hg[rows], lb_all[:, cs], nwh_ref[:, cs], tril_ref[...])
            for name, value in (("q_rel", q_rel), ("k_rel", k_rel), ("q_dec", q_dec), ("k_dec", k_dec)):
                o16_ref[0, rows, p16_col[name] + c0:p16_col[name] + c0 + tn] = value
            gate_ref[0, rows, cs] = gate
            for u, decay in enumerate(decays):
                dec_ref[0, half * HGRN_UNROLL + u:half * HGRN_UNROLL + u + 1, cs] = decay
            n_groups = INPROJ_TM // HGRN_ROWS
            if (half + 1) % (n_groups // 2) == 0:
                followers[(half + 1) // (n_groups // 2) - 1]()
    for c0 in range(0, 2 * KV_WIDTH, tn):
        o16_ref[0, :, P16_KV_COL + c0:P16_KV_COL + c0 + tn] = project(4 * hw + ATTN_WIDTH + c0).astype(BF16)


def _inproj(x, mod, norm_w, w_bf, lower_bounds, hgrn_norm_w, layer, wa, wb):
    b, s, d = x.shape
    tm = INPROJ_TM
    nt = s // tm
    assert tm % (2 * HGRN_ROWS) == 0 and (2 * KV_WIDTH) % INPROJ_TN == 0
    chunks = tm // HGRN_CHUNK
    tril3 = _hgrn_tril3()
    wa_spec, wa_shape = _cast_rider_specs(wa, b * nt, nt)
    wb_spec, wb_shape = _cast_rider_specs(wb, b * nt, nt)
    return pl.pallas_call(
        functools.partial(_inproj_kernel, layer=layer),
        grid=(b, s // tm),
        in_specs=[pl.BlockSpec((1, tm, d), lambda bi, m: (bi, m, 0)),
                  pl.BlockSpec((1, 6, d), lambda bi, m: (bi, 0, 0)),
                  pl.BlockSpec((1, d), lambda bi, m: (0, 0)),
                  pl.BlockSpec(w_bf.shape, lambda bi, m: (0, 0), pipeline_mode=pl.Buffered(1)),
                  pl.BlockSpec(lower_bounds.shape, lambda bi, m: (0, 0)),
                  pl.BlockSpec((1, HGRN_WIDTH), lambda bi, m: (0, 0)),
                  pl.BlockSpec(tril3.shape, lambda bi, m: (0, 0)),
                  wa_spec, wb_spec],
        out_specs=[pl.BlockSpec((1, tm, HGRN_WIDTH), lambda bi, m: (bi, m, 0)),
                   pl.BlockSpec((1, tm, P16_WIDTH), lambda bi, m: (bi, m, 0)),
                   pl.BlockSpec((1, chunks, HGRN_WIDTH), lambda bi, m: (bi, m, 0)),
                   wa_spec, wb_spec],
        out_shape=[jax.ShapeDtypeStruct((b, s, HGRN_WIDTH), F32),
                   jax.ShapeDtypeStruct((b, s, P16_WIDTH), BF16),
                   jax.ShapeDtypeStruct((b, s // HGRN_CHUNK, HGRN_WIDTH), F32),
                   wa_shape, wb_shape],
        scratch_shapes=[pltpu.VMEM((tm, d), BF16)],
        compiler_params=_params("arbitrary", "arbitrary", vmem_limit_bytes=INPROJ_VMEM_LIMIT_BYTES),
        name="inproj",
    )(x, mod, norm_w, w_bf, lower_bounds, hgrn_norm_w, tril3, wa, wb)


def _hgrn_kernel(v_ref, qrel_ref, krel_ref, qdec_ref, kdec_ref, gate_ref, dec_ref, o_ref, st_ref):
    c, dk, nu = HGRN_CHUNK, HGRN_HEAD_DIM, HGRN_TRIP_CHUNKS

    @pl.when(pl.program_id(1) == 0)
    def _():
        st_ref[...] = jnp.zeros_like(st_ref)

    causal = (lax.broadcasted_iota(jnp.int32, (c, c), 0) >= lax.broadcasted_iota(jnp.int32, (c, c), 1))
    heads = [slice(h * dk, (h + 1) * dk) for h in range(HGRN_HEADS)]

    def body(i, carry):
        st = [st_ref[h] for h in range(HGRN_HEADS)]
        for u in range(nu):
            rows = pl.ds(pl.multiple_of((i * nu + u) * c, c), c)
            decay = dec_ref[0, pl.ds(i * nu + u, 1), :]
            a = [jnp.where(causal, _dot_nt(qrel_ref[0, rows, hs], krel_ref[0, rows, hs]), 0.0).astype(BF16)
                 for hs in heads]
            o = [_dot(a[h], v_ref[0, rows, hs]) + _dot(qdec_ref[0, rows, hs], st[h].T.astype(BF16))
                 for h, hs in enumerate(heads)]
            st = [decay[:, hs] * st[h] + _dot_tn(v_ref[0, rows, hs], kdec_ref[0, rows, hs])
                  for h, hs in enumerate(heads)]
            for h, hs in enumerate(heads):
                oh = o[h] * lax.rsqrt(jnp.mean(o[h] * o[h], axis=-1, keepdims=True) + EPS)
                o_ref[0, rows, hs] = (oh * gate_ref[0, rows, hs]).astype(o_ref.dtype)
        for h in range(HGRN_HEADS):
            st_ref[h] = st[h]
        return carry

    lax.fori_loop(0, HGRN_TC // (nu * c), body, 0)


def _hgrn(gate, p16, dec):
    b, s, _ = gate.shape
    nt = s // HGRN_TC
    blk = lambda name: pl.BlockSpec((1, HGRN_TC, HGRN_WIDTH),
                                    lambda bi, t, j=P16_BLOCKS.index(name): (bi, t, j))
    return pl.pallas_call(
        _hgrn_kernel,
        grid=(b, nt),
        in_specs=[blk("v"), blk("q_rel"), blk("k_rel"), blk("q_dec"), blk("k_dec"),
                  pl.BlockSpec((1, HGRN_TC, HGRN_WIDTH), lambda bi, t: (bi, t, 0)),
                  pl.BlockSpec((1, HGRN_TC // HGRN_CHUNK, HGRN_WIDTH), lambda bi, t: (bi, t, 0))],
        out_specs=pl.BlockSpec((1, HGRN_TC, HGRN_WIDTH), lambda bi, t: (bi, t, 0)),
        out_shape=jax.ShapeDtypeStruct((b, s, HGRN_WIDTH), BF16),
        scratch_shapes=[pltpu.VMEM((HGRN_HEADS, HGRN_HEAD_DIM, HGRN_HEAD_DIM), F32)],
        compiler_params=_params("arbitrary", "arbitrary"),
        name="hgrn",
    )(p16, p16, p16, p16, p16, gate, dec)


def _t5_causal_buckets(dist):
    max_exact = REL_BUCKETS // 2
    d = np.maximum(dist, 0)
    log_b = max_exact + (np.log(np.maximum(d, 1) / max_exact)
                         / np.log(REL_MAX_DIST / max_exact)
                         * (REL_BUCKETS - max_exact)).astype(np.int32)
    log_b = np.minimum(log_b, REL_BUCKETS - 1)
    return np.where(d < max_exact, d, log_b).astype(np.int32)


def _bias_kernel(tab_ref, sink_ref, bucket_ref, valid_ref, w_ref, o_ref, w16_ref):
    w16_ref[...] = w_ref[...].astype(BF16)
    bucket = bucket_ref[...]
    sink_col = lax.broadcasted_iota(jnp.int32, bucket.shape, 1) == 0
    for i in range(BIAS_HEADS_PER_STEP):
        h = pl.program_id(0) * BIAS_HEADS_PER_STEP + i
        acc = jnp.zeros(bucket.shape, F32)
        for bk in range(REL_BUCKETS):
            acc = jnp.where(bucket == bk, tab_ref[bk, h], acc)
        for variant in range(2):
            masked = jnp.where(valid_ref[variant] != 0, acc, NEG_INF)
            o_ref[variant, i] = jnp.where(sink_col, sink_ref[h], masked)


def _bias_table(rel_table, sinks, w):
    l = ATTN_BLOCK
    qi = np.arange(l)[:, None]
    kj = np.arange(2 * l)[None, :]
    dist = qi + l - kj
    in_window = (dist >= 0) & (dist < WINDOW)
    valid = np.stack([in_window & (kj >= l), in_window]).astype(np.int32)
    assert not valid[:, :, 0].any()
    bucket = _t5_causal_buckets(dist)
    steps = ATTN_Q_HEADS // BIAS_HEADS_PER_STEP
    w_rows = w.shape[0] // steps
    assert steps * BIAS_HEADS_PER_STEP == ATTN_Q_HEADS
    assert w_rows * steps == w.shape[0] and w_rows % BF16_TILE_ROWS == 0
    w_spec = pl.BlockSpec((w_rows, w.shape[1]), lambda h: (h, 0))
    return pl.pallas_call(
        _bias_kernel,
        grid=(steps,),
        in_specs=[pl.BlockSpec(memory_space=pltpu.SMEM),
                  pl.BlockSpec(memory_space=pltpu.SMEM),
                  pl.BlockSpec((l, 2 * l), lambda h: (0, 0)),
                  pl.BlockSpec((2, l, 2 * l), lambda h: (0, 0, 0)),
                  w_spec],
        out_specs=[pl.BlockSpec((2, BIAS_HEADS_PER_STEP, l, 2 * l), lambda h: (0, h, 0, 0)), w_spec],
        out_shape=[jax.ShapeDtypeStruct((2, ATTN_Q_HEADS, l, 2 * l), F32),
                   jax.ShapeDtypeStruct(w.shape, BF16)],
        compiler_params=_params("arbitrary"),
        name="bias",
    )(rel_table, sinks, jnp.asarray(bucket), jnp.asarray(valid), w)


def _zero_first_row(a):
    top = a[:BF16_TILE_ROWS]
    row = lax.broadcasted_iota(jnp.int32, top.shape, 0)
    return jnp.concatenate([jnp.where(row == 0, 0.0, top).astype(a.dtype), a[BF16_TILE_ROWS:]], axis=0)


def _attn_kernel(q_ref, kp_ref, kc_ref, vp_ref, vc_ref, bias_ref, wa_ref, wb_ref,
                 o_ref, wa16_ref, wb16_ref):
    l = ATTN_BLOCK
    dh = ATTN_HEAD_DIM
    wa16_ref[...] = wa_ref[...].astype(BF16)
    wb16_ref[...] = wb_ref[...].astype(BF16)
    lane = lax.broadcasted_iota(jnp.int32, (1, 2 * dh), 1)
    low = lane < dh
    first = jnp.minimum(pl.program_id(1), 1)

    for pair in range(ATTN_KV_HEADS // 2):
        pc = slice(pair * 2 * dh, (pair + 1) * 2 * dh)
        kk = pltpu.bitcast(jnp.concatenate([kp_ref[0, :, pc], kc_ref[0, :, pc]], axis=0), jnp.int32)
        vv = pltpu.bitcast(jnp.concatenate([vp_ref[0, :, pc], vc_ref[0, :, pc]], axis=0), jnp.int32)
        kk_sw = pltpu.roll(kk, dh, axis=1)
        vv_sw = pltpu.roll(vv, dh, axis=1)
        for sub in range(2):
            kvh = pair * 2 + sub
            if sub == 0:
                k2_all = jnp.where(low, kk, kk_sw)
                v2_all = jnp.where(low, vv, vv_sw)
            else:
                k2_all = jnp.where(low, kk_sw, kk)
                v2_all = jnp.where(low, vv_sw, vv)
            k2_all = pltpu.bitcast(k2_all, BF16)
            v2_all = pltpu.bitcast(v2_all, BF16)
            for t in range(ATTN_QB):
                qrows = slice(t * l, (t + 1) * l)
                k2 = _zero_first_row(k2_all[t * l:(t + 2) * l])
                v2 = _zero_first_row(v2_all[t * l:(t + 2) * l])
                qs = []
                for j in range(ATTN_GROUP):
                    hq = kvh * ATTN_GROUP + j
                    qc = slice((hq // 2) * 2 * dh, (hq // 2 + 1) * 2 * dh)
                    q2 = q_ref[0, qrows, qc] * (dh ** -0.5)
                    keep = low if hq % 2 == 0 else jnp.logical_not(low)
                    qs.append(jnp.where(keep, q2, 0.0).astype(BF16))
                s_all = _dot_nt(jnp.concatenate(qs, axis=0), k2)
                ps = []
                for j in range(ATTN_GROUP):
                    hq = kvh * ATTN_GROUP + j
                    bias = bias_ref[first, hq] if t == 0 else bias_ref[1, hq]
                    s = s_all[j * l:(j + 1) * l] + bias
                    p = jnp.exp(s - jnp.max(s, axis=-1, keepdims=True))
                    ps.append(p.astype(BF16))
                v2x = jnp.concatenate([v2, jnp.ones_like(v2)], axis=1)
                o_all = _dot(jnp.concatenate(ps, axis=0), v2x)
                o_all = o_all[:, :2 * dh] / o_all[:, 2 * dh:]
                for jp in range(ATTN_GROUP // 2):
                    hq = kvh * ATTN_GROUP + 2 * jp
                    oe = o_all[(2 * jp) * l:(2 * jp + 1) * l]
                    oo = o_all[(2 * jp + 1) * l:(2 * jp + 2) * l]
                    oc = slice((hq // 2) * 2 * dh, (hq // 2 + 1) * 2 * dh)
                    o_ref[0, qrows, oc] = jnp.where(low, oe, oo).astype(o_ref.dtype)


def _cast_rider_specs(w, steps, steps_per_batch):
    rows = w.shape[0] // steps
    assert rows * steps == w.shape[0] and rows % BF16_TILE_ROWS == 0
    spec = pl.BlockSpec((rows, w.shape[1]), lambda bi, n: (bi * steps_per_batch + n, 0))
    return spec, jax.ShapeDtypeStruct(w.shape, BF16)


def _attn(p16, bias, wa, wb):
    b, s, _ = p16.shape
    l = ATTN_BLOCK
    tq = ATTN_QB * l
    nq = s // tq
    q_blk = P16_BLOCKS.index("aq")
    k_blk = P16_KV_COL // KV_WIDTH
    v_blk = k_blk + 1
    prev_spec = lambda blk: pl.BlockSpec(
        (1, l, KV_WIDTH), lambda bi, n: (bi, jnp.maximum(n * ATTN_QB - 1, 0), blk))
    cur_spec = lambda blk: pl.BlockSpec((1, tq, KV_WIDTH), lambda bi, n: (bi, n, blk))
    wa_spec, wa_shape = _cast_rider_specs(wa, b * nq, nq)
    wb_spec, wb_shape = _cast_rider_specs(wb, b * nq, nq)
    return pl.pallas_call(
        _attn_kernel,
        grid=(b, nq),
        in_specs=[pl.BlockSpec((1, tq, ATTN_WIDTH), lambda bi, n: (bi, n, q_blk)),
                  prev_spec(k_blk), cur_spec(k_blk), prev_spec(v_blk), cur_spec(v_blk),
                  pl.BlockSpec(bias.shape, lambda bi, n: (0, 0, 0, 0), pipeline_mode=pl.Buffered(1)),
                  wa_spec, wb_spec],
        out_specs=[pl.BlockSpec((1, tq, ATTN_WIDTH), lambda bi, n: (bi, n, 0)), wa_spec, wb_spec],
        out_shape=[jax.ShapeDtypeStruct((b, s, ATTN_WIDTH), BF16), wa_shape, wb_shape],
        compiler_params=_params("arbitrary", "arbitrary"),
        name="attn",
    )(p16, p16, p16, p16, p16, bias, wa, wb)


def _outproj_kernel(orec_ref, oatt_ref, x_ref, mod_ref, nw_ref, w_ref, x1_ref, h2_ref, rs_ref):
    ssq = None
    for c0 in range(0, D_MODEL, OUTPROJ_TN):
        cs = slice(c0, c0 + OUTPROJ_TN)
        y = _dot(orec_ref[0], w_ref[0:HGRN_WIDTH, cs]) + _dot(oatt_ref[0], w_ref[HGRN_WIDTH:, cs])
        x1 = x_ref[0, :, cs] + mod_ref[0, 2:3, cs] * y
        x1_ref[0, :, cs] = x1
        ssq = _lane_folded_sumsq(ssq, x1)
    rs_ref[...] = _rsqrt_mean(ssq)

    shift = mod_ref[0, 3:4, :]
    wmod = nw_ref[...] * (1.0 + mod_ref[0, 4:5, :])

    def chunk(rows):
        h2_ref[0, rows, :] = (x1_ref[0, rows, :] * rs_ref[rows, :] * wmod + shift).astype(BF16)

    _for_row_chunks(OUTPROJ_TM, chunk)


def _outproj(o_rec, o_att, x, mod, norm_w, w_bf):
    b, s, d = x.shape
    tm = OUTPROJ_TM
    return pl.pallas_call(
        _outproj_kernel,
        grid=(b, s // tm),
        in_specs=[pl.BlockSpec((1, tm, HGRN_WIDTH), lambda bi, m: (bi, m, 0)),
                  pl.BlockSpec((1, tm, ATTN_WIDTH), lambda bi, m: (bi, m, 0)),
                  pl.BlockSpec((1, tm, d), lambda bi, m: (bi, m, 0)),
                  pl.BlockSpec((1, 6, d), lambda bi, m: (bi, 0, 0)),
                  pl.BlockSpec((1, d), lambda bi, m: (0, 0)),
                  pl.BlockSpec(w_bf.shape, lambda bi, m: (0, 0), pipeline_mode=pl.Buffered(1))],
        out_specs=[pl.BlockSpec((1, tm, d), lambda bi, m: (bi, m, 0)),
                   pl.BlockSpec((1, tm, d), lambda bi, m: (bi, m, 0))],
        out_shape=[jax.ShapeDtypeStruct((b, s, d), F32),
                   jax.ShapeDtypeStruct((b, s, d), BF16)],
        scratch_shapes=[pltpu.VMEM((tm, 1), F32)],
        compiler_params=_params("parallel", "parallel"),
        name="outproj",
    )(o_rec, o_att, x, mod, norm_w, w_bf)


def _ffn_kernel(h_ref, x1_hbm, mod_ref, fw_ref, wg_ref, wu_ref, wd_ref, o_ref, x1_buf, rs_ref, x1_sem,
                *, final_norm):
    bi, m, j = pl.program_id(0), pl.program_id(1), pl.program_id(2)

    def x1_copy():
        rows = pl.ds(pl.multiple_of(m * FFN_TM, FFN_TM), FFN_TM)
        return pltpu.make_async_copy(x1_hbm.at[bi, rows, :], x1_buf, x1_sem)

    last = pl.num_programs(2) - 1
    ksubs = [slice(k0, k0 + FFN_SUB) for k0 in range(0, FFN_TF, FFN_SUB)]
    pieces = [slice(c0, c0 + FFN_DN) for c0 in range(0, D_MODEL, FFN_DN)]

    def acts():
        return [(_silu(_dot(h_ref[0], wg_ref[:, ks])) * _dot(h_ref[0], wu_ref[:, ks])).astype(BF16)
                for ks in ksubs]

    def down(a, cs):
        return functools.reduce(lambda x, y: x + y, [_dot(ai, wd_ref[ks, cs]) for ai, ks in zip(a, ksubs)])

    @pl.when(j == 0)
    def _():
        x1_copy().start()
        a = acts()
        for cs in pieces:
            o_ref[0, :, cs] = down(a, cs)

    @pl.when(jnp.logical_and(j > 0, j < last))
    def _():
        a = acts()
        for cs in pieces:
            o_ref[0, :, cs] += down(a, cs)

    @pl.when(j == last)
    def _():
        x1_copy().wait()
        a = acts()
        ssq = None
        for cs in pieces:
            x2 = x1_buf[:, cs] + mod_ref[0, 5:6, cs] * (o_ref[0, :, cs] + down(a, cs))
            x1_buf[:, cs] = x2
            if final_norm:
                ssq = _lane_folded_sumsq(ssq, x2)
        if final_norm:
            rs_ref[...] = _rsqrt_mean(ssq)
        fw = fw_ref[...]

        def chunk(rows):
            x2 = x1_buf[rows, :]
            o_ref[0, rows, :] = x2 * rs_ref[rows, :] * fw if final_norm else x2

        _for_row_chunks(FFN_TM, chunk, loop_trips=2)


def _ffn(h2, x1, mod, final_w, wg, wu, wd, final_norm):
    b, s, d = x1.shape
    tm, tf = FFN_TM, FFN_TF
    return pl.pallas_call(
        functools.partial(_ffn_kernel, final_norm=final_norm),
        grid=(b, s // tm, D_FF // tf),
        in_specs=[pl.BlockSpec((1, tm, d), lambda bi, m, j: (bi, m, 0)),
                  pl.BlockSpec(memory_space=pl.ANY),
                  pl.BlockSpec((1, 6, d), lambda bi, m, j: (bi, 0, 0)),
                  pl.BlockSpec((1, d), lambda bi, m, j: (0, 0)),
                  pl.BlockSpec((d, tf), lambda bi, m, j: (0, j)),
                  pl.BlockSpec((d, tf), lambda bi, m, j: (0, j)),
                  pl.BlockSpec((tf, d), lambda bi, m, j: (j, 0))],
        out_specs=pl.BlockSpec((1, tm, d), lambda bi, m, j: (bi, m, 0)),
        out_shape=jax.ShapeDtypeStruct((b, s, d), F32),
        scratch_shapes=[pltpu.VMEM((tm, d), F32), pltpu.VMEM((tm, 1), F32), pltpu.SemaphoreType.DMA(())],
        compiler_params=_params("arbitrary", "arbitrary", "arbitrary"),
        name="ffn",
    )(h2, x1, mod, final_w, wg, wu, wd)


def kernel(x, c, w_ada, b_ada, norm1_w, w_in, lower_bounds, hgrn_norm_w, attn_sinks,
           rel_bias_table, w_out, norm2_w, w_gate, w_up, w_down, final_norm_w):
    b, s, d = x.shape
    assert (d, w_in.shape[-1], w_gate.shape[-1]) == (D_MODEL, IN_WIDTH, D_FF)
    assert s % max(INPROJ_TM, HGRN_TC, OUTPROJ_TM, FFN_TM, ATTN_QB * ATTN_BLOCK) == 0
    depth = w_ada.shape[0]
    c8 = jnp.pad(c, ((0, 8 - b), (0, 0)))
    for layer in range(depth):
        bias, wi16 = _bias_table(rel_bias_table, attn_sinks[layer], w_in[layer])
        mod = _ada(c8, w_ada[layer], b_ada[layer][None, :])[:b].reshape(b, 6, d)
        gate, p16, dec, wd16, wo16 = _inproj(x, mod, norm1_w[layer][None, :], wi16, lower_bounds,
                                             hgrn_norm_w[layer][None, :], layer,
                                             w_down[layer], w_out[layer])
        o_rec = _hgrn(gate, p16, dec)
        o_att, wg16, wu16 = _attn(p16, bias, w_gate[layer], w_up[layer])
        x1, h2 = _outproj(o_rec, o_att, x, mod, norm2_w[layer][None, :], wo16)
        x = _ffn(h2, x1, mod, final_norm_w[None, :], wg16, wu16, wd16,
                 final_norm=(layer == depth - 1))
    return x
```

```python
import functools

import numpy as np
import jax
import jax.numpy as jnp
from jax import lax
from jax.experimental import pallas as pl
from jax.experimental.pallas import tpu as pltpu

D_MODEL = 2048
HGRN_WIDTH = 1024
HGRN_HEAD_DIM = 128
HGRN_HEADS = HGRN_WIDTH // HGRN_HEAD_DIM
HGRN_CHUNK = 64
ATTN_WIDTH = 1024
ATTN_HEAD_DIM = 64
ATTN_Q_HEADS = ATTN_WIDTH // ATTN_HEAD_DIM
ATTN_KV_HEADS = 4
ATTN_GROUP = ATTN_Q_HEADS // ATTN_KV_HEADS
WINDOW = 128
ATTN_BLOCK = 128
ATTN_QB = 4
REL_BUCKETS = 32
REL_MAX_DIST = 128
D_FF = 5632
KV_WIDTH = ATTN_KV_HEADS * ATTN_HEAD_DIM
IN_WIDTH = 4 * HGRN_WIDTH + ATTN_WIDTH + 2 * KV_WIDTH
P16_BLOCKS = ("v", "aq", "q_rel", "k_rel", "q_dec", "k_dec")
P16_KV_COL = len(P16_BLOCKS) * HGRN_WIDTH
P16_WIDTH = P16_KV_COL + 2 * KV_WIDTH
EPS = 1e-6
NEG_INF = -1e30
LOG2E = 1.4426950408889634

F32 = jnp.float32
BF16 = jnp.bfloat16

VMEM_LIMIT_BYTES = 56 * 1024 * 1024
INPROJ_VMEM_LIMIT_BYTES = 61 * 1024 * 1024
LANES = 128
SUBLANES = 8
BF16_TILE_ROWS = 2 * SUBLANES

ADA_TN = 2048
BIAS_HEADS_PER_STEP = 2
INPROJ_TM = 512
INPROJ_TN = 256
HGRN_TC = 1024
HGRN_UNROLL = 4
HGRN_ROWS = HGRN_UNROLL * HGRN_CHUNK
HGRN_TRIP_CHUNKS = 16
OUTPROJ_TM = 512
OUTPROJ_TN = 256
OUTPROJ_ROW_SPLIT = 2
FFN_TM = 1024
FFN_TF = 512
FFN_SUB = 256
FFN_DN = 512
ROW_CHUNK = 16


def _params(*semantics, vmem_limit_bytes=VMEM_LIMIT_BYTES):
    return pltpu.CompilerParams(dimension_semantics=semantics, vmem_limit_bytes=vmem_limit_bytes)


def _sigmoid(v):
    return 1.0 / (1.0 + jnp.exp2(v * (-LOG2E)))


def _silu(v):
    return v * _sigmoid(v)


def _dot(a, b):
    return jnp.dot(a, b, preferred_element_type=F32)


def _dot_nt(a, b):
    return lax.dot_general(a, b, (((1,), (1,)), ((), ())), preferred_element_type=F32)


def _dot_tn(a, b):
    return lax.dot_general(a, b, (((0,), (0,)), ((), ())), preferred_element_type=F32)


def _ada_kernel(c_ref, w_ref, b_ref, o_ref):
    c_act = _silu(c_ref[...])
    o_ref[...] = _dot(c_act.astype(BF16), w_ref[...].astype(BF16)) + b_ref[...]


def _ada(c8, w, b):
    n = w.shape[1]
    return pl.pallas_call(
        _ada_kernel,
        grid=(n // ADA_TN,),
        in_specs=[pl.BlockSpec((8, D_MODEL), lambda j: (0, 0)),
                  pl.BlockSpec((D_MODEL, ADA_TN), lambda j: (0, j)),
                  pl.BlockSpec((1, ADA_TN), lambda j: (0, j))],
        out_specs=pl.BlockSpec((8, ADA_TN), lambda j: (0, j)),
        out_shape=jax.ShapeDtypeStruct((8, n), F32),
        compiler_params=_params("arbitrary"),
        name="ada",
    )(c8, w, b)


def _modulated_norm(x, wmod, shift):
    return x * lax.rsqrt(jnp.mean(x * x, axis=-1, keepdims=True) + EPS) * wmod + shift


def _for_row_chunks(n_rows, fn, loop_trips=1):
    per_trip = n_rows // loop_trips
    assert per_trip * loop_trips == n_rows and per_trip % ROW_CHUNK == 0

    def trip(i, carry):
        for r0 in range(0, per_trip, ROW_CHUNK):
            start = r0 if loop_trips == 1 else pl.multiple_of(i * per_trip + r0, ROW_CHUNK)
            fn(pl.ds(start, ROW_CHUNK))
        return carry

    if loop_trips == 1:
        trip(0, 0)
    else:
        lax.fori_loop(0, loop_trips, trip, 0)


def _lane_folded_sumsq(ssq, v):
    sq = v * v
    for k0 in range(0, v.shape[1], LANES):
        ssq = sq[:, k0:k0 + LANES] if ssq is None else ssq + sq[:, k0:k0 + LANES]
    return ssq


def _rsqrt_mean(ssq):
    return lax.rsqrt(jnp.sum(ssq, axis=-1, keepdims=True) * (1.0 / D_MODEL) + EPS)


def _split3(v):
    hi = v.astype(BF16)
    r1 = v - hi.astype(F32)
    mid = r1.astype(BF16)
    lo = (r1 - mid.astype(F32)).astype(BF16)
    return hi, mid, lo


def _hgrn_tril3():
    tril = np.tril(np.ones((HGRN_CHUNK, HGRN_CHUNK), np.float32))
    return jnp.asarray(np.concatenate([tril, tril, tril], axis=1), dtype=BF16)


def _hgrn_prepare(q, f_logit, g, lb, nw, tril3):
    c, nu = HGRN_CHUNK, HGRN_UNROLL

    def per_chunk_rows(x, offset):
        rid = lax.broadcasted_iota(jnp.int32, x.shape, 0)
        out = jnp.broadcast_to(x[offset:offset + 1, :], x.shape)
        for u in range(1, nu):
            out = jnp.where(rid >= u * c, jnp.broadcast_to(x[u * c + offset:u * c + offset + 1, :], x.shape), out)
        return out

    f = lb + (1.0 - lb) * _sigmoid(f_logit)
    k = 1.0 - f
    parts = _split3(jnp.log2(f))
    b = jnp.concatenate(
        [_dot(tril3, jnp.concatenate([p[u * c:(u + 1) * c] for p in parts], axis=0)) for u in range(nu)],
        axis=0)
    b_mid = per_chunk_rows(b, c // 2 - 1)
    b_last = per_chunk_rows(b, c - 1)
    d_mid = b - b_mid
    qr = q * jnp.exp2(d_mid)
    kr = k * jnp.exp2(-d_mid)
    q_dec = (qr * jnp.exp2(b_mid)).astype(BF16)
    k_dec = (kr * jnp.exp2(b_last - b_mid)).astype(BF16)
    decays = [jnp.exp2(b[(u + 1) * c - 1:(u + 1) * c, :]) for u in range(nu)]
    return qr.astype(BF16), kr.astype(BF16), q_dec, k_dec, nw * _silu(g), decays


def _inproj_kernel(x_ref, mod_ref, nw_ref, w_ref, lb_ref, nwh_ref, tril_ref, wa_ref, wb_ref,
                   gate_ref, o16_ref, dec_ref, wa16_ref, wb16_ref, h_ref, *, layer):
    wa16_ref[...] = wa_ref[...].astype(BF16)
    wb16_ref[...] = wb_ref[...].astype(BF16)
    shift = mod_ref[0, 0:1, :]
    wmod = nw_ref[...] * (1.0 + mod_ref[0, 1:2, :])

    def chunk(rows):
        h_ref[rows, :] = _modulated_norm(x_ref[0, rows, :], wmod, shift).astype(BF16)

    _for_row_chunks(INPROJ_TM, chunk)

    lb_rows = [lb_ref[r:r + 1, :] for r in range(lb_ref.shape[0])]
    lb_max = functools.reduce(jnp.maximum, lb_rows)
    lb_exp = [jnp.exp(r - lb_max) for r in lb_rows]
    lb_all = sum(lb_exp[:layer + 1]) / sum(lb_exp)

    tn, hw = INPROJ_TN, HGRN_WIDTH
    p16_col = {name: i * hw for i, name in enumerate(P16_BLOCKS)}

    def project(w_col):
        return _dot(h_ref[...], w_ref[:, w_col:w_col + tn])

    def to_p16(name, c0, value):
        o16_ref[0, :, p16_col[name] + c0:p16_col[name] + c0 + tn] = value.astype(BF16)

    for c0 in range(0, hw, tn):
        cs = slice(c0, c0 + tn)
        hq, hf, hg = project(c0), project(hw + c0), project(3 * hw + c0)
        followers = [lambda: to_p16("v", c0, project(2 * hw + c0)),
                     lambda: to_p16("aq", c0, project(4 * hw + c0))]
        for half in range(INPROJ_TM // HGRN_ROWS):
            rows = slice(half * HGRN_ROWS, (half + 1) * HGRN_ROWS)
            q_rel, k_rel, q_dec, k_dec, gate, decays = _hgrn_prepare(
                hq[rows], hf[rows], hg[rows], lb_all[:, cs], nwh_ref[:, cs], tril_ref[...])
            for name, value in (("q_rel", q_rel), ("k_rel", k_rel), ("q_dec", q_dec), ("k_dec", k_dec)):
                o16_ref[0, rows, p16_col[name] + c0:p16_col[name] + c0 + tn] = value
            gate_ref[0, rows, cs] = gate
            for u, decay in enumerate(decays):
                dec_ref[0, half * HGRN_UNROLL + u:half * HGRN_UNROLL + u + 1, cs] = decay
            n_groups = INPROJ_TM // HGRN_ROWS
            if (half + 1) % (n_groups // 2) == 0:
                followers[(half + 1) // (n_groups // 2) - 1]()
    for c0 in range(0, 2 * KV_WIDTH, tn):
        o16_ref[0, :, P16_KV_COL + c0:P16_KV_COL + c0 + tn] = project(4 * hw + ATTN_WIDTH + c0).astype(BF16)


def _inproj(x, mod, norm_w, w_bf, lower_bounds, hgrn_norm_w, layer, wa, wb):
    b, s, d = x.shape
    tm = INPROJ_TM
    nt = s // tm
    assert tm % (2 * HGRN_ROWS) == 0 and (2 * KV_WIDTH) % INPROJ_TN == 0
    chunks = tm // HGRN_CHUNK
    tril3 = _hgrn_tril3()
    wa_spec, wa_shape = _cast_rider_specs(wa, b * nt, nt)
    wb_spec, wb_shape = _cast_rider_specs(wb, b * nt, nt)
    return pl.pallas_call(
        functools.partial(_inproj_kernel, layer=layer),
        grid=(b, s // tm),
        in_specs=[pl.BlockSpec((1, tm, d), lambda bi, m: (bi, m, 0)),
                  pl.BlockSpec((1, 6, d), lambda bi, m: (bi, 0, 0)),
                  pl.BlockSpec((1, d), lambda bi, m: (0, 0)),
                  pl.BlockSpec(w_bf.shape, lambda bi, m: (0, 0), pipeline_mode=pl.Buffered(1)),
                  pl.BlockSpec(lower_bounds.shape, lambda bi, m: (0, 0)),
                  pl.BlockSpec((1, HGRN_WIDTH), lambda bi, m: (0, 0)),
                  pl.BlockSpec(tril3.shape, lambda bi, m: (0, 0)),
                  wa_spec, wb_spec],
        out_specs=[pl.BlockSpec((1, tm, HGRN_WIDTH), lambda bi, m: (bi, m, 0)),
                   pl.BlockSpec((1, tm, P16_WIDTH), lambda bi, m: (bi, m, 0)),
                   pl.BlockSpec((1, chunks, HGRN_WIDTH), lambda bi, m: (bi, m, 0)),
                   wa_spec, wb_spec],
        out_shape=[jax.ShapeDtypeStruct((b, s, HGRN_WIDTH), F32),
                   jax.ShapeDtypeStruct((b, s, P16_WIDTH), BF16),
                   jax.ShapeDtypeStruct((b, s // HGRN_CHUNK, HGRN_WIDTH), F32),
                   wa_shape, wb_shape],
        scratch_shapes=[pltpu.VMEM((tm, d), BF16)],
        compiler_params=_params("arbitrary", "arbitrary", vmem_limit_bytes=INPROJ_VMEM_LIMIT_BYTES),
        name="inproj",
    )(x, mod, norm_w, w_bf, lower_bounds, hgrn_norm_w, tril3, wa, wb)


def _hgrn_kernel(v_ref, qrel_ref, krel_ref, qdec_ref, kdec_ref, gate_ref, dec_ref, o_ref, st_ref):
    c, dk, nu = HGRN_CHUNK, HGRN_HEAD_DIM, HGRN_TRIP_CHUNKS

    @pl.when(pl.program_id(1) == 0)
    def _():
        st_ref[...] = jnp.zeros_like(st_ref)

    causal = (lax.broadcasted_iota(jnp.int32, (c, c), 0) >= lax.broadcasted_iota(jnp.int32, (c, c), 1))
    heads = [slice(h * dk, (h + 1) * dk) for h in range(HGRN_HEADS)]

    def body(i, carry):
        st = [st_ref[h] for h in range(HGRN_HEADS)]
        for u in range(nu):
            rows = pl.ds(pl.multiple_of((i * nu + u) * c, c), c)
            decay = dec_ref[0, pl.ds(i * nu + u, 1), :]
            a = [jnp.where(causal, _dot_nt(qrel_ref[0, rows, hs], krel_ref[0, rows, hs]), 0.0).astype(BF16)
                 for hs in heads]
            o = [_dot(a[h], v_ref[0, rows, hs]) + _dot(qdec_ref[0, rows, hs], st[h].T.astype(BF16))
                 for h, hs in enumerate(heads)]
            st = [decay[:, hs] * st[h] + _dot_tn(v_ref[0, rows, hs], kdec_ref[0, rows, hs])
                  for h, hs in enumerate(heads)]
            for h, hs in enumerate(heads):
                oh = o[h] * lax.rsqrt(jnp.mean(o[h] * o[h], axis=-1, keepdims=True) + EPS)
                o_ref[0, rows, hs] = (oh * gate_ref[0, rows, hs]).astype(o_ref.dtype)
        for h in range(HGRN_HEADS):
            st_ref[h] = st[h]
        return carry

    lax.fori_loop(0, HGRN_TC // (nu * c), body, 0)


def _hgrn(gate, p16, dec):
    b, s, _ = gate.shape
    nt = s // HGRN_TC
    blk = lambda name: pl.BlockSpec((1, HGRN_TC, HGRN_WIDTH),
                                    lambda bi, t, j=P16_BLOCKS.index(name): (bi, t, j))
    return pl.pallas_call(
        _hgrn_kernel,
        grid=(b, nt),
        in_specs=[blk("v"), blk("q_rel"), blk("k_rel"), blk("q_dec"), blk("k_dec"),
                  pl.BlockSpec((1, HGRN_TC, HGRN_WIDTH), lambda bi, t: (bi, t, 0)),
                  pl.BlockSpec((1, HGRN_TC // HGRN_CHUNK, HGRN_WIDTH), lambda bi, t: (bi, t, 0))],
        out_specs=pl.BlockSpec((1, HGRN_TC, HGRN_WIDTH), lambda bi, t: (bi, t, 0)),
        out_shape=jax.ShapeDtypeStruct((b, s, HGRN_WIDTH), BF16),
        scratch_shapes=[pltpu.VMEM((HGRN_HEADS, HGRN_HEAD_DIM, HGRN_HEAD_DIM), F32)],
        compiler_params=_params("arbitrary", "arbitrary"),
        name="hgrn",
    )(p16, p16, p16, p16, p16, gate, dec)


def _t5_causal_buckets(dist):
    max_exact = REL_BUCKETS // 2
    d = np.maximum(dist, 0)
    log_b = max_exact + (np.log(np.maximum(d, 1) / max_exact)
                         / np.log(REL_MAX_DIST / max_exact)
                         * (REL_BUCKETS - max_exact)).astype(np.int32)
    log_b = np.minimum(log_b, REL_BUCKETS - 1)
    return np.where(d < max_exact, d, log_b).astype(np.int32)


def _bias_kernel(tab_ref, sink_ref, bucket_ref, valid_ref, w_ref, o_ref, w16_ref):
    w16_ref[...] = w_ref[...].astype(BF16)
    bucket = bucket_ref[...]
    sink_col = lax.broadcasted_iota(jnp.int32, bucket.shape, 1) == 0
    for i in range(BIAS_HEADS_PER_STEP):
        h = pl.program_id(0) * BIAS_HEADS_PER_STEP + i
        acc = jnp.zeros(bucket.shape, F32)
        for bk in range(REL_BUCKETS):
            acc = jnp.where(bucket == bk, tab_ref[bk, h], acc)
        for variant in range(2):
            masked = jnp.where(valid_ref[variant] != 0, acc, NEG_INF)
            o_ref[variant, i] = jnp.where(sink_col, sink_ref[h], masked)


def _bias_table(rel_table, sinks, w):
    l = ATTN_BLOCK
    qi = np.arange(l)[:, None]
    kj = np.arange(2 * l)[None, :]
    dist = qi + l - kj
    in_window = (dist >= 0) & (dist < WINDOW)
    valid = np.stack([in_window & (kj >= l), in_window]).astype(np.int32)
    assert not valid[:, :, 0].any()
    bucket = _t5_causal_buckets(dist)
    steps = ATTN_Q_HEADS // BIAS_HEADS_PER_STEP
    w_rows = w.shape[0] // steps
    assert steps * BIAS_HEADS_PER_STEP == ATTN_Q_HEADS
    assert w_rows * steps == w.shape[0] and w_rows % BF16_TILE_ROWS == 0
    w_spec = pl.BlockSpec((w_rows, w.shape[1]), lambda h: (h, 0))
    return pl.pallas_call(
        _bias_kernel,
        grid=(steps,),
        in_specs=[pl.BlockSpec(memory_space=pltpu.SMEM),
                  pl.BlockSpec(memory_space=pltpu.SMEM),
                  pl.BlockSpec((l, 2 * l), lambda h: (0, 0)),
                  pl.BlockSpec((2, l, 2 * l), lambda h: (0, 0, 0)),
                  w_spec],
        out_specs=[pl.BlockSpec((2, BIAS_HEADS_PER_STEP, l, 2 * l), lambda h: (0, h, 0, 0)), w_spec],
        out_shape=[jax.ShapeDtypeStruct((2, ATTN_Q_HEADS, l, 2 * l), F32),
                   jax.ShapeDtypeStruct(w.shape, BF16)],
        compiler_params=_params("arbitrary"),
        name="bias",
    )(rel_table, sinks, jnp.asarray(bucket), jnp.asarray(valid), w)


def _zero_first_row(a):
    top = a[:BF16_TILE_ROWS]
    row = lax.broadcasted_iota(jnp.int32, top.shape, 0)
    return jnp.concatenate([jnp.where(row == 0, 0.0, top).astype(a.dtype), a[BF16_TILE_ROWS:]], axis=0)


def _attn_kernel(q_ref, kp_ref, kc_ref, vp_ref, vc_ref, bias_ref, wa_ref, wb_ref,
                 o_ref, wa16_ref, wb16_ref):
    l = ATTN_BLOCK
    dh = ATTN_HEAD_DIM
    wa16_ref[...] = wa_ref[...].astype(BF16)
    wb16_ref[...] = wb_ref[...].astype(BF16)
    lane = lax.broadcasted_iota(jnp.int32, (1, 2 * dh), 1)
    low = lane < dh
    first = jnp.minimum(pl.program_id(1), 1)

    for pair in range(ATTN_KV_HEADS // 2):
        pc = slice(pair * 2 * dh, (pair + 1) * 2 * dh)
        kk = pltpu.bitcast(jnp.concatenate([kp_ref[0, :, pc], kc_ref[0, :, pc]], axis=0), jnp.int32)
        vv = pltpu.bitcast(jnp.concatenate([vp_ref[0, :, pc], vc_ref[0, :, pc]], axis=0), jnp.int32)
        kk_sw = pltpu.roll(kk, dh, axis=1)
        vv_sw = pltpu.roll(vv, dh, axis=1)
        for sub in range(2):
            kvh = pair * 2 + sub
            if sub == 0:
                k2_all = jnp.where(low, kk, kk_sw)
                v2_all = jnp.where(low, vv, vv_sw)
            else:
                k2_all = jnp.where(low, kk_sw, kk)
                v2_all = jnp.where(low, vv_sw, vv)
            k2_all = pltpu.bitcast(k2_all, BF16)
            v2_all = pltpu.bitcast(v2_all, BF16)
            for t in range(ATTN_QB):
                qrows = slice(t * l, (t + 1) * l)
                k2 = _zero_first_row(k2_all[t * l:(t + 2) * l])
                v2 = _zero_first_row(v2_all[t * l:(t + 2) * l])
                qs = []
                for j in range(ATTN_GROUP):
                    hq = kvh * ATTN_GROUP + j
                    qc = slice((hq // 2) * 2 * dh, (hq // 2 + 1) * 2 * dh)
                    q2 = q_ref[0, qrows, qc] * (dh ** -0.5)
                    keep = low if hq % 2 == 0 else jnp.logical_not(low)
                    qs.append(jnp.where(keep, q2, 0.0).astype(BF16))
                s_all = _dot_nt(jnp.concatenate(qs, axis=0), k2)
                ps = []
                for j in range(ATTN_GROUP):
                    hq = kvh * ATTN_GROUP + j
                    bias = bias_ref[first, hq] if t == 0 else bias_ref[1, hq]
                    s = s_all[j * l:(j + 1) * l] + bias
                    p = jnp.exp(s - jnp.max(s, axis=-1, keepdims=True))
                    ps.append(p.astype(BF16))
                v2x = jnp.concatenate([v2, jnp.ones_like(v2)], axis=1)
                o_all = _dot(jnp.concatenate(ps, axis=0), v2x)
                o_all = o_all[:, :2 * dh] / o_all[:, 2 * dh:]
                for jp in range(ATTN_GROUP // 2):
                    hq = kvh * ATTN_GROUP + 2 * jp
                    oe = o_all[(2 * jp) * l:(2 * jp + 1) * l]
                    oo = o_all[(2 * jp + 1) * l:(2 * jp + 2) * l]
                    oc = slice((hq // 2) * 2 * dh, (hq // 2 + 1) * 2 * dh)
                    o_ref[0, qrows, oc] = jnp.where(low, oe, oo).astype(o_ref.dtype)


def _cast_rider_specs(w, steps, steps_per_batch):
    rows = w.shape[0] // steps
    assert rows * steps == w.shape[0] and rows % BF16_TILE_ROWS == 0
    spec = pl.BlockSpec((rows, w.shape[1]), lambda bi, n: (bi * steps_per_batch + n, 0))
    return spec, jax.ShapeDtypeStruct(w.shape, BF16)


def _attn(p16, bias, wa, wb):
    b, s, _ = p16.shape
    l = ATTN_BLOCK
    tq = ATTN_QB * l
    nq = s // tq
    q_blk = P16_BLOCKS.index("aq")
    k_blk = P16_KV_COL // KV_WIDTH
    v_blk = k_blk + 1
    prev_spec = lambda blk: pl.BlockSpec(
        (1, l, KV_WIDTH), lambda bi, n: (bi, jnp.maximum(n * ATTN_QB - 1, 0), blk))
    cur_spec = lambda blk: pl.BlockSpec((1, tq, KV_WIDTH), lambda bi, n: (bi, n, blk))
    wa_spec, wa_shape = _cast_rider_specs(wa, b * nq, nq)
    wb_spec, wb_shape = _cast_rider_specs(wb, b * nq, nq)
    return pl.pallas_call(
        _attn_kernel,
        grid=(b, nq),
        in_specs=[pl.BlockSpec((1, tq, ATTN_WIDTH), lambda bi, n: (bi, n, q_blk)),
                  prev_spec(k_blk), cur_spec(k_blk), prev_spec(v_blk), cur_spec(v_blk),
                  pl.BlockSpec(bias.shape, lambda bi, n: (0, 0, 0, 0), pipeline_mode=pl.Buffered(1)),
                  wa_spec, wb_spec],
        out_specs=[pl.BlockSpec((1, tq, ATTN_WIDTH), lambda bi, n: (bi, n, 0)), wa_spec, wb_spec],
        out_shape=[jax.ShapeDtypeStruct((b, s, ATTN_WIDTH), BF16), wa_shape, wb_shape],
        compiler_params=_params("arbitrary", "arbitrary"),
        name="attn",
    )(p16, p16, p16, p16, p16, bias, wa, wb)


def _outproj_kernel(orec_ref, oatt_ref, x_ref, mod_ref, nw_ref, w_ref, x1_ref, h2_ref, rs_ref):
    shift = mod_ref[0, 3:4, :]
    wmod = nw_ref[...] * (1.0 + mod_ref[0, 4:5, :])
    group = OUTPROJ_TM // OUTPROJ_ROW_SPLIT
    for r0 in range(0, OUTPROJ_TM, group):
        rg = slice(r0, r0 + group)
        ssq = None
        for c0 in range(0, D_MODEL, OUTPROJ_TN):
            cs = slice(c0, c0 + OUTPROJ_TN)
            y = (_dot(orec_ref[0, rg, :], w_ref[0:HGRN_WIDTH, cs])
                 + _dot(oatt_ref[0, rg, :], w_ref[HGRN_WIDTH:, cs]))
            x1 = x_ref[0, rg, cs] + mod_ref[0, 2:3, cs] * y
            x1_ref[0, rg, cs] = x1
            ssq = _lane_folded_sumsq(ssq, x1)
        rs_ref[rg, :] = _rsqrt_mean(ssq)

        def chunk(rows, r0=r0):
            rr = pl.ds(r0 + rows.start, rows.size)
            h2_ref[0, rr, :] = (x1_ref[0, rr, :] * rs_ref[rr, :] * wmod + shift).astype(BF16)

        _for_row_chunks(group, chunk)


def _outproj(o_rec, o_att, x, mod, norm_w, w_bf):
    b, s, d = x.shape
    tm = OUTPROJ_TM
    return pl.pallas_call(
        _outproj_kernel,
        grid=(b, s // tm),
        in_specs=[pl.BlockSpec((1, tm, HGRN_WIDTH), lambda bi, m: (bi, m, 0)),
                  pl.BlockSpec((1, tm, ATTN_WIDTH), lambda bi, m: (bi, m, 0)),
                  pl.BlockSpec((1, tm, d), lambda bi, m: (bi, m, 0)),
                  pl.BlockSpec((1, 6, d), lambda bi, m: (bi, 0, 0)),
                  pl.BlockSpec((1, d), lambda bi, m: (0, 0)),
                  pl.BlockSpec(w_bf.shape, lambda bi, m: (0, 0), pipeline_mode=pl.Buffered(1))],
        out_specs=[pl.BlockSpec((1, tm, d), lambda bi, m: (bi, m, 0)),
                   pl.BlockSpec((1, tm, d), lambda bi, m: (bi, m, 0))],
        out_shape=[jax.ShapeDtypeStruct((b, s, d), F32),
                   jax.ShapeDtypeStruct((b, s, d), BF16)],
        scratch_shapes=[pltpu.VMEM((tm, 1), F32)],
        compiler_params=_params("parallel", "parallel"),
        name="outproj",
    )(o_rec, o_att, x, mod, norm_w, w_bf)


def _ffn_kernel(h_ref, x1_hbm, mod_ref, fw_ref, wg_ref, wu_ref, wd_ref, o_ref, x1_buf, rs_ref, x1_sem,
                *, final_norm):
    bi, m, j = pl.program_id(0), pl.program_id(1), pl.program_id(2)

    def x1_copy():
        rows = pl.ds(pl.multiple_of(m * FFN_TM, FFN_TM), FFN_TM)
        return pltpu.make_async_copy(x1_hbm.at[bi, rows, :], x1_buf, x1_sem)

    last = pl.num_programs(2) - 1
    ksubs = [slice(k0, k0 + FFN_SUB) for k0 in range(0, FFN_TF, FFN_SUB)]
    pieces = [slice(c0, c0 + FFN_DN) for c0 in range(0, D_MODEL, FFN_DN)]

    def acts():
        return [(_silu(_dot(h_ref[0], wg_ref[:, ks])) * _dot(h_ref[0], wu_ref[:, ks])).astype(BF16)
                for ks in ksubs]

    def down(a, cs):
        return functools.reduce(lambda x, y: x + y, [_dot(ai, wd_ref[ks, cs]) for ai, ks in zip(a, ksubs)])

    @pl.when(j == 0)
    def _():
        x1_copy().start()
        a = acts()
        for cs in pieces:
            o_ref[0, :, cs] = down(a, cs)

    @pl.when(jnp.logical_and(j > 0, j < last))
    def _():
        a = acts()
        for cs in pieces:
            o_ref[0, :, cs] += down(a, cs)

    @pl.when(j == last)
    def _():
        x1_copy().wait()
        a = acts()
        ssq = None
        for cs in pieces:
            x2 = x1_buf[:, cs] + mod_ref[0, 5:6, cs] * (o_ref[0, :, cs] + down(a, cs))
            x1_buf[:, cs] = x2
            if final_norm:
                ssq = _lane_folded_sumsq(ssq, x2)
        if final_norm:
            rs_ref[...] = _rsqrt_mean(ssq)
        fw = fw_ref[...]

        def chunk(rows):
            x2 = x1_buf[rows, :]
            o_ref[0, rows, :] = x2 * rs_ref[rows, :] * fw if final_norm else x2

        _for_row_chunks(FFN_TM, chunk, loop_trips=2)


def _ffn(h2, x1, mod, final_w, wg, wu, wd, final_norm):
    b, s, d = x1.shape
    tm, tf = FFN_TM, FFN_TF
    return pl.pallas_call(
        functools.partial(_ffn_kernel, final_norm=final_norm),
        grid=(b, s // tm, D_FF // tf),
        in_specs=[pl.BlockSpec((1, tm, d), lambda bi, m, j: (bi, m, 0)),
                  pl.BlockSpec(memory_space=pl.ANY),
                  pl.BlockSpec((1, 6, d), lambda bi, m, j: (bi, 0, 0)),
                  pl.BlockSpec((1, d), lambda bi, m, j: (0, 0)),
                  pl.BlockSpec((d, tf), lambda bi, m, j: (0, j)),
                  pl.BlockSpec((d, tf), lambda bi, m, j: (0, j)),
                  pl.BlockSpec((tf, d), lambda bi, m, j: (j, 0))],
        out_specs=pl.BlockSpec((1, tm, d), lambda bi, m, j: (bi, m, 0)),
        out_shape=jax.ShapeDtypeStruct((b, s, d), F32),
        scratch_shapes=[pltpu.VMEM((tm, d), F32), pltpu.VMEM((tm, 1), F32), pltpu.SemaphoreType.DMA(())],
        compiler_params=_params("arbitrary", "arbitrary", "arbitrary"),
        name="ffn",
    )(h2, x1, mod, final_w, wg, wu, wd)


def kernel(x, c, w_ada, b_ada, norm1_w, w_in, lower_bounds, hgrn_norm_w, attn_sinks,
           rel_bias_table, w_out, norm2_w, w_gate, w_up, w_down, final_norm_w):
    b, s, d = x.shape
    assert (d, w_in.shape[-1], w_gate.shape[-1]) == (D_MODEL, IN_WIDTH, D_FF)
    assert s % max(INPROJ_TM, HGRN_TC, OUTPROJ_TM, FFN_TM, ATTN_QB * ATTN_BLOCK) == 0
    depth = w_ada.shape[0]
    c8 = jnp.pad(c, ((0, 8 - b), (0, 0)))
    for layer in range(depth):
        bias, wi16 = _bias_table(rel_bias_table, attn_sinks[layer], w_in[layer])
        mod = _ada(c8, w_ada[layer], b_ada[layer][None, :])[:b].reshape(b, 6, d)
        gate, p16, dec, wd16, wo16 = _inproj(x, mod, norm1_w[layer][None, :], wi16, lower_bounds,
                                             hgrn_norm_w[layer][None, :], layer,
                                             w_down[layer], w_out[layer])
        o_rec = _hgrn(gate, p16, dec)
        o_att, wg16, wu16 = _attn(p16, bias, w_gate[layer], w_up[layer])
        x1, h2 = _outproj(o_rec, o_att, x, mod, norm2_w[layer][None, :], wo16)
        x = _ffn(h2, x1, mod, final_norm_w[None, :], wg16, wu16, wd16,
                 final_norm=(layer == depth - 1))
    return x
```
